```python
import functools
import jax, jax.numpy as jnp
from jax import lax
import numpy as np

D_MODEL = 4096
BATCH = 2
SEQ = 4096
DEPTH = 1
DEC_BATCH = 128
DEC_SEQ = 8
PAST_LEN = 8192
PAGE_SIZE = 128

ATTN_WIDTH = D_MODEL // 2
D_CONV = D_MODEL - ATTN_WIDTH
HEAD_DIM = 64
N_Q_HEADS = ATTN_WIDTH // HEAD_DIM
N_KV_HEADS = max(1, N_Q_HEADS // 8)
GROUP = N_Q_HEADS // N_KV_HEADS
KV_DIM = N_KV_HEADS * HEAD_DIM
WINDOW = 128
BLOCK = 128
CONV_K = 3
D_FF = ((8 * D_MODEL // 3 + 255) // 256) * 256
N_IN = ATTN_WIDTH + 2 * KV_DIM + 3 * D_CONV
N_MOD = 6
EPS = 1e-5

kernel_name = "hybrid_swa_sink_shortconv_convffn_adaln_step"


def _rmsnorm(x, g):
    xf = x.astype(jnp.float32)
    y = xf * lax.rsqrt(jnp.mean(xf * xf, axis=-1, keepdims=True) + EPS)
    return (y * g.astype(jnp.float32)).astype(x.dtype)


def _causal_dwconv(u, prev, w):
    t = u.shape[1]
    full = jnp.concatenate([prev.astype(u.dtype), u], axis=1)
    y = full[:, 0:t] * w[0]
    for j in range(1, CONV_K):
        y = y + full[:, j:j + t] * w[j]
    return y, full[:, full.shape[1] - (CONV_K - 1):]


def _sink_softmax(s, mask, sinks):
    sink = sinks.astype(jnp.float32).reshape(N_KV_HEADS, GROUP, 1, 1)
    s = jnp.where(mask, s, -jnp.inf)
    m = jnp.maximum(jnp.max(s, axis=-1, keepdims=True), sink)
    p = jnp.exp(s - m)
    return p / (jnp.sum(p, axis=-1, keepdims=True) + jnp.exp(sink - m))


def _prompt_attention(q, k, v, sinks):
    n, t = q.shape[0], q.shape[1]
    nb = t // BLOCK
    qb = q.reshape(n, nb, BLOCK, N_KV_HEADS, GROUP, HEAD_DIM)
    kb = k.reshape(n, nb, BLOCK, N_KV_HEADS, HEAD_DIM)
    vb = v.reshape(n, nb, BLOCK, N_KV_HEADS, HEAD_DIM)

    def with_prev(a):
        prev = jnp.concatenate([jnp.zeros_like(a[:, :1]), a[:, :-1]], axis=1)
        return jnp.concatenate([prev, a], axis=2)

    kk, vv = with_prev(kb), with_prev(vb)
    s = jnp.einsum("nbqkgd,nbskd->nbkgqs", qb, kk).astype(jnp.float32) * (HEAD_DIM ** -0.5)
    blk = jnp.arange(nb)[:, None, None] * BLOCK
    qpos = blk + jnp.arange(BLOCK)[None, :, None]
    kpos = blk - BLOCK + jnp.arange(2 * BLOCK)[None, None, :]
    diff = qpos - kpos
    mask = (diff >= 0) & (diff <= WINDOW) & (kpos >= 0)
    p = _sink_softmax(s, mask[:, None, None], sinks).astype(v.dtype)
    o = jnp.einsum("nbkgqs,nbskd->nbqkgd", p, vv).reshape(n, t, N_Q_HEADS * HEAD_DIM)
    keep = min(WINDOW, t)
    return o, k[:, t - keep:], v[:, t - keep:]


def _sample_attention(q, k, v, k_buf, v_buf, sinks):
    n, t = q.shape[0], q.shape[1]
    w = k_buf.shape[1]
    kk = jnp.concatenate([k_buf.astype(k.dtype), k], axis=1)
    vv = jnp.concatenate([v_buf.astype(v.dtype), v], axis=1)
    qg = q.reshape(n, t, N_KV_HEADS, GROUP, HEAD_DIM)
    s = jnp.einsum("nqkgd,nskd->nkgqs", qg, kk).astype(jnp.float32) * (HEAD_DIM ** -0.5)
    qpos = PAST_LEN + jnp.arange(t)
    kpos = jnp.concatenate([PAST_LEN - w + jnp.arange(w), PAST_LEN + jnp.arange(t)])
    diff = qpos[:, None] - kpos[None, :]
    mask = (diff >= 0) & (diff <= WINDOW)
    p = _sink_softmax(s, mask, sinks).astype(v.dtype)
    o = jnp.einsum("nkgqs,nskd->nqkgd", p, vv).reshape(n, t, N_Q_HEADS * HEAD_DIM)
    return o, kk[:, kk.shape[1] - w:], vv[:, vv.shape[1] - w:]


def _layer(x, c, conv_prev, ffn_prev, attend, w_mod, b_mod, g_mix, g_ffn, w_in, conv_w,
           w_out, w_gate_up, ffn_conv_w, w_down):
    n, t, _ = x.shape
    mod = (jax.nn.silu(c) @ w_mod + b_mod).reshape(n, N_MOD, 1, D_MODEL)
    shift_m, scale_m, gate_m = mod[:, 0], mod[:, 1], mod[:, 2]
    shift_f, scale_f, gate_f = mod[:, 3], mod[:, 4], mod[:, 5]

    h = _rmsnorm(x, g_mix) * (1 + scale_m) + shift_m
    proj = h @ w_in
    cuts = [ATTN_WIDTH, ATTN_WIDTH + KV_DIM, ATTN_WIDTH + 2 * KV_DIM,
            ATTN_WIDTH + 2 * KV_DIM + D_CONV, ATTN_WIDTH + 2 * KV_DIM + 2 * D_CONV]
    q, k, v, b_gate, c_gate, h_conv = jnp.split(proj, cuts, axis=-1)
    attn_out, k_state, v_state = attend(
        q.reshape(n, t, N_Q_HEADS, HEAD_DIM),
        k.reshape(n, t, N_KV_HEADS, HEAD_DIM),
        v.reshape(n, t, N_KV_HEADS, HEAD_DIM))
    y_conv, conv_state = _causal_dwconv(c_gate * h_conv, conv_prev, conv_w)
    mixed = jnp.concatenate([attn_out, b_gate * y_conv], axis=-1) @ w_out
    x = x + gate_m * mixed

    h = _rmsnorm(x, g_ffn) * (1 + scale_f) + shift_f
    gate_pre, up = jnp.split(h @ w_gate_up, [D_FF], axis=-1)
    g_conv, ffn_state = _causal_dwconv(gate_pre, ffn_prev, ffn_conv_w)
    x = x + gate_f * ((jax.nn.silu(g_conv) * up) @ w_down)
    return x, k_state, v_state, conv_state, ffn_state


def setup_inputs(seed: int = 0) -> dict:
    key = jax.random.key(seed)
    ks = jax.random.split(key, 20)
    f32 = jnp.float32

    def nrm(k, shape, scale):
        return jax.random.normal(k, shape, f32) * scale

    wb = min(WINDOW, PAST_LEN)
    return {
        "x_prompt": nrm(ks[0], (BATCH, SEQ, D_MODEL), 1.0),
        "x_sample": nrm(ks[1], (DEC_BATCH, DEC_SEQ, D_MODEL), 1.0),
        "cache_k_win": nrm(ks[2], (DEPTH, DEC_BATCH, wb, N_KV_HEADS, HEAD_DIM), 1.0),
        "cache_v_win": nrm(ks[3], (DEPTH, DEC_BATCH, wb, N_KV_HEADS, HEAD_DIM), 1.0),
        "state_conv": nrm(ks[4], (DEPTH, DEC_BATCH, CONV_K - 1, D_CONV), 1.0),
        "state_ffn_conv": nrm(ks[5], (DEPTH, DEC_BATCH, CONV_K - 1, D_FF), 1.0),
        "c_prompt": nrm(ks[6], (BATCH, D_MODEL), 1.0),
        "c_sample": nrm(ks[7], (DEC_BATCH, D_MODEL), 1.0),
        "w_mod": nrm(ks[8], (DEPTH, D_MODEL, N_MOD * D_MODEL), 0.5 * D_MODEL ** -0.5),
        "b_mod": nrm(ks[9], (DEPTH, N_MOD * D_MODEL), 0.02),
        "g_mix": 1.0 + nrm(ks[10], (DEPTH, D_MODEL), 0.05),
        "g_ffn": 1.0 + nrm(ks[11], (DEPTH, D_MODEL), 0.05),
        "w_in": nrm(ks[12], (DEPTH, D_MODEL, N_IN), D_MODEL ** -0.5),
        "conv_w": nrm(ks[13], (DEPTH, CONV_K, D_CONV), CONV_K ** -0.5),
        "sinks": nrm(ks[14], (DEPTH, N_Q_HEADS), 1.0),
        "w_out": nrm(ks[15], (DEPTH, ATTN_WIDTH + D_CONV, D_MODEL), (ATTN_WIDTH + D_CONV) ** -0.5),
        "w_gate_up": nrm(ks[16], (DEPTH, D_MODEL, 2 * D_FF), D_MODEL ** -0.5),
        "ffn_conv_w": nrm(ks[17], (DEPTH, CONV_K, D_FF), CONV_K ** -0.5),
        "w_down": nrm(ks[18], (DEPTH, D_FF, D_MODEL), D_FF ** -0.5),
        "g_final": 1.0 + nrm(ks[19], (D_MODEL,), 0.05),
    }


def reference(x_prompt, x_sample, cache_k_win, cache_v_win, state_conv, state_ffn_conv,
              c_prompt, c_sample, w_mod, b_mod, g_mix, g_ffn, w_in, conv_w, sinks, w_out,
              w_gate_up, ffn_conv_w, w_down, g_final):
    xp, xs = x_prompt, x_sample
    kp_l, vp_l, cp_l, fp_l = [], [], [], []
    ks_l, vs_l, cs_l, fs_l = [], [], [], []
    for l in range(DEPTH):
        weights = (w_mod[l], b_mod[l], g_mix[l], g_ffn[l], w_in[l], conv_w[l], w_out[l],
                   w_gate_up[l], ffn_conv_w[l], w_down[l])
        conv0 = jnp.zeros((xp.shape[0], CONV_K - 1, D_CONV), xp.dtype)
        ffn0 = jnp.zeros((xp.shape[0], CONV_K - 1, D_FF), xp.dtype)
        xp, kp, vp, cp, fp = _layer(
            xp, c_prompt, conv0, ffn0,
            functools.partial(_prompt_attention, sinks=sinks[l]), *weights)
        xs, ksn, vsn, csn, fsn = _layer(
            xs, c_sample, state_conv[l], state_ffn_conv[l],
            functools.partial(_sample_attention, k_buf=cache_k_win[l], v_buf=cache_v_win[l],
                              sinks=sinks[l]), *weights)
        kp_l.append(kp); vp_l.append(vp); cp_l.append(cp); fp_l.append(fp)
        ks_l.append(ksn); vs_l.append(vsn); cs_l.append(csn); fs_l.append(fsn)
    y_prompt = _rmsnorm(xp, g_final)
    y_sample = _rmsnorm(xs, g_final)
    return (y_prompt, y_sample,
            jnp.stack(kp_l), jnp.stack(vp_l), jnp.stack(cp_l), jnp.stack(fp_l),
            jnp.stack(ks_l), jnp.stack(vs_l), jnp.stack(cs_l), jnp.stack(fs_l))
```

```python
import functools

import jax
import jax.numpy as jnp
from jax import lax
from jax.experimental import pallas as pl
from jax.experimental.pallas import tpu as pltpu

F32 = jnp.float32
BF16 = jnp.bfloat16

HEAD_DIM = 64
GROUP = 8
WINDOW = 128
BLOCK = 128
CONV_K = 3
N_MOD = 6
EPS = 1e-5
SUBLANES = 8
MIB = 1024 * 1024
VMEM_CAP = 56 * MIB


def _params(semantics, vmem_bytes):
    return pltpu.CompilerParams(dimension_semantics=semantics,
                                vmem_limit_bytes=int(min(max(vmem_bytes, 16 * MIB), VMEM_CAP)))


def _dot(a, b):
    return jnp.dot(a, b, preferred_element_type=F32)


def _dot_nt(a, b):
    return lax.dot_general(a, b, (((1,), (1,)), ((), ())), preferred_element_type=F32)


def _mod_body(c_ref, w_ref, b_ref, o_ref):
    c = c_ref[...]
    a = (c * jax.nn.sigmoid(c)).astype(BF16)
    o_ref[...] = _dot(a, w_ref[...].astype(BF16)) + b_ref[...]


def _mod(c_all, w_mod, b_mod, tn=512):
    nc, d = c_all.shape
    n = w_mod.shape[1]
    vmem = 2 * (nc * d * 4 + d * tn * 4 + nc * tn * 4) + d * tn * 2 + nc * d * 4
    return pl.pallas_call(
        _mod_body,
        grid=(n // tn,),
        in_specs=[pl.BlockSpec((nc, d), lambda j: (0, 0)),
                  pl.BlockSpec((d, tn), lambda j: (0, j)),
                  pl.BlockSpec((1, tn), lambda j: (0, j))],
        out_specs=pl.BlockSpec((nc, tn), lambda j: (0, j)),
        out_shape=jax.ShapeDtypeStruct((nc, n), F32),
        compiler_params=_params(("parallel",), vmem + 4 * MIB),
        name="mod",
    )(c_all, w_mod, b_mod.reshape(1, n))


def _rms(x):
    return x * lax.rsqrt(jnp.mean(x * x, axis=-1, keepdims=True) + EPS)


def _norm_mod_body(x_ref, mod_ref, g_ref, o_ref, *, shift_idx, scale_idx):
    x = x_ref[...]
    m = mod_ref[...]
    y = _rms(x) * g_ref[...]
    h = y * (1.0 + m[:, scale_idx:scale_idx + 1, :]) + m[:, shift_idx:shift_idx + 1, :]
    o_ref[...] = h.reshape(o_ref.shape).astype(o_ref.dtype)


def _norm_mod(x3, mod3, g, shift_idx, scale_idx, bn, tt):
    n, t, d = x3.shape
    nt = t // tt
    vmem = 2 * (bn * tt * d * 4 + bn * tt * d * 2 + bn * SUBLANES * d * 4) + 3 * bn * tt * d * 4
    return pl.pallas_call(
        functools.partial(_norm_mod_body, shift_idx=shift_idx, scale_idx=scale_idx),
        grid=(n // bn, nt),
        in_specs=[pl.BlockSpec((bn, tt, d), lambda i, j: (i, j, 0)),
                  pl.BlockSpec((bn, N_MOD, d), lambda i, j: (i, 0, 0)),
                  pl.BlockSpec((1, d), lambda i, j: (0, 0))],
        out_specs=pl.BlockSpec((bn * tt, d), lambda i, j: (i * nt + j, 0)),
        out_shape=jax.ShapeDtypeStruct((n * t, d), BF16),
        compiler_params=_params(("parallel", "parallel"), vmem),
        name="norm_mod",
    )(x3, mod3, g.reshape(1, d))


def _final_norm_body(x_ref, g_ref, o_ref):
    o_ref[...] = _rms(x_ref[...]) * g_ref[...]


def _final_norm(x2, g, tm=256):
    m, d = x2.shape
    tm = min(tm, m)
    return pl.pallas_call(
        _final_norm_body,
        grid=(m // tm,),
        in_specs=[pl.BlockSpec((tm, d), lambda i: (i, 0)),
                  pl.BlockSpec((1, d), lambda i: (0, 0))],
        out_specs=pl.BlockSpec((tm, d), lambda i: (i, 0)),
        out_shape=jax.ShapeDtypeStruct((m, d), F32),
        compiler_params=_params(("parallel",), 6 * tm * d * 4),
        name="final_norm",
    )(x2, g.reshape(1, d))


def _matmul_body(a_ref, w_ref, o_ref):
    o_ref[...] = _dot(a_ref[...], w_ref[...]).astype(o_ref.dtype)


def _matmul(a, w, n_cols, tm, tn, out_dtype=F32):
    m, k = a.shape
    vmem = 2 * (tm * k * 2 + k * tn * 2 + tm * tn * 4) + tm * tn * 4
    return pl.pallas_call(
        _matmul_body,
        grid=(m // tm, n_cols // tn),
        in_specs=[pl.BlockSpec((tm, k), lambda i, j: (i, 0)),
                  pl.BlockSpec((k, tn), lambda i, j: (0, j))],
        out_specs=pl.BlockSpec((tm, tn), lambda i, j: (i, j)),
        out_shape=jax.ShapeDtypeStruct((m, n_cols), out_dtype),
        compiler_params=_params(("parallel", "parallel"), vmem + 4 * MIB),
        name="qkv",
    )(a, w)


def _conv_rows(v, w, buf_ref, carry_ref, j, first):
    tm = v.shape[0]

    @pl.when(first)
    def _():
        buf_ref[0:SUBLANES, :] = jnp.zeros((SUBLANES, v.shape[1]), F32)

    @pl.when(jnp.logical_not(first))
    def _():
        buf_ref[0:SUBLANES, :] = carry_ref[j]

    buf_ref[SUBLANES:SUBLANES + tm, :] = v
    v1 = buf_ref[SUBLANES - 1:SUBLANES - 1 + tm, :]
    v2 = buf_ref[SUBLANES - 2:SUBLANES - 2 + tm, :]
    tail = v[tm - SUBLANES:tm, :]
    carry_ref[j] = tail
    return w[0:1, :] * v2 + w[1:2, :] * v1 + w[2:3, :] * v, tail


def _conv_groups(v, w, st):
    tm, tn = v.shape
    nb = tm // SUBLANES
    v3 = v.reshape(nb, SUBLANES, tn)
    t = lax.broadcasted_iota(jnp.int32, (nb, SUBLANES, tn), 1)
    p0 = st[:, 0:1, :]
    p1 = st[:, 1:2, :]
    v1 = jnp.where(t == 0, p1, pltpu.roll(v3, 1, axis=1))
    v2 = jnp.where(t == 0, p0, jnp.where(t == 1, p1, pltpu.roll(v3, 2, axis=1)))
    y = w[0:1, :] * v2 + w[1:2, :] * v1 + w[2:3, :] * v3
    return y.reshape(tm, tn), v3[:, SUBLANES - (CONV_K - 1):, :]


def _silu(x):
    return x * jax.nn.sigmoid(x)


def _gated_prompt_body(*refs, n_w, tiles_per_seq):
    h_ref = refs[0]
    w_refs = refs[1:1 + n_w]
    cw_ref, o_ref, tail_ref, buf_ref, carry_ref = refs[1 + n_w:]
    i, j = pl.program_id(0), pl.program_id(1)
    h = h_ref[...]
    first = (i % tiles_per_seq) == 0
    if n_w == 3:
        v = _dot(h, w_refs[1][...]) * _dot(h, w_refs[2][...])
        y, tail = _conv_rows(v, cw_ref[...], buf_ref, carry_ref, j, first)
        out = _dot(h, w_refs[0][...]) * y
    else:
        v = _dot(h, w_refs[0][...])
        y, tail = _conv_rows(v, cw_ref[...], buf_ref, carry_ref, j, first)
        out = _silu(y) * _dot(h, w_refs[1][...])
    o_ref[...] = out.astype(o_ref.dtype)
    tail_ref[0] = tail


def _gated_sample_body(*refs, n_w):
    h_ref = refs[0]
    w_refs = refs[1:1 + n_w]
    cw_ref, st_ref, o_ref, newst_ref = refs[1 + n_w:]
    h = h_ref[...]
    if n_w == 3:
        v = _dot(h, w_refs[1][...]) * _dot(h, w_refs[2][...])
        y, new_st = _conv_groups(v, cw_ref[...], st_ref[...])
        out = _dot(h, w_refs[0][...]) * y
    else:
        v = _dot(h, w_refs[0][...])
        y, new_st = _conv_groups(v, cw_ref[...], st_ref[...])
        out = _silu(y) * _dot(h, w_refs[1][...])
    o_ref[...] = out.astype(o_ref.dtype)
    newst_ref[...] = new_st


def _gated_proj(h, ws, col_offsets, n_cols, conv_w, state, seq_len, tm, tn, name):
    m, k = h.shape
    n_w = len(ws)
    nj = pl.cdiv(n_cols, tn)
    offs = [o // tn for o in col_offsets]
    assert all(o % tn == 0 for o in col_offsets)
    w_specs = [pl.BlockSpec((k, tn), functools.partial(lambda i, j, o: (0, o + j), o=o)) for o in offs]
    in_specs = [pl.BlockSpec((tm, k), lambda i, j: (i, 0))] + w_specs + [
        pl.BlockSpec((CONV_K, tn), lambda i, j: (0, j))]
    vmem = 2 * (tm * k * 2 + n_w * k * tn * 2 + tm * tn * 2) + (n_w + 4) * tm * tn * 4
    if state is None:
        assert seq_len % tm == 0
        body = functools.partial(_gated_prompt_body, n_w=n_w, tiles_per_seq=seq_len // tm)
        out_specs = [pl.BlockSpec((tm, tn), lambda i, j: (i, j)),
                     pl.BlockSpec((1, SUBLANES, tn), lambda i, j: (i, 0, j))]
        out_shape = [jax.ShapeDtypeStruct((m, n_cols), BF16),
                     jax.ShapeDtypeStruct((m // tm, SUBLANES, n_cols), F32)]
        scratch = [pltpu.VMEM((tm + SUBLANES, tn), F32), pltpu.VMEM((nj, SUBLANES, tn), F32)]
        args = (h,) + tuple(ws) + (conv_w,)
        semantics = ("arbitrary", "arbitrary")
    else:
        assert seq_len == SUBLANES and tm % SUBLANES == 0
        nb = tm // SUBLANES
        body = functools.partial(_gated_sample_body, n_w=n_w)
        in_specs.append(pl.BlockSpec((nb, CONV_K - 1, tn), lambda i, j: (i, 0, j)))
        out_specs = [pl.BlockSpec((tm, tn), lambda i, j: (i, j)),
                     pl.BlockSpec((nb, CONV_K - 1, tn), lambda i, j: (i, 0, j))]
        out_shape = [jax.ShapeDtypeStruct((m, n_cols), BF16),
                     jax.ShapeDtypeStruct(state.shape, F32)]
        scratch = []
        args = (h,) + tuple(ws) + (conv_w, state)
        semantics = ("parallel", "parallel")
        vmem += 4 * nb * SUBLANES * tn * 4
    return pl.pallas_call(
        body, grid=(m // tm, nj), in_specs=in_specs, out_specs=out_specs, out_shape=out_shape,
        scratch_shapes=scratch, compiler_params=_params(semantics, vmem + 4 * MIB), name=name,
    )(*args)


def _sink_softmax(s, mask, sink):
    s = jnp.where(mask, s, -jnp.inf)
    mx = jnp.maximum(jnp.max(s, axis=-1, keepdims=True), sink)
    p = jnp.exp(s - mx)
    denom = jnp.sum(p, axis=-1, keepdims=True) + jnp.exp(sink - mx)
    return p * (1.0 / denom)


def _prompt_attn_body(sink_ref, q_ref, kp_ref, kc_ref, vp_ref, vc_ref, o_ref, *, n_kv):
    b = pl.program_id(1)
    scale = HEAD_DIM ** -0.5
    q = q_ref[...].astype(BF16)
    kk = jnp.concatenate([kp_ref[...], kc_ref[...]], axis=0).astype(BF16)
    vv = jnp.concatenate([vp_ref[...], vc_ref[...]], axis=0).astype(BF16)
    r = lax.broadcasted_iota(jnp.int32, (BLOCK, 2 * BLOCK), 0)
    c = lax.broadcasted_iota(jnp.int32, (BLOCK, 2 * BLOCK), 1)
    mask = (c >= r) & (c <= r + WINDOW) & ((c >= BLOCK) | (b > 0))
    for kh in range(n_kv):
        k_h = kk[:, kh * HEAD_DIM:(kh + 1) * HEAD_DIM]
        v_h = vv[:, kh * HEAD_DIM:(kh + 1) * HEAD_DIM]
        for g in range(GROUP):
            hd = kh * GROUP + g
            s = _dot_nt(q[:, hd * HEAD_DIM:(hd + 1) * HEAD_DIM], k_h) * scale
            p = _sink_softmax(s, mask, sink_ref[hd])
            o_ref[:, hd * HEAD_DIM:(hd + 1) * HEAD_DIM] = _dot(p.astype(BF16), v_h).astype(o_ref.dtype)


def _prompt_attention(qkv, sinks, n, t, attn_w, kv_dim):
    nb = t // BLOCK
    kcol = attn_w // kv_dim
    cur = lambda col: (lambda i, b: (i * nb + b, col))
    prev = lambda col: (lambda i, b: (i * nb + jnp.maximum(b - 1, 0), col))
    return pl.pallas_call(
        functools.partial(_prompt_attn_body, n_kv=kv_dim // HEAD_DIM),
        grid=(n, nb),
        in_specs=[pl.BlockSpec(memory_space=pltpu.MemorySpace.SMEM),
                  pl.BlockSpec((BLOCK, attn_w), cur(0)),
                  pl.BlockSpec((BLOCK, kv_dim), prev(kcol)),
                  pl.BlockSpec((BLOCK, kv_dim), cur(kcol)),
                  pl.BlockSpec((BLOCK, kv_dim), prev(kcol + 1)),
                  pl.BlockSpec((BLOCK, kv_dim), cur(kcol + 1))],
        out_specs=pl.BlockSpec((BLOCK, attn_w), cur(0)),
        out_shape=jax.ShapeDtypeStruct((n * t, attn_w), BF16),
        compiler_params=_params(("parallel", "parallel"), 24 * MIB),
        name="prompt_attention",
    )(sinks, qkv, qkv, qkv, qkv, qkv)


def _sample_attn_body(sink_ref, q_ref, kn_ref, vn_ref, ck_ref, cv_ref, o_ref, *, bn, n_kv, t):
    scale = HEAD_DIM ** -0.5
    rows = t * GROUP
    cols = 2 * WINDOW
    r = lax.broadcasted_iota(jnp.int32, (rows, cols), 0) // GROUP
    c = lax.broadcasted_iota(jnp.int32, (rows, cols), 1)
    mask = ((c < WINDOW) & (c >= r)) | ((c >= WINDOW) & (c - WINDOW <= r))
    pad = jnp.zeros((cols - WINDOW - t, HEAD_DIM), F32)
    for n in range(bn):
        for kh in range(n_kv):
            sl = slice(kh * HEAD_DIM, (kh + 1) * HEAD_DIM)
            kk = jnp.concatenate([ck_ref[n, :, sl], kn_ref[n * t:(n + 1) * t, sl], pad], axis=0).astype(BF16)
            vv = jnp.concatenate([cv_ref[n, :, sl], vn_ref[n * t:(n + 1) * t, sl], pad], axis=0).astype(BF16)
            s = _dot_nt(q_ref[n, kh].astype(BF16), kk) * scale
            p = _sink_softmax(s, mask, sink_ref[kh])
            o_ref[n, kh] = _dot(p.astype(BF16), vv).astype(o_ref.dtype)


def _sample_attention(q4, qkv, cache_k, cache_v, sink_rows, n, t, attn_w, kv_dim, bn=8):
    n_kv = kv_dim // HEAD_DIM
    kcol = attn_w // kv_dim
    rows = t * GROUP
    return pl.pallas_call(
        functools.partial(_sample_attn_body, bn=bn, n_kv=n_kv, t=t),
        grid=(n // bn,),
        in_specs=[pl.BlockSpec((n_kv, rows, 1), lambda i: (0, 0, 0)),
                  pl.BlockSpec((bn, n_kv, rows, HEAD_DIM), lambda i: (i, 0, 0, 0)),
                  pl.BlockSpec((bn * t, kv_dim), lambda i: (i, kcol)),
                  pl.BlockSpec((bn * t, kv_dim), lambda i: (i, kcol + 1)),
                  pl.BlockSpec((bn, WINDOW, kv_dim), lambda i: (i, 0, 0)),
                  pl.BlockSpec((bn, WINDOW, kv_dim), lambda i: (i, 0, 0))],
        out_specs=pl.BlockSpec((bn, n_kv, rows, HEAD_DIM), lambda i: (i, 0, 0, 0)),
        out_shape=jax.ShapeDtypeStruct((n, n_kv, rows, HEAD_DIM), BF16),
        compiler_params=_params(("parallel",), 24 * MIB),
        name="sample_attention",
    )(sink_rows, q4, qkv, qkv, cache_k, cache_v)


def _gated_residual(x, acc, mod_ref, gate_idx):
    tm, tn = acc.shape
    bn = mod_ref.shape[0]
    gate = mod_ref[:, gate_idx:gate_idx + 1, :]
    y = x.reshape(bn, tm // bn, tn) + gate * acc.reshape(bn, tm // bn, tn)
    return y.reshape(tm, tn)


def _out_proj_body(a_ref, c_ref, wa_ref, wc_ref, x_ref, mod_ref, o_ref, *, gate_idx):
    acc = _dot(a_ref[...], wa_ref[...]) + _dot(c_ref[...], wc_ref[...])
    o_ref[...] = _gated_residual(x_ref[...], acc, mod_ref, gate_idx)


def _out_proj(attn, conv, w_out, x, mod3, gate_idx, seq_len, tm, tn):
    m, ka = attn.shape
    kc = conv.shape[1]
    assert ka == kc
    d = w_out.shape[1]
    bn, seq_of = _seq_blocks(seq_len, tm)
    vmem = 2 * (tm * (ka + kc) * 2 + (ka + kc) * tn * 2 + 2 * tm * tn * 4) + 3 * tm * tn * 4
    return pl.pallas_call(
        functools.partial(_out_proj_body, gate_idx=gate_idx),
        grid=(m // tm, d // tn),
        in_specs=[pl.BlockSpec((tm, ka), lambda i, j: (i, 0)),
                  pl.BlockSpec((tm, kc), lambda i, j: (i, 0)),
                  pl.BlockSpec((ka, tn), lambda i, j: (0, j)),
                  pl.BlockSpec((kc, tn), lambda i, j: (1, j)),
                  pl.BlockSpec((tm, tn), lambda i, j: (i, j)),
                  pl.BlockSpec((bn, N_MOD, tn), lambda i, j: (seq_of(i), 0, j))],
        out_specs=pl.BlockSpec((tm, tn), lambda i, j: (i, j)),
        out_shape=jax.ShapeDtypeStruct((m, d), F32),
        compiler_params=_params(("parallel", "parallel"), vmem + 4 * MIB),
        name="out_proj",
    )(attn, conv, w_out, w_out, x, mod3)


def _seq_blocks(seq_len, tm):
    if seq_len >= tm:
        assert seq_len % tm == 0
        tiles = seq_len // tm
        return 1, (lambda i: i // tiles)
    assert tm % seq_len == 0
    return tm // seq_len, (lambda i: i)


def _ffn_down_body(a_ref, w_ref, x_ref, mod_ref, o_ref, *, gate_idx):
    o_ref[...] = _gated_residual(x_ref[...], _dot(a_ref[...], w_ref[...]), mod_ref, gate_idx)


def _ffn_down(act, w_down, x, mod3, gate_idx, seq_len, tm, tn):
    m, k = act.shape
    d = w_down.shape[1]
    bn, seq_of = _seq_blocks(seq_len, tm)
    vmem = 2 * (tm * k * 2 + k * tn * 2 + 2 * tm * tn * 4) + 3 * tm * tn * 4
    return pl.pallas_call(
        functools.partial(_ffn_down_body, gate_idx=gate_idx),
        grid=(m // tm, d // tn),
        in_specs=[pl.BlockSpec((tm, k), lambda i, j: (i, 0)),
                  pl.BlockSpec((k, tn), lambda i, j: (0, j)),
                  pl.BlockSpec((tm, tn), lambda i, j: (i, j)),
                  pl.BlockSpec((bn, N_MOD, tn), lambda i, j: (seq_of(i), 0, j))],
        out_specs=pl.BlockSpec((tm, tn), lambda i, j: (i, j)),
        out_shape=jax.ShapeDtypeStruct((m, d), F32),
        compiler_params=_params(("parallel", "parallel"), vmem + 4 * MIB),
        name="ffn_down",
    )(act, w_down, x, mod3)


def _layer(x3, mod3, conv_state, ffn_state, cache_k, cache_v, w, is_prompt):
    n, t, d = x3.shape
    m = n * t
    attn_w = d // 2
    d_conv = d - attn_w
    n_q = attn_w // HEAD_DIM
    n_kv = max(1, n_q // GROUP)
    kv_dim = n_kv * HEAD_DIM
    d_ff = w["w_down"].shape[0]
    off_b = attn_w + 2 * kv_dim
    x2 = x3.reshape(m, d)

    if is_prompt:
        tm = min(1024, t)
        bn, tt = 1, min(256, t)
    else:
        assert t == SUBLANES
        tm = min(1024, m)
        bn, tt = min(32, n), t
    tm_down = min(512, tm)

    h = _norm_mod(x3, mod3, w["g_mix"], 0, 1, bn, tt)
    qkv = _matmul(h, w["w_in"], off_b, tm, 512)
    conv, conv_aux = _gated_proj(h, (w["w_in"],) * 3, (off_b, off_b + d_conv, off_b + 2 * d_conv), d_conv,
                                 w["conv_w"], conv_state, t, tm, 256, "conv_gate")
    if is_prompt:
        attn = _prompt_attention(qkv, w["sinks"], n, t, attn_w, kv_dim)
    else:
        q4 = qkv[:, :attn_w].reshape(n, t, n_kv, GROUP, HEAD_DIM).transpose(0, 2, 1, 3, 4)
        q4 = q4.reshape(n, n_kv, t * GROUP, HEAD_DIM)
        sink_rows = jnp.tile(w["sinks"].reshape(n_kv, 1, GROUP), (1, t, 1)).reshape(n_kv, t * GROUP, 1)
        o4 = _sample_attention(q4, qkv, cache_k.reshape(n, WINDOW, kv_dim), cache_v.reshape(n, WINDOW, kv_dim),
                               sink_rows, n, t, attn_w, kv_dim, bn=min(8, n))
        attn = o4.reshape(n, n_kv, t, GROUP, HEAD_DIM).transpose(0, 2, 1, 3, 4).reshape(m, attn_w)
    x1 = _out_proj(attn, conv, w["w_out"], x2, mod3, 2, t, tm, 512)

    h2 = _norm_mod(x1.reshape(n, t, d), mod3, w["g_ffn"], 3, 4, bn, tt)
    act, ffn_aux = _gated_proj(h2, (w["w_gate"], w["w_up"]), (0, 0), d_ff, w["ffn_conv_w"], ffn_state, t, tm,
                               512, "ffn_act")
    xo = _ffn_down(act, w["w_down"], x1, mod3, 5, t, tm_down, 512)

    k_new = qkv[:, attn_w:attn_w + kv_dim].reshape(n, t, n_kv, HEAD_DIM)
    v_new = qkv[:, attn_w + kv_dim:off_b].reshape(n, t, n_kv, HEAD_DIM)
    if is_prompt:
        keep = min(WINDOW, t)
        tiles = t // tm
        k_state, v_state = k_new[:, t - keep:], v_new[:, t - keep:]
        conv_state_out = conv_aux[tiles - 1::tiles, SUBLANES - (CONV_K - 1):, :]
        ffn_state_out = ffn_aux[tiles - 1::tiles, SUBLANES - (CONV_K - 1):, :]
    else:
        k_state = jnp.concatenate([cache_k, k_new], axis=1)[:, t:]
        v_state = jnp.concatenate([cache_v, v_new], axis=1)[:, t:]
        conv_state_out, ffn_state_out = conv_aux, ffn_aux
    return xo, k_state, v_state, conv_state_out, ffn_state_out


def kernel(x_prompt, x_sample, cache_k_win, cache_v_win, state_conv, state_ffn_conv, c_prompt, c_sample, w_mod, b_mod, g_mix, g_ffn, w_in, conv_w, sinks, w_out, w_gate_up, ffn_conv_w, w_down, g_final):
    depth = w_mod.shape[0]
    np_, tp, d = x_prompt.shape
    ns, ts, _ = x_sample.shape
    assert tp % BLOCK == 0 and cache_k_win.shape[2] == WINDOW

    d_ff = w_down.shape[1]
    n_c = np_ + ns
    n_c_pad = -(-n_c // SUBLANES) * SUBLANES
    c_all = jnp.concatenate([c_prompt, c_sample, jnp.zeros((n_c_pad - n_c, d), F32)], axis=0)

    xp, xs = x_prompt, x_sample
    outs_p, outs_s = [], []
    for l in range(depth):
        w = dict(g_mix=g_mix[l], g_ffn=g_ffn[l], conv_w=conv_w[l], sinks=sinks[l], ffn_conv_w=ffn_conv_w[l],
                 w_in=w_in[l].astype(BF16), w_out=w_out[l].astype(BF16),
                 w_gate=w_gate_up[l, :, :d_ff].astype(BF16), w_up=w_gate_up[l, :, d_ff:].astype(BF16),
                 w_down=w_down[l].astype(BF16))
        mod = _mod(c_all, w_mod[l], b_mod[l]).reshape(n_c_pad, N_MOD, d)
        xp2, *st_p = _layer(xp, mod[:np_], None, None, None, None, w, True)
        xs2, *st_s = _layer(xs, mod[np_:n_c], state_conv[l], state_ffn_conv[l], cache_k_win[l], cache_v_win[l],
                            w, False)
        xp, xs = xp2.reshape(xp.shape), xs2.reshape(xs.shape)
        outs_p.append(st_p)
        outs_s.append(st_s)

    y_prompt = _final_norm(xp.reshape(np_ * tp, d), g_final).reshape(np_, tp, d)
    y_sample = _final_norm(xs.reshape(ns * ts, d), g_final).reshape(ns, ts, d)
    stack = lambda outs, k: jnp.stack([o[k] for o in outs])
    return (y_prompt, y_sample,
            stack(outs_p, 0), stack(outs_p, 1), stack(outs_p, 2), stack(outs_p, 3),
            stack(outs_s, 0), stack(outs_s, 1), stack(outs_s, 2), stack(outs_s, 3))
```

```python
import functools

import jax
import jax.numpy as jnp
from jax import lax
from jax.experimental import pallas as pl
from jax.experimental.pallas import tpu as pltpu

F32 = jnp.float32
BF16 = jnp.bfloat16

HEAD_DIM = 64
GROUP = 8
WINDOW = 128
BLOCK = 128
CONV_K = 3
N_MOD = 6
EPS = 1e-5
SUBLANES = 8
MIB = 1024 * 1024
VMEM_CAP = 56 * MIB


def _params(semantics, vmem_bytes):
    return pltpu.CompilerParams(dimension_semantics=semantics,
                                vmem_limit_bytes=int(min(max(vmem_bytes, 16 * MIB), VMEM_CAP)))


def _dot(a, b):
    return jnp.dot(a, b, preferred_element_type=F32)


def _dot_nt(a, b):
    return lax.dot_general(a, b, (((1,), (1,)), ((), ())), preferred_element_type=F32)


def _weight_tile(w_ref, wout_ref):
    if wout_ref is None:
        return w_ref[...]
    wout_ref[...] = w_ref[...].astype(BF16)
    return wout_ref[...]


def _row_spec(shape, index_map, n_row_tiles):
    if n_row_tiles == 1:
        return pl.BlockSpec(shape, index_map, pipeline_mode=pl.Buffered(1))
    return pl.BlockSpec(shape, index_map)


def _nbytes(shape, dtype):
    n = jnp.dtype(dtype).itemsize
    for s in shape:
        n *= s
    return n


def _mod_body(c_ref, w_ref, b_ref, o_ref):
    c = c_ref[...]
    a = (c * jax.nn.sigmoid(c)).astype(BF16)
    o_ref[...] = _dot(a, w_ref[...].astype(BF16)) + b_ref[...]


def _mod(c_all, w_mod, b_mod, tn=512):
    nc, d = c_all.shape
    n = w_mod.shape[1]
    vmem = 2 * (nc * d * 4 + d * tn * 4 + nc * tn * 4) + d * tn * 2 + nc * d * 4
    return pl.pallas_call(
        _mod_body,
        grid=(n // tn,),
        in_specs=[pl.BlockSpec((nc, d), lambda j: (0, 0)),
                  pl.BlockSpec((d, tn), lambda j: (0, j)),
                  pl.BlockSpec((1, tn), lambda j: (0, j))],
        out_specs=pl.BlockSpec((nc, tn), lambda j: (0, j)),
        out_shape=jax.ShapeDtypeStruct((nc, n), F32),
        compiler_params=_params(("parallel",), vmem + 4 * MIB),
        name="mod",
    )(c_all, w_mod, b_mod.reshape(1, n))


def _rms(x):
    return x * lax.rsqrt(jnp.mean(x * x, axis=-1, keepdims=True) + EPS)


def _norm_mod_body(x_ref, mod_ref, g_ref, o_ref, *, shift_idx, scale_idx):
    x = x_ref[...]
    y = _rms(x) * g_ref[...]
    h = y * (1.0 + mod_ref[:, scale_idx:scale_idx + 1, :]) + mod_ref[:, shift_idx:shift_idx + 1, :]
    o_ref[...] = h.reshape(o_ref.shape).astype(o_ref.dtype)


def _norm_mod(x3, mod3, g, shift_idx, scale_idx, bn, tt):
    n, t, d = x3.shape
    nt = t // tt
    vmem = 2 * (bn * tt * d * 4 + bn * tt * d * 2 + bn * SUBLANES * d * 4) + 3 * bn * tt * d * 4
    return pl.pallas_call(
        functools.partial(_norm_mod_body, shift_idx=shift_idx, scale_idx=scale_idx),
        grid=(n // bn, nt),
        in_specs=[pl.BlockSpec((bn, tt, d), lambda i, j: (i, j, 0)),
                  pl.BlockSpec((bn, N_MOD, d), lambda i, j: (i, 0, 0)),
                  pl.BlockSpec((1, d), lambda i, j: (0, 0))],
        out_specs=pl.BlockSpec((bn * tt, d), lambda i, j: (i * nt + j, 0)),
        out_shape=jax.ShapeDtypeStruct((n * t, d), BF16),
        compiler_params=_params(("parallel", "parallel"), vmem),
        name="norm_mod",
    )(x3, mod3, g.reshape(1, d))


def _final_norm_body(x_ref, g_ref, o_ref):
    o_ref[...] = _rms(x_ref[...]) * g_ref[...]


def _final_norm(x2, g, tm=256):
    m, d = x2.shape
    tm = min(tm, m)
    return pl.pallas_call(
        _final_norm_body,
        grid=(m // tm,),
        in_specs=[pl.BlockSpec((tm, d), lambda i: (i, 0)),
                  pl.BlockSpec((1, d), lambda i: (0, 0))],
        out_specs=pl.BlockSpec((tm, d), lambda i: (i, 0)),
        out_shape=jax.ShapeDtypeStruct((m, d), F32),
        compiler_params=_params(("parallel",), 6 * tm * d * 4),
        name="final_norm",
    )(x2, g.reshape(1, d))


def _matmul_body(a_ref, w_ref, o_ref, wout_ref=None):
    o_ref[...] = _dot(a_ref[...], _weight_tile(w_ref, wout_ref)).astype(o_ref.dtype)


def _matmul(a, w, n_cols, tm, tn, emit_w):
    m, k = a.shape
    n_i = m // tm
    assert n_i == 1 or not emit_w
    out_specs = [pl.BlockSpec((tm, tn), lambda i, j: (i, j))]
    out_shape = [jax.ShapeDtypeStruct((m, n_cols), F32)]
    if emit_w:
        out_specs.append(pl.BlockSpec((k, tn), lambda i, j: (0, j)))
        out_shape.append(jax.ShapeDtypeStruct((k, n_cols), BF16))
    vmem = (min(n_i, 2) * tm * k * 2 + 2 * _nbytes((k, tn), w.dtype) + (2 * k * tn * 2 if emit_w else 0)
            + 3 * tm * tn * 4)
    return pl.pallas_call(
        _matmul_body,
        grid=(n_i, n_cols // tn),
        in_specs=[_row_spec((tm, k), lambda i, j: (i, 0), n_i),
                  pl.BlockSpec((k, tn), lambda i, j: (0, j))],
        out_specs=out_specs, out_shape=out_shape,
        compiler_params=_params(("arbitrary", "arbitrary"), vmem + 4 * MIB),
        name="qkv",
    )(a, w)


def _silu(x):
    return x * jax.nn.sigmoid(x)


def _gated_body(*refs, n_w, emit_w, is_prompt, tiles_per_seq, n_chunks):
    h_ref = refs[0]
    w_refs = refs[1:1 + n_w]
    cw_ref = refs[1 + n_w]
    pos = 2 + n_w
    st_ref = None
    if not is_prompt:
        st_ref = refs[pos]
        pos += 1
    o_ref, aux_ref = refs[pos], refs[pos + 1]
    pos += 2
    wout_refs = refs[pos:pos + n_w] if emit_w else (None,) * n_w
    carry_ref = refs[-1] if is_prompt else None

    ws = [_weight_tile(r, o) for r, o in zip(w_refs, wout_refs)]
    cw = cw_ref[...]
    tm, tn = o_ref.shape
    rm = tm // n_chunks
    nb = rm // SUBLANES
    t = lax.broadcasted_iota(jnp.int32, (nb, SUBLANES, tn), 1)
    if is_prompt:
        i, j = pl.program_id(0), pl.program_id(1)

        @pl.when((i % tiles_per_seq) == 0)
        def _():
            carry_ref[j] = jnp.zeros((SUBLANES, tn), F32)

        tail = carry_ref[j]
    for c in range(n_chunks):
        h = h_ref[c * rm:(c + 1) * rm, :]
        if n_w == 3:
            v = _dot(h, ws[1]) * _dot(h, ws[2])
        else:
            v = _dot(h, ws[0])
        v3 = v.reshape(nb, SUBLANES, tn)
        if is_prompt:
            ext = jnp.concatenate([tail[None], v3], axis=0)
            e1 = pltpu.roll(ext, 1, axis=1)
            e2 = pltpu.roll(ext, 2, axis=1)
            v1 = jnp.where(t == 0, e1[:nb], e1[1:])
            v2 = jnp.where(t < 2, e2[:nb], e2[1:])
            tail = v3[nb - 1]
        else:
            p0 = st_ref[c * nb:(c + 1) * nb, 0:1, :]
            p1 = st_ref[c * nb:(c + 1) * nb, 1:2, :]
            v1 = jnp.where(t == 0, p1, pltpu.roll(v3, 1, axis=1))
            v2 = jnp.where(t == 0, p0, jnp.where(t == 1, p1, pltpu.roll(v3, 2, axis=1)))
            aux_ref[c * nb:(c + 1) * nb] = v3[:, SUBLANES - (CONV_K - 1):, :]
        y = (cw[0:1, :] * v2 + cw[1:2, :] * v1 + cw[2:3, :] * v3).reshape(rm, tn)
        if n_w == 3:
            out = _dot(h, ws[0]) * y
        else:
            out = _silu(y) * _dot(h, ws[1])
        o_ref[c * rm:(c + 1) * rm, :] = out.astype(o_ref.dtype)
    if is_prompt:
        carry_ref[j] = tail
        aux_ref[0] = tail


def _gated_proj(h, ws, col_offsets, n_cols, conv_w, state, seq_len, tm, tn, emit_w, name, n_chunks=1):
    m, k = h.shape
    n_w = len(ws)
    n_i, nj = m // tm, pl.cdiv(n_cols, tn)
    assert all(o % tn == 0 for o in col_offsets)
    offs = [o // tn for o in col_offsets]
    is_prompt = state is None
    spec = ((lambda shape, f: pl.BlockSpec(shape, f)) if is_prompt else
            (lambda shape, f: pl.BlockSpec(shape, lambda j, i: f(i, j))))
    w_specs = [spec((k, tn), functools.partial(lambda i, j, o: (0, o + j), o=o)) for o in offs]
    h_spec = _row_spec((tm, k), (lambda i, j: (i, 0)) if is_prompt else (lambda j, i: (i, 0)), n_i)
    in_specs = [h_spec] + w_specs + [spec((CONV_K, tn), lambda i, j: (0, j))]
    out_specs = [spec((tm, tn), lambda i, j: (i, j))]
    out_shape = [jax.ShapeDtypeStruct((m, n_cols), BF16)]
    args = (h,) + tuple(ws) + (conv_w,)
    if is_prompt:
        assert seq_len % tm == 0
        tiles_per_seq = seq_len // tm
        grid = (n_i, nj)
        out_specs.append(spec((1, SUBLANES, tn), lambda i, j: (i, 0, j)))
        out_shape.append(jax.ShapeDtypeStruct((n_i, SUBLANES, n_cols), F32))
        scratch = [pltpu.VMEM((nj, SUBLANES, tn), F32)]
    else:
        assert seq_len == SUBLANES and tm % SUBLANES == 0
        tiles_per_seq = None
        grid = (nj, n_i)
        nb = tm // SUBLANES
        in_specs.append(spec((nb, CONV_K - 1, tn), lambda i, j: (i, 0, j)))
        out_specs.append(spec((nb, CONV_K - 1, tn), lambda i, j: (i, 0, j)))
        out_shape.append(jax.ShapeDtypeStruct(state.shape, F32))
        scratch = []
        args += (state,)
    if emit_w:
        out_specs += [spec((k, tn), lambda i, j: (0, j))] * n_w
        out_shape += [jax.ShapeDtypeStruct((k, n_cols), BF16)] * n_w
    vmem = (min(n_i, 2) * tm * k * 2 + 2 * sum(_nbytes((k, tn), w.dtype) for w in ws)
            + (2 * n_w * k * tn * 2 if emit_w else 0) + 2 * tm * tn * 2 + 14 * tm * tn * 4)
    return pl.pallas_call(
        functools.partial(_gated_body, n_w=n_w, emit_w=emit_w, is_prompt=is_prompt, tiles_per_seq=tiles_per_seq,
                          n_chunks=n_chunks),
        grid=grid, in_specs=in_specs, out_specs=out_specs, out_shape=out_shape,
        scratch_shapes=scratch, compiler_params=_params(("arbitrary", "arbitrary"), vmem + 4 * MIB), name=name,
    )(*args)


def _sink_softmax(s, mask, sink):
    s = jnp.where(mask, s, -jnp.inf)
    mx = jnp.maximum(jnp.max(s, axis=-1, keepdims=True), sink)
    p = jnp.exp(s - mx)
    denom = jnp.sum(p, axis=-1, keepdims=True) + jnp.exp(sink - mx)
    return p * (1.0 / denom)


def _prompt_attn_body(sink_ref, q_ref, kp_ref, kc_ref, vp_ref, vc_ref, o_ref, *, n_kv):
    b = pl.program_id(1)
    scale = HEAD_DIM ** -0.5
    q = q_ref[...].astype(BF16)
    kk = jnp.concatenate([kp_ref[...], kc_ref[...]], axis=0).astype(BF16)
    vv = jnp.concatenate([vp_ref[...], vc_ref[...]], axis=0).astype(BF16)
    r = lax.broadcasted_iota(jnp.int32, (BLOCK, 2 * BLOCK), 0)
    c = lax.broadcasted_iota(jnp.int32, (BLOCK, 2 * BLOCK), 1)
    mask = (c >= r) & (c <= r + WINDOW) & ((c >= BLOCK) | (b > 0))
    for kh in range(n_kv):
        k_h = kk[:, kh * HEAD_DIM:(kh + 1) * HEAD_DIM]
        v_h = vv[:, kh * HEAD_DIM:(kh + 1) * HEAD_DIM]
        for g in range(GROUP):
            hd = kh * GROUP + g
            s = _dot_nt(q[:, hd * HEAD_DIM:(hd + 1) * HEAD_DIM], k_h) * scale
            p = _sink_softmax(s, mask, sink_ref[hd])
            o_ref[:, hd * HEAD_DIM:(hd + 1) * HEAD_DIM] = _dot(p.astype(BF16), v_h).astype(o_ref.dtype)


def _prompt_attention(qkv, sinks, n, t, attn_w, kv_dim):
    nb = t // BLOCK
    kcol = attn_w // kv_dim
    cur = lambda col: (lambda i, b: (i * nb + b, col))
    prev = lambda col: (lambda i, b: (i * nb + jnp.maximum(b - 1, 0), col))
    return pl.pallas_call(
        functools.partial(_prompt_attn_body, n_kv=kv_dim // HEAD_DIM),
        grid=(n, nb),
        in_specs=[pl.BlockSpec(memory_space=pltpu.MemorySpace.SMEM),
                  pl.BlockSpec((BLOCK, attn_w), cur(0)),
                  pl.BlockSpec((BLOCK, kv_dim), prev(kcol)),
                  pl.BlockSpec((BLOCK, kv_dim), cur(kcol)),
                  pl.BlockSpec((BLOCK, kv_dim), prev(kcol + 1)),
                  pl.BlockSpec((BLOCK, kv_dim), cur(kcol + 1))],
        out_specs=pl.BlockSpec((BLOCK, attn_w), cur(0)),
        out_shape=jax.ShapeDtypeStruct((n * t, attn_w), BF16),
        compiler_params=_params(("parallel", "parallel"), 24 * MIB),
        name="prompt_attention",
    )(sinks, qkv, qkv, qkv, qkv, qkv)


def _sample_attn_body(sink_ref, q_ref, kn_ref, vn_ref, ck_ref, cv_ref, o_ref, *, bn, n_kv, t):
    scale = HEAD_DIM ** -0.5
    rows = t * GROUP
    cols = 2 * WINDOW
    r = lax.broadcasted_iota(jnp.int32, (rows, cols), 0) // GROUP
    c = lax.broadcasted_iota(jnp.int32, (rows, cols), 1)
    mask = ((c < WINDOW) & (c >= r)) | ((c >= WINDOW) & (c - WINDOW <= r))
    pad = jnp.zeros((cols - WINDOW - t, HEAD_DIM), F32)
    for n in range(bn):
        for kh in range(n_kv):
            sl = slice(kh * HEAD_DIM, (kh + 1) * HEAD_DIM)
            kk = jnp.concatenate([ck_ref[n, :, sl], kn_ref[n * t:(n + 1) * t, sl], pad], axis=0).astype(BF16)
            vv = jnp.concatenate([cv_ref[n, :, sl], vn_ref[n * t:(n + 1) * t, sl], pad], axis=0).astype(BF16)
            s = _dot_nt(q_ref[n, kh].astype(BF16), kk) * scale
            p = _sink_softmax(s, mask, sink_ref[kh])
            o_ref[n, kh] = _dot(p.astype(BF16), vv).astype(o_ref.dtype)


def _sample_attention(q4, qkv, cache_k, cache_v, sink_rows, n, t, attn_w, kv_dim, bn=8):
    n_kv = kv_dim // HEAD_DIM
    kcol = attn_w // kv_dim
    rows = t * GROUP
    return pl.pallas_call(
        functools.partial(_sample_attn_body, bn=bn, n_kv=n_kv, t=t),
        grid=(n // bn,),
        in_specs=[pl.BlockSpec((n_kv, rows, 1), lambda i: (0, 0, 0)),
                  pl.BlockSpec((bn, n_kv, rows, HEAD_DIM), lambda i: (i, 0, 0, 0)),
                  pl.BlockSpec((bn * t, kv_dim), lambda i: (i, kcol)),
                  pl.BlockSpec((bn * t, kv_dim), lambda i: (i, kcol + 1)),
                  pl.BlockSpec((bn, WINDOW, kv_dim), lambda i: (i, 0, 0)),
                  pl.BlockSpec((bn, WINDOW, kv_dim), lambda i: (i, 0, 0))],
        out_specs=pl.BlockSpec((bn, n_kv, rows, HEAD_DIM), lambda i: (i, 0, 0, 0)),
        out_shape=jax.ShapeDtypeStruct((n, n_kv, rows, HEAD_DIM), BF16),
        compiler_params=_params(("parallel",), 24 * MIB),
        name="sample_attention",
    )(sink_rows, q4, qkv, qkv, cache_k, cache_v)


def _seq_blocks(seq_len, tm):
    if seq_len >= tm:
        assert seq_len % tm == 0
        tiles = seq_len // tm
        return 1, (lambda i: i // tiles)
    assert tm % seq_len == 0
    return tm // seq_len, (lambda i: i)


def _gated_residual(x, acc, mod_ref, gate_idx):
    tm, tn = acc.shape
    bn = mod_ref.shape[0]
    gate = mod_ref[:, gate_idx:gate_idx + 1, :]
    y = x.reshape(bn, tm // bn, tn) + gate * acc.reshape(bn, tm // bn, tn)
    return y.reshape(tm, tn)


def _out_proj_body(a_ref, c_ref, wa_ref, wc_ref, x_ref, mod_ref, o_ref, woa_ref=None, woc_ref=None, *, gate_idx):
    acc = _dot(a_ref[...], _weight_tile(wa_ref, woa_ref)) + _dot(c_ref[...], _weight_tile(wc_ref, woc_ref))
    o_ref[...] = _gated_residual(x_ref[...], acc, mod_ref, gate_idx)


def _out_proj(attn, conv, wa, wc, row_blocks, x, mod3, gate_idx, seq_len, tm, tn, emit_w):
    m, ka = attn.shape
    assert conv.shape[1] == ka
    d = x.shape[1]
    n_i = m // tm
    assert n_i == 1 or not emit_w
    bn, seq_of = _seq_blocks(seq_len, tm)
    ra, rc = row_blocks
    out_specs = [pl.BlockSpec((tm, tn), lambda i, j: (i, j))]
    out_shape = [jax.ShapeDtypeStruct((m, d), F32)]
    if emit_w:
        out_specs += [pl.BlockSpec((ka, tn), lambda i, j: (0, j))] * 2
        out_shape += [jax.ShapeDtypeStruct((ka, d), BF16)] * 2
    vmem = (min(n_i, 2) * 2 * tm * ka * 2 + 4 * _nbytes((ka, tn), wa.dtype) + (4 * ka * tn * 2 if emit_w else 0)
            + 7 * tm * tn * 4)
    return pl.pallas_call(
        functools.partial(_out_proj_body, gate_idx=gate_idx),
        grid=(n_i, d // tn),
        in_specs=[_row_spec((tm, ka), lambda i, j: (i, 0), n_i),
                  _row_spec((tm, ka), lambda i, j: (i, 0), n_i),
                  pl.BlockSpec((ka, tn), lambda i, j: (ra, j)),
                  pl.BlockSpec((ka, tn), lambda i, j: (rc, j)),
                  pl.BlockSpec((tm, tn), lambda i, j: (i, j)),
                  pl.BlockSpec((bn, N_MOD, tn), lambda i, j: (seq_of(i), 0, j))],
        out_specs=out_specs, out_shape=out_shape,
        compiler_params=_params(("arbitrary", "arbitrary"), vmem + 4 * MIB),
        name="out_proj",
    )(attn, conv, wa, wc, x, mod3)


def _ffn_down_body(a_ref, w_ref, x_ref, mod_ref, o_ref, wout_ref=None, *, gate_idx, nk):
    acc = _dot(a_ref[...], _weight_tile(w_ref, wout_ref))
    if nk == 1:
        o_ref[...] = _gated_residual(x_ref[...], acc, mod_ref, gate_idx)
    else:
        kk = pl.program_id(2)

        @pl.when(kk == 0)
        def _():
            o_ref[...] = _gated_residual(x_ref[...], acc, mod_ref, gate_idx)

        @pl.when(kk > 0)
        def _():
            o_ref[...] = _gated_residual(o_ref[...], acc, mod_ref, gate_idx)


def _ffn_down(act, w_down, x, mod3, gate_idx, seq_len, tm, tn, nk, emit_w):
    m, k = act.shape
    d = x.shape[1]
    tk = k // nk
    assert tk * nk == k and (m == tm or not emit_w)
    bn, seq_of = _seq_blocks(seq_len, tm)
    out_specs = [pl.BlockSpec((tm, tn), lambda i, j, kk: (i, j))]
    out_shape = [jax.ShapeDtypeStruct((m, d), F32)]
    if emit_w:
        out_specs.append(pl.BlockSpec((tk, tn), lambda i, j, kk: (kk, j)))
        out_shape.append(jax.ShapeDtypeStruct((k, d), BF16))
    vmem = (2 * tm * tk * 2 + 2 * _nbytes((tk, tn), w_down.dtype) + (2 * tk * tn * 2 if emit_w else 0)
            + 7 * tm * tn * 4)
    return pl.pallas_call(
        functools.partial(_ffn_down_body, gate_idx=gate_idx, nk=nk),
        grid=(m // tm, d // tn, nk),
        in_specs=[pl.BlockSpec((tm, tk), lambda i, j, kk: (i, kk)),
                  pl.BlockSpec((tk, tn), lambda i, j, kk: (kk, j)),
                  pl.BlockSpec((tm, tn), lambda i, j, kk: (i, j)),
                  pl.BlockSpec((bn, N_MOD, tn), lambda i, j, kk: (seq_of(i), 0, j))],
        out_specs=out_specs, out_shape=out_shape,
        compiler_params=_params(("arbitrary", "arbitrary", "arbitrary"), vmem + 4 * MIB),
        name="ffn_down",
    )(act, w_down, x, mod3)


def _layer(x3, mod3, conv_state, ffn_state, cache_k, cache_v, w, is_prompt):
    n, t, d = x3.shape
    m = n * t
    attn_w = d // 2
    d_conv = d - attn_w
    n_q = attn_w // HEAD_DIM
    n_kv = max(1, n_q // GROUP)
    kv_dim = n_kv * HEAD_DIM
    off_b = attn_w + 2 * kv_dim
    x2 = x3.reshape(m, d)
    emit = not is_prompt

    if is_prompt:
        tm = min(1024, t)
        bn, tt = 1, min(256, t)
        tn_qkv, tn_conv, tn_out, tn_act, tn_down, nk_down = 512, 512, 512, 512, 512, 1
        w_conv, off_conv = (w["w_b"], w["w_c"], w["w_h"]), (0, 0, 0)
        w_act, off_act = (w["w_g"], w["w_u"]), (0, 0)
        w_oa, w_oc, out_rows = w["w_oa"], w["w_oc"], (0, 0)
        w_qkv, w_down = w["w_qkv"], w["w_d"]
        d_ff = w_down.shape[0]
    else:
        assert t == SUBLANES
        tm = min(1024, m)
        bn, tt = min(32, n), t
        tn_qkv, tn_conv, tn_out, tn_act, tn_down, nk_down = 256, 256, 256, 256, 256, 2
        d_ff = w["w_down"].shape[0]
        w_conv, off_conv = (w["w_in"],) * 3, (off_b, off_b + d_conv, off_b + 2 * d_conv)
        w_act, off_act = (w["w_gate_up"],) * 2, (0, d_ff)
        w_oa = w_oc = w["w_out"]
        out_rows = (0, 1)
        w_qkv, w_down = w["w_in"], w["w_down"]
    tm_conv = min(512, tm)
    tm_down = min(512, tm) if is_prompt else tm

    h = _norm_mod(x3, mod3, w["g_mix"], 0, 1, bn, tt)
    qkv, *wb_qkv = _matmul(h, w_qkv, off_b, tm, tn_qkv, emit)
    conv, conv_aux, *wb_conv = _gated_proj(h, w_conv, off_conv, d_conv, w["conv_w"], conv_state, t, tm_conv,
                                           tn_conv, emit, "conv_gate", n_chunks=2 if is_prompt else 1)
    if is_prompt:
        attn = _prompt_attention(qkv, w["sinks"], n, t, attn_w, kv_dim)
    else:
        q4 = qkv[:, :attn_w].reshape(n, t, n_kv, GROUP, HEAD_DIM).transpose(0, 2, 1, 3, 4)
        q4 = q4.reshape(n, n_kv, t * GROUP, HEAD_DIM)
        sink_rows = jnp.tile(w["sinks"].reshape(n_kv, 1, GROUP), (1, t, 1)).reshape(n_kv, t * GROUP, 1)
        o4 = _sample_attention(q4, qkv, cache_k.reshape(n, WINDOW, kv_dim), cache_v.reshape(n, WINDOW, kv_dim),
                               sink_rows, n, t, attn_w, kv_dim, bn=min(8, n))
        attn = o4.reshape(n, n_kv, t, GROUP, HEAD_DIM).transpose(0, 2, 1, 3, 4).reshape(m, attn_w)
    x1, *wb_out = _out_proj(attn, conv, w_oa, w_oc, out_rows, x2, mod3, 2, t, tm, tn_out, emit)

    h2 = _norm_mod(x1.reshape(n, t, d), mod3, w["g_ffn"], 3, 4, bn, tt)
    act, ffn_aux, *wb_act = _gated_proj(h2, w_act, off_act, d_ff, w["ffn_conv_w"], ffn_state, t, tm, tn_act, emit,
                                        "ffn_act", n_chunks=4 if is_prompt else 1)
    xo, *wb_down = _ffn_down(act, w_down, x1, mod3, 5, t, tm_down, tn_down, nk_down, emit)

    k_new = qkv[:, attn_w:attn_w + kv_dim].reshape(n, t, n_kv, HEAD_DIM)
    v_new = qkv[:, attn_w + kv_dim:off_b].reshape(n, t, n_kv, HEAD_DIM)
    if is_prompt:
        keep = min(WINDOW, t)
        k_state, v_state = k_new[:, t - keep:], v_new[:, t - keep:]
        conv_state_out = conv_aux[t // tm_conv - 1::t // tm_conv, SUBLANES - (CONV_K - 1):, :]
        ffn_state_out = ffn_aux[t // tm - 1::t // tm, SUBLANES - (CONV_K - 1):, :]
        return xo, (k_state, v_state, conv_state_out, ffn_state_out)
    k_state = jnp.concatenate([cache_k, k_new], axis=1)[:, t:]
    v_state = jnp.concatenate([cache_v, v_new], axis=1)[:, t:]
    wb = dict(w, w_qkv=wb_qkv[0], w_b=wb_conv[0], w_c=wb_conv[1], w_h=wb_conv[2], w_oa=wb_out[0], w_oc=wb_out[1],
              w_g=wb_act[0], w_u=wb_act[1], w_d=wb_down[0])
    return xo, (k_state, v_state, conv_aux, ffn_aux), wb


def kernel(x_prompt, x_sample, cache_k_win, cache_v_win, state_conv, state_ffn_conv, c_prompt, c_sample, w_mod, b_mod, g_mix, g_ffn, w_in, conv_w, sinks, w_out, w_gate_up, ffn_conv_w, w_down, g_final):
    depth = w_mod.shape[0]
    np_, tp, d = x_prompt.shape
    ns, ts, _ = x_sample.shape
    assert tp % BLOCK == 0 and cache_k_win.shape[2] == WINDOW

    n_c = np_ + ns
    n_c_pad = -(-n_c // SUBLANES) * SUBLANES
    c_all = jnp.concatenate([c_prompt, c_sample, jnp.zeros((n_c_pad - n_c, d), F32)], axis=0)

    xp, xs = x_prompt, x_sample
    outs_p, outs_s = [], []
    for l in range(depth):
        w = dict(g_mix=g_mix[l], g_ffn=g_ffn[l], conv_w=conv_w[l], sinks=sinks[l], ffn_conv_w=ffn_conv_w[l],
                 w_in=w_in[l], w_out=w_out[l], w_gate_up=w_gate_up[l], w_down=w_down[l])
        mod = _mod(c_all, w_mod[l], b_mod[l]).reshape(n_c_pad, N_MOD, d)
        xs2, st_s, wb = _layer(xs, mod[np_:n_c], state_conv[l], state_ffn_conv[l], cache_k_win[l], cache_v_win[l],
                               w, False)
        xp2, st_p = _layer(xp, mod[:np_], None, None, None, None, wb, True)
        xp, xs = xp2.reshape(xp.shape), xs2.reshape(xs.shape)
        outs_p.append(st_p)
        outs_s.append(st_s)

    y_prompt = _final_norm(xp.reshape(np_ * tp, d), g_final).reshape(np_, tp, d)
    y_sample = _final_norm(xs.reshape(ns * ts, d), g_final).reshape(ns, ts, d)
    stack = lambda outs, k: jnp.stack([o[k] for o in outs])
    return (y_prompt, y_sample,
            stack(outs_p, 0), stack(outs_p, 1), stack(outs_p, 2), stack(outs_p, 3),
            stack(outs_s, 0), stack(outs_s, 1), stack(outs_s, 2), stack(outs_s, 3))
```

```python
import functools

import jax
import jax.numpy as jnp
from jax import lax
from jax.experimental import pallas as pl
from jax.experimental.pallas import tpu as pltpu

F32 = jnp.float32
BF16 = jnp.bfloat16

HEAD_DIM = 64
SCALE = HEAD_DIM ** -0.5
GROUP = 8
WINDOW = 128
BLOCK = 128
CONV_K = 3
N_MOD = 6
EPS = 1e-5
SUBLANES = 8
MIB = 1024 * 1024
VMEM_CAP = 56 * MIB


def _params(semantics, vmem_bytes):
    return pltpu.CompilerParams(dimension_semantics=semantics,
                                vmem_limit_bytes=int(min(max(vmem_bytes, 16 * MIB), VMEM_CAP)))


def _dot(a, b):
    return jnp.dot(a, b, preferred_element_type=F32)


def _dot_nt(a, b):
    return lax.dot_general(a, b, (((1,), (1,)), ((), ())), preferred_element_type=F32)


def _weight_tile(w_ref, wout_ref):
    if wout_ref is None:
        return w_ref[...]
    wout_ref[...] = w_ref[...].astype(BF16)
    return wout_ref[...]


def _row_spec(shape, index_map, n_row_tiles):
    if n_row_tiles == 1:
        return pl.BlockSpec(shape, index_map, pipeline_mode=pl.Buffered(1))
    return pl.BlockSpec(shape, index_map)


def _nbytes(shape, dtype):
    n = jnp.dtype(dtype).itemsize
    for s in shape:
        n *= s
    return n


def _mod_body(c_ref, w_ref, b_ref, o_ref):
    c = c_ref[...]
    a = (c * jax.nn.sigmoid(c)).astype(BF16)
    o_ref[...] = _dot(a, w_ref[...].astype(BF16)) + b_ref[...]


def _mod(c_all, w_mod, b_mod, tn=512):
    nc, d = c_all.shape
    n = w_mod.shape[1]
    vmem = 2 * (nc * d * 4 + d * tn * 4 + nc * tn * 4) + d * tn * 2 + nc * d * 4
    return pl.pallas_call(
        _mod_body,
        grid=(n // tn,),
        in_specs=[pl.BlockSpec((nc, d), lambda j: (0, 0)),
                  pl.BlockSpec((d, tn), lambda j: (0, j)),
                  pl.BlockSpec((1, tn), lambda j: (0, j))],
        out_specs=pl.BlockSpec((nc, tn), lambda j: (0, j)),
        out_shape=jax.ShapeDtypeStruct((nc, n), F32),
        compiler_params=_params(("parallel",), vmem + 4 * MIB),
        name="mod",
    )(c_all, w_mod, b_mod.reshape(1, n))


def _rms(x):
    return x * lax.rsqrt(jnp.mean(x * x, axis=-1, keepdims=True) + EPS)


def _norm_mod_body(x_ref, mod_ref, g_ref, o_ref, *, shift_idx, scale_idx):
    x = x_ref[...]
    y = _rms(x) * g_ref[...]
    h = y * (1.0 + mod_ref[:, scale_idx:scale_idx + 1, :]) + mod_ref[:, shift_idx:shift_idx + 1, :]
    o_ref[...] = h.reshape(o_ref.shape).astype(o_ref.dtype)


def _norm_mod(x3, mod3, g, shift_idx, scale_idx, bn, tt):
    n, t, d = x3.shape
    nt = t // tt
    vmem = 2 * (bn * tt * d * 4 + bn * tt * d * 2 + bn * SUBLANES * d * 4) + 3 * bn * tt * d * 4
    return pl.pallas_call(
        functools.partial(_norm_mod_body, shift_idx=shift_idx, scale_idx=scale_idx),
        grid=(n // bn, nt),
        in_specs=[pl.BlockSpec((bn, tt, d), lambda i, j: (i, j, 0)),
                  pl.BlockSpec((bn, N_MOD, d), lambda i, j: (i, 0, 0)),
                  pl.BlockSpec((1, d), lambda i, j: (0, 0))],
        out_specs=pl.BlockSpec((bn * tt, d), lambda i, j: (i * nt + j, 0)),
        out_shape=jax.ShapeDtypeStruct((n * t, d), BF16),
        compiler_params=_params(("parallel", "parallel"), vmem),
        name="norm_mod",
    )(x3, mod3, g.reshape(1, d))


def _final_norm_body(x_ref, g_ref, o_ref):
    o_ref[...] = _rms(x_ref[...]) * g_ref[...]


def _final_norm(x2, g, tm=256):
    m, d = x2.shape
    tm = min(tm, m)
    return pl.pallas_call(
        _final_norm_body,
        grid=(m // tm,),
        in_specs=[pl.BlockSpec((tm, d), lambda i: (i, 0)),
                  pl.BlockSpec((1, d), lambda i: (0, 0))],
        out_specs=pl.BlockSpec((tm, d), lambda i: (i, 0)),
        out_shape=jax.ShapeDtypeStruct((m, d), F32),
        compiler_params=_params(("parallel",), 6 * tm * d * 4),
        name="final_norm",
    )(x2, g.reshape(1, d))


def _matmul_body(a_ref, w_ref, o_ref, wout_ref=None):
    o_ref[...] = _dot(a_ref[...], _weight_tile(w_ref, wout_ref)).astype(o_ref.dtype)


def _matmul(a, w, n_cols, tm, tn, emit_w):
    m, k = a.shape
    n_i = m // tm
    assert n_i == 1 or not emit_w
    out_specs = [pl.BlockSpec((tm, tn), lambda i, j: (i, j))]
    out_shape = [jax.ShapeDtypeStruct((m, n_cols), F32)]
    if emit_w:
        out_specs.append(pl.BlockSpec((k, tn), lambda i, j: (0, j)))
        out_shape.append(jax.ShapeDtypeStruct((k, n_cols), BF16))
    vmem = (min(n_i, 2) * tm * k * 2 + 2 * _nbytes((k, tn), w.dtype) + (2 * k * tn * 2 if emit_w else 0)
            + 3 * tm * tn * 4)
    return pl.pallas_call(
        _matmul_body,
        grid=(n_i, n_cols // tn),
        in_specs=[_row_spec((tm, k), lambda i, j: (i, 0), n_i),
                  pl.BlockSpec((k, tn), lambda i, j: (0, j))],
        out_specs=out_specs, out_shape=out_shape,
        compiler_params=_params(("arbitrary", "arbitrary"), vmem + 4 * MIB),
        name="qkv",
    )(a, w)


def _silu(x):
    return x * jax.nn.sigmoid(x)


def _gated_body(*refs, n_w, emit_w, is_prompt, tiles_per_seq, n_chunks):
    h_ref = refs[0]
    w_refs = refs[1:1 + n_w]
    cw_ref = refs[1 + n_w]
    pos = 2 + n_w
    st_ref = None
    if not is_prompt:
        st_ref = refs[pos]
        pos += 1
    o_ref, aux_ref = refs[pos], refs[pos + 1]
    pos += 2
    wout_refs = refs[pos:pos + n_w] if emit_w else (None,) * n_w
    carry_ref = refs[-1] if is_prompt else None

    ws = [_weight_tile(r, o) for r, o in zip(w_refs, wout_refs)]
    cw = cw_ref[...]
    tm, tn = o_ref.shape
    rm = tm // n_chunks
    nb = rm // SUBLANES
    t = lax.broadcasted_iota(jnp.int32, (nb, SUBLANES, tn), 1)
    if is_prompt:
        i, j = pl.program_id(0), pl.program_id(1)

        @pl.when((i % tiles_per_seq) == 0)
        def _():
            carry_ref[j] = jnp.zeros((SUBLANES, tn), F32)

        tail = carry_ref[j]
    for c in range(n_chunks):
        h = h_ref[c * rm:(c + 1) * rm, :]
        if n_w == 3:
            v = _dot(h, ws[1]) * _dot(h, ws[2])
        else:
            v = _dot(h, ws[0])
        v3 = v.reshape(nb, SUBLANES, tn)
        if is_prompt:
            ext = jnp.concatenate([tail[None], v3], axis=0)
            e1 = pltpu.roll(ext, 1, axis=1)
            e2 = pltpu.roll(ext, 2, axis=1)
            v1 = jnp.where(t == 0, e1[:nb], e1[1:])
            v2 = jnp.where(t < 2, e2[:nb], e2[1:])
            tail = v3[nb - 1]
        else:
            p0 = st_ref[c * nb:(c + 1) * nb, 0:1, :]
            p1 = st_ref[c * nb:(c + 1) * nb, 1:2, :]
            v1 = jnp.where(t == 0, p1, pltpu.roll(v3, 1, axis=1))
            v2 = jnp.where(t == 0, p0, jnp.where(t == 1, p1, pltpu.roll(v3, 2, axis=1)))
            aux_ref[c * nb:(c + 1) * nb] = v3[:, SUBLANES - (CONV_K - 1):, :]
        y = (cw[0:1, :] * v2 + cw[1:2, :] * v1 + cw[2:3, :] * v3).reshape(rm, tn)
        if n_w == 3:
            out = _dot(h, ws[0]) * y
        else:
            out = _silu(y) * _dot(h, ws[1])
        o_ref[c * rm:(c + 1) * rm, :] = out.astype(o_ref.dtype)
    if is_prompt:
        carry_ref[j] = tail
        aux_ref[0] = tail


def _gated_proj(h, ws, col_offsets, n_cols, conv_w, state, seq_len, tm, tn, emit_w, name, n_chunks=1):
    m, k = h.shape
    n_w = len(ws)
    n_i, nj = m // tm, pl.cdiv(n_cols, tn)
    assert all(o % tn == 0 for o in col_offsets)
    offs = [o // tn for o in col_offsets]
    is_prompt = state is None
    spec = ((lambda shape, f: pl.BlockSpec(shape, f)) if is_prompt else
            (lambda shape, f: pl.BlockSpec(shape, lambda j, i: f(i, j))))
    w_specs = [spec((k, tn), functools.partial(lambda i, j, o: (0, o + j), o=o)) for o in offs]
    h_spec = _row_spec((tm, k), (lambda i, j: (i, 0)) if is_prompt else (lambda j, i: (i, 0)), n_i)
    in_specs = [h_spec] + w_specs + [spec((CONV_K, tn), lambda i, j: (0, j))]
    out_specs = [spec((tm, tn), lambda i, j: (i, j))]
    out_shape = [jax.ShapeDtypeStruct((m, n_cols), BF16)]
    args = (h,) + tuple(ws) + (conv_w,)
    if is_prompt:
        assert seq_len % tm == 0
        tiles_per_seq = seq_len // tm
        grid = (n_i, nj)
        out_specs.append(spec((1, SUBLANES, tn), lambda i, j: (i, 0, j)))
        out_shape.append(jax.ShapeDtypeStruct((n_i, SUBLANES, n_cols), F32))
        scratch = [pltpu.VMEM((nj, SUBLANES, tn), F32)]
    else:
        assert seq_len == SUBLANES and tm % SUBLANES == 0
        tiles_per_seq = None
        grid = (nj, n_i)
        nb = tm // SUBLANES
        in_specs.append(spec((nb, CONV_K - 1, tn), lambda i, j: (i, 0, j)))
        out_specs.append(spec((nb, CONV_K - 1, tn), lambda i, j: (i, 0, j)))
        out_shape.append(jax.ShapeDtypeStruct(state.shape, F32))
        scratch = []
        args += (state,)
    if emit_w:
        out_specs += [spec((k, tn), lambda i, j: (0, j))] * n_w
        out_shape += [jax.ShapeDtypeStruct((k, n_cols), BF16)] * n_w
    vmem = (min(n_i, 2) * tm * k * 2 + 2 * sum(_nbytes((k, tn), w.dtype) for w in ws)
            + (2 * n_w * k * tn * 2 if emit_w else 0) + 2 * tm * tn * 2 + 14 * tm * tn * 4)
    return pl.pallas_call(
        functools.partial(_gated_body, n_w=n_w, emit_w=emit_w, is_prompt=is_prompt, tiles_per_seq=tiles_per_seq,
                          n_chunks=n_chunks),
        grid=grid, in_specs=in_specs, out_specs=out_specs, out_shape=out_shape,
        scratch_shapes=scratch, compiler_params=_params(("arbitrary", "arbitrary"), vmem + 4 * MIB), name=name,
    )(*args)


def _sink_softmax(s, mask, sink):
    s = jnp.where(mask, s, -jnp.inf)
    mx = jnp.maximum(jnp.max(s, axis=-1, keepdims=True), sink)
    p = jnp.exp(s - mx)
    denom = jnp.sum(p, axis=-1, keepdims=True) + jnp.exp(sink - mx)
    return p * (1.0 / denom)


def _prompt_attn_body(sink_ref, q_ref, kp_ref, kc_ref, vp_ref, vc_ref, o_ref, *, n_kv):
    b = pl.program_id(1)
    q = (q_ref[...] * SCALE).astype(BF16)
    kk = jnp.concatenate([kp_ref[...], kc_ref[...]], axis=0).astype(BF16)
    vv = jnp.concatenate([vp_ref[...], vc_ref[...]], axis=0).astype(BF16)
    r = lax.broadcasted_iota(jnp.int32, (BLOCK, 2 * BLOCK), 0)
    c = lax.broadcasted_iota(jnp.int32, (BLOCK, 2 * BLOCK), 1)
    mask = (c >= r) & (c <= r + WINDOW) & ((c >= BLOCK) | (b > 0))
    heads = range(n_kv * GROUP)
    col = lambda a, i: a[:, i * HEAD_DIM:(i + 1) * HEAD_DIM]
    s = [_dot_nt(col(q, hd), col(kk, hd // GROUP)) for hd in heads]
    p = [_sink_softmax(s[hd], mask, sink_ref[hd]).astype(BF16) for hd in heads]
    for hd in heads:
        o_ref[:, hd * HEAD_DIM:(hd + 1) * HEAD_DIM] = _dot(p[hd], col(vv, hd // GROUP)).astype(o_ref.dtype)


def _prompt_attention(qkv, sinks, n, t, attn_w, kv_dim):
    nb = t // BLOCK
    kcol = attn_w // kv_dim
    cur = lambda col: (lambda i, b: (i * nb + b, col))
    prev = lambda col: (lambda i, b: (i * nb + jnp.maximum(b - 1, 0), col))
    return pl.pallas_call(
        functools.partial(_prompt_attn_body, n_kv=kv_dim // HEAD_DIM),
        grid=(n, nb),
        in_specs=[pl.BlockSpec(memory_space=pltpu.MemorySpace.SMEM),
                  pl.BlockSpec((BLOCK, attn_w), cur(0)),
                  pl.BlockSpec((BLOCK, kv_dim), prev(kcol)),
                  pl.BlockSpec((BLOCK, kv_dim), cur(kcol)),
                  pl.BlockSpec((BLOCK, kv_dim), prev(kcol + 1)),
                  pl.BlockSpec((BLOCK, kv_dim), cur(kcol + 1))],
        out_specs=pl.BlockSpec((BLOCK, attn_w), cur(0)),
        out_shape=jax.ShapeDtypeStruct((n * t, attn_w), BF16),
        compiler_params=_params(("parallel", "parallel"), 24 * MIB),
        name="prompt_attention",
    )(sinks, qkv, qkv, qkv, qkv, qkv)


def _sample_attn_body(sink_ref, q_ref, kn_ref, vn_ref, ck_ref, cv_ref, o_ref, *, bn, n_kv, t):
    rows = t * GROUP
    cols = 2 * WINDOW
    r = lax.broadcasted_iota(jnp.int32, (rows, cols), 0) // GROUP
    c = lax.broadcasted_iota(jnp.int32, (rows, cols), 1)
    mask = ((c < WINDOW) & (c >= r)) | ((c >= WINDOW) & (c - WINDOW <= r))
    pad = jnp.zeros((cols - WINDOW - t, n_kv * HEAD_DIM), F32)
    pairs = [(n, kh) for n in range(bn) for kh in range(n_kv)]
    col = lambda a, i: a[:, i * HEAD_DIM:(i + 1) * HEAD_DIM]
    kk = [jnp.concatenate([ck_ref[n], kn_ref[n * t:(n + 1) * t, :], pad], axis=0).astype(BF16) for n in range(bn)]
    vv = [jnp.concatenate([cv_ref[n], vn_ref[n * t:(n + 1) * t, :], pad], axis=0).astype(BF16) for n in range(bn)]
    s = [_dot_nt((q_ref[n, kh] * SCALE).astype(BF16), col(kk[n], kh)) for n, kh in pairs]
    p = [_sink_softmax(s[i], mask, sink_ref[kh]).astype(BF16) for i, (n, kh) in enumerate(pairs)]
    for i, (n, kh) in enumerate(pairs):
        o_ref[n, kh] = _dot(p[i], col(vv[n], kh)).astype(o_ref.dtype)


def _sample_attention(q4, qkv, cache_k, cache_v, sink_rows, n, t, attn_w, kv_dim, bn=8):
    n_kv = kv_dim // HEAD_DIM
    kcol = attn_w // kv_dim
    rows = t * GROUP
    return pl.pallas_call(
        functools.partial(_sample_attn_body, bn=bn, n_kv=n_kv, t=t),
        grid=(n // bn,),
        in_specs=[pl.BlockSpec((n_kv, rows, 1), lambda i: (0, 0, 0)),
                  pl.BlockSpec((bn, n_kv, rows, HEAD_DIM), lambda i: (i, 0, 0, 0)),
                  pl.BlockSpec((bn * t, kv_dim), lambda i: (i, kcol)),
                  pl.BlockSpec((bn * t, kv_dim), lambda i: (i, kcol + 1)),
                  pl.BlockSpec((bn, WINDOW, kv_dim), lambda i: (i, 0, 0)),
                  pl.BlockSpec((bn, WINDOW, kv_dim), lambda i: (i, 0, 0))],
        out_specs=pl.BlockSpec((bn, n_kv, rows, HEAD_DIM), lambda i: (i, 0, 0, 0)),
        out_shape=jax.ShapeDtypeStruct((n, n_kv, rows, HEAD_DIM), BF16),
        compiler_params=_params(("parallel",), 24 * MIB),
        name="sample_attention",
    )(sink_rows, q4, qkv, qkv, cache_k, cache_v)


def _seq_blocks(seq_len, tm):
    if seq_len >= tm:
        assert seq_len % tm == 0
        tiles = seq_len // tm
        return 1, (lambda i: i // tiles)
    assert tm % seq_len == 0
    return tm // seq_len, (lambda i: i)


def _gated_residual(x, acc, mod_ref, gate_idx):
    tm, tn = acc.shape
    bn = mod_ref.shape[0]
    gate = mod_ref[:, gate_idx:gate_idx + 1, :]
    y = x.reshape(bn, tm // bn, tn) + gate * acc.reshape(bn, tm // bn, tn)
    return y.reshape(tm, tn)


def _out_proj_body(a_ref, c_ref, wa_ref, wc_ref, x_ref, mod_ref, o_ref, woa_ref=None, woc_ref=None, *, gate_idx):
    acc = _dot(a_ref[...], _weight_tile(wa_ref, woa_ref)) + _dot(c_ref[...], _weight_tile(wc_ref, woc_ref))
    o_ref[...] = _gated_residual(x_ref[...], acc, mod_ref, gate_idx)


def _out_proj(attn, conv, wa, wc, row_blocks, x, mod3, gate_idx, seq_len, tm, tn, emit_w):
    m, ka = attn.shape
    assert conv.shape[1] == ka
    d = x.shape[1]
    n_i = m // tm
    assert n_i == 1 or not emit_w
    bn, seq_of = _seq_blocks(seq_len, tm)
    ra, rc = row_blocks
    out_specs = [pl.BlockSpec((tm, tn), lambda i, j: (i, j))]
    out_shape = [jax.ShapeDtypeStruct((m, d), F32)]
    if emit_w:
        out_specs += [pl.BlockSpec((ka, tn), lambda i, j: (0, j))] * 2
        out_shape += [jax.ShapeDtypeStruct((ka, d), BF16)] * 2
    vmem = (min(n_i, 2) * 2 * tm * ka * 2 + 4 * _nbytes((ka, tn), wa.dtype) + (4 * ka * tn * 2 if emit_w else 0)
            + 7 * tm * tn * 4)
    return pl.pallas_call(
        functools.partial(_out_proj_body, gate_idx=gate_idx),
        grid=(n_i, d // tn),
        in_specs=[_row_spec((tm, ka), lambda i, j: (i, 0), n_i),
                  _row_spec((tm, ka), lambda i, j: (i, 0), n_i),
                  pl.BlockSpec((ka, tn), lambda i, j: (ra, j)),
                  pl.BlockSpec((ka, tn), lambda i, j: (rc, j)),
                  pl.BlockSpec((tm, tn), lambda i, j: (i, j)),
                  pl.BlockSpec((bn, N_MOD, tn), lambda i, j: (seq_of(i), 0, j))],
        out_specs=out_specs, out_shape=out_shape,
        compiler_params=_params(("arbitrary", "arbitrary"), vmem + 4 * MIB),
        name="out_proj",
    )(attn, conv, wa, wc, x, mod3)


def _ffn_down_body(a_ref, w_ref, x_ref, mod_ref, o_ref, wout_ref=None, *, gate_idx, nk):
    if nk == 1:
        acc = _dot(a_ref[...], _weight_tile(w_ref, wout_ref))
        o_ref[...] = _gated_residual(x_ref[...], acc, mod_ref, gate_idx)
    else:
        kk = pl.program_id(2)
        tk = w_ref.shape[0]
        a = a_ref[:, pl.ds(pl.multiple_of(kk * tk, 128), tk)]
        acc = _dot(a, _weight_tile(w_ref, wout_ref))

        @pl.when(kk == 0)
        def _():
            o_ref[...] = _gated_residual(x_ref[...], acc, mod_ref, gate_idx)

        @pl.when(kk > 0)
        def _():
            o_ref[...] = _gated_residual(o_ref[...], acc, mod_ref, gate_idx)


def _ffn_down(act, w_down, x, mod3, gate_idx, seq_len, tm, tn, nk, emit_w):
    m, k = act.shape
    d = x.shape[1]
    tk = k // nk
    assert tk * nk == k and (m == tm or not emit_w)
    bn, seq_of = _seq_blocks(seq_len, tm)
    out_specs = [pl.BlockSpec((tm, tn), lambda i, j, kk: (i, j))]
    out_shape = [jax.ShapeDtypeStruct((m, d), F32)]
    if emit_w:
        out_specs.append(pl.BlockSpec((tk, tn), lambda i, j, kk: (kk, j)))
        out_shape.append(jax.ShapeDtypeStruct((k, d), BF16))
    assert tk % 128 == 0
    n_i = m // tm
    vmem = (min(n_i, 2) * tm * k * 2 + 2 * _nbytes((tk, tn), w_down.dtype) + (2 * tk * tn * 2 if emit_w else 0)
            + 7 * tm * tn * 4)
    return pl.pallas_call(
        functools.partial(_ffn_down_body, gate_idx=gate_idx, nk=nk),
        grid=(n_i, d // tn, nk),
        in_specs=[_row_spec((tm, k), lambda i, j, kk: (i, 0), n_i),
                  pl.BlockSpec((tk, tn), lambda i, j, kk: (kk, j)),
                  pl.BlockSpec((tm, tn), lambda i, j, kk: (i, j)),
                  pl.BlockSpec((bn, N_MOD, tn), lambda i, j, kk: (seq_of(i), 0, j))],
        out_specs=out_specs, out_shape=out_shape,
        compiler_params=_params(("arbitrary", "arbitrary", "arbitrary"), vmem + 4 * MIB),
        name="ffn_down",
    )(act, w_down, x, mod3)


def _layer(x3, mod3, conv_state, ffn_state, cache_k, cache_v, w, is_prompt):
    n, t, d = x3.shape
    m = n * t
    attn_w = d // 2
    d_conv = d - attn_w
    n_q = attn_w // HEAD_DIM
    n_kv = max(1, n_q // GROUP)
    kv_dim = n_kv * HEAD_DIM
    off_b = attn_w + 2 * kv_dim
    x2 = x3.reshape(m, d)
    emit = not is_prompt

    if is_prompt:
        tm = min(1024, t)
        bn, tt = 1, min(256, t)
        tn_qkv, tn_conv, tn_out, tn_act, tn_down, nk_down = 512, 512, 512, 512, 512, 1
        w_conv, off_conv = (w["w_b"], w["w_c"], w["w_h"]), (0, 0, 0)
        w_act, off_act = (w["w_g"], w["w_u"]), (0, 0)
        w_oa, w_oc, out_rows = w["w_oa"], w["w_oc"], (0, 0)
        w_qkv, w_down = w["w_qkv"], w["w_d"]
        d_ff = w_down.shape[0]
    else:
        assert t == SUBLANES
        tm = min(1024, m)
        bn, tt = min(32, n), t
        tn_qkv, tn_conv, tn_out, tn_act, tn_down, nk_down = 256, 256, 256, 256, 256, 2
        d_ff = w["w_down"].shape[0]
        w_conv, off_conv = (w["w_in"],) * 3, (off_b, off_b + d_conv, off_b + 2 * d_conv)
        w_act, off_act = (w["w_gate_up"],) * 2, (0, d_ff)
        w_oa = w_oc = w["w_out"]
        out_rows = (0, 1)
        w_qkv, w_down = w["w_in"], w["w_down"]
    tm_conv = min(512, tm)
    tm_down = min(512, tm) if is_prompt else tm

    h = _norm_mod(x3, mod3, w["g_mix"], 0, 1, bn, tt)
    qkv, *wb_qkv = _matmul(h, w_qkv, off_b, tm, tn_qkv, emit)
    conv, conv_aux, *wb_conv = _gated_proj(h, w_conv, off_conv, d_conv, w["conv_w"], conv_state, t, tm_conv,
                                           tn_conv, emit, "conv_gate", n_chunks=2 if is_prompt else 1)
    if is_prompt:
        attn = _prompt_attention(qkv, w["sinks"], n, t, attn_w, kv_dim)
    else:
        q4 = qkv[:, :attn_w].reshape(n, t, n_kv, GROUP, HEAD_DIM).transpose(0, 2, 1, 3, 4)
        q4 = q4.reshape(n, n_kv, t * GROUP, HEAD_DIM)
        sink_rows = jnp.tile(w["sinks"].reshape(n_kv, 1, GROUP), (1, t, 1)).reshape(n_kv, t * GROUP, 1)
        o4 = _sample_attention(q4, qkv, cache_k.reshape(n, WINDOW, kv_dim), cache_v.reshape(n, WINDOW, kv_dim),
                               sink_rows, n, t, attn_w, kv_dim, bn=min(8, n))
        attn = o4.reshape(n, n_kv, t, GROUP, HEAD_DIM).transpose(0, 2, 1, 3, 4).reshape(m, attn_w)
    x1, *wb_out = _out_proj(attn, conv, w_oa, w_oc, out_rows, x2, mod3, 2, t, tm, tn_out, emit)

    h2 = _norm_mod(x1.reshape(n, t, d), mod3, w["g_ffn"], 3, 4, bn, tt)
    act, ffn_aux, *wb_act = _gated_proj(h2, w_act, off_act, d_ff, w["ffn_conv_w"], ffn_state, t, tm, tn_act, emit,
                                        "ffn_act", n_chunks=4 if is_prompt else 1)
    xo, *wb_down = _ffn_down(act, w_down, x1, mod3, 5, t, tm_down, tn_down, nk_down, emit)

    k_new = qkv[:, attn_w:attn_w + kv_dim].reshape(n, t, n_kv, HEAD_DIM)
    v_new = qkv[:, attn_w + kv_dim:off_b].reshape(n, t, n_kv, HEAD_DIM)
    if is_prompt:
        keep = min(WINDOW, t)
        k_state, v_state = k_new[:, t - keep:], v_new[:, t - keep:]
        conv_state_out = conv_aux[t // tm_conv - 1::t // tm_conv, SUBLANES - (CONV_K - 1):, :]
        ffn_state_out = ffn_aux[t // tm - 1::t // tm, SUBLANES - (CONV_K - 1):, :]
        return xo, (k_state, v_state, conv_state_out, ffn_state_out)
    k_state = jnp.concatenate([cache_k, k_new], axis=1)[:, t:]
    v_state = jnp.concatenate([cache_v, v_new], axis=1)[:, t:]
    wb = dict(w, w_qkv=wb_qkv[0], w_b=wb_conv[0], w_c=wb_conv[1], w_h=wb_conv[2], w_oa=wb_out[0], w_oc=wb_out[1],
              w_g=wb_act[0], w_u=wb_act[1], w_d=wb_down[0])
    return xo, (k_state, v_state, conv_aux, ffn_aux), wb


def kernel(x_prompt, x_sample, cache_k_win, cache_v_win, state_conv, state_ffn_conv, c_prompt, c_sample, w_mod, b_mod, g_mix, g_ffn, w_in, conv_w, sinks, w_out, w_gate_up, ffn_conv_w, w_down, g_final):
    depth = w_mod.shape[0]
    np_, tp, d = x_prompt.shape
    ns, ts, _ = x_sample.shape
    assert tp % BLOCK == 0 and cache_k_win.shape[2] == WINDOW

    n_c = np_ + ns
    n_c_pad = -(-n_c // SUBLANES) * SUBLANES
    c_all = jnp.concatenate([c_prompt, c_sample, jnp.zeros((n_c_pad - n_c, d), F32)], axis=0)

    xp, xs = x_prompt, x_sample
    outs_p, outs_s = [], []
    for l in range(depth):
        w = dict(g_mix=g_mix[l], g_ffn=g_ffn[l], conv_w=conv_w[l], sinks=sinks[l], ffn_conv_w=ffn_conv_w[l],
                 w_in=w_in[l], w_out=w_out[l], w_gate_up=w_gate_up[l], w_down=w_down[l])
        mod = _mod(c_all, w_mod[l], b_mod[l]).reshape(n_c_pad, N_MOD, d)
        xs2, st_s, wb = _layer(xs, mod[np_:n_c], state_conv[l], state_ffn_conv[l], cache_k_win[l], cache_v_win[l],
                               w, False)
        xp2, st_p = _layer(xp, mod[:np_], None, None, None, None, wb, True)
        xp, xs = xp2.reshape(xp.shape), xs2.reshape(xs.shape)
        outs_p.append(st_p)
        outs_s.append(st_s)

    y_prompt = _final_norm(xp.reshape(np_ * tp, d), g_final).reshape(np_, tp, d)
    y_sample = _final_norm(xs.reshape(ns * ts, d), g_final).reshape(ns, ts, d)
    stack = lambda outs, k: jnp.stack([o[k] for o in outs])
    return (y_prompt, y_sample,
            stack(outs_p, 0), stack(outs_p, 1), stack(outs_p, 2), stack(outs_p, 3),
            stack(outs_s, 0), stack(outs_s, 1), stack(outs_s, 2), stack(outs_s, 3))
```

```python
import functools

import jax
import jax.numpy as jnp
from jax import lax
from jax.experimental import pallas as pl
from jax.experimental.pallas import tpu as pltpu

F32 = jnp.float32
BF16 = jnp.bfloat16

HEAD_DIM = 64
SCALE = HEAD_DIM ** -0.5
GROUP = 8
WINDOW = 128
BLOCK = 128
CONV_K = 3
N_MOD = 6
EPS = 1e-5
SUBLANES = 8
MIB = 1024 * 1024
VMEM_CAP = 56 * MIB


def _params(semantics, vmem_bytes):
    return pltpu.CompilerParams(dimension_semantics=semantics,
                                vmem_limit_bytes=int(min(max(vmem_bytes, 16 * MIB), VMEM_CAP)))


def _dot(a, b):
    return jnp.dot(a, b, preferred_element_type=F32)


def _dot_nt(a, b):
    return lax.dot_general(a, b, (((1,), (1,)), ((), ())), preferred_element_type=F32)


def _weight_tile(w_ref, wout_ref):
    if wout_ref is None:
        return w_ref[...]
    wout_ref[...] = w_ref[...].astype(BF16)
    return wout_ref[...]


def _row_spec(shape, index_map, n_row_tiles):
    if n_row_tiles == 1:
        return pl.BlockSpec(shape, index_map, pipeline_mode=pl.Buffered(1))
    return pl.BlockSpec(shape, index_map)


def _nbytes(shape, dtype):
    n = jnp.dtype(dtype).itemsize
    for s in shape:
        n *= s
    return n


def _mod_body(c_ref, w_ref, b_ref, o_ref):
    c = c_ref[...]
    a = (c * jax.nn.sigmoid(c)).astype(BF16)
    o_ref[...] = _dot(a, w_ref[...].astype(BF16)) + b_ref[...]


def _mod(c_all, w_mod, b_mod, tn=512):
    nc, d = c_all.shape
    n = w_mod.shape[1]
    vmem = 2 * (nc * d * 4 + d * tn * 4 + nc * tn * 4) + d * tn * 2 + nc * d * 4
    return pl.pallas_call(
        _mod_body,
        grid=(n // tn,),
        in_specs=[pl.BlockSpec((nc, d), lambda j: (0, 0)),
                  pl.BlockSpec((d, tn), lambda j: (0, j)),
                  pl.BlockSpec((1, tn), lambda j: (0, j))],
        out_specs=pl.BlockSpec((nc, tn), lambda j: (0, j)),
        out_shape=jax.ShapeDtypeStruct((nc, n), F32),
        compiler_params=_params(("parallel",), vmem + 4 * MIB),
        name="mod",
    )(c_all, w_mod, b_mod.reshape(1, n))


def _rms(x):
    return x * lax.rsqrt(jnp.mean(x * x, axis=-1, keepdims=True) + EPS)


def _norm_mod_body(x_ref, mod_ref, g_ref, o_ref, *, shift_idx, scale_idx):
    x = x_ref[...]
    y = _rms(x) * g_ref[...]
    h = y * (1.0 + mod_ref[:, scale_idx:scale_idx + 1, :]) + mod_ref[:, shift_idx:shift_idx + 1, :]
    o_ref[...] = h.reshape(o_ref.shape).astype(o_ref.dtype)


def _norm_mod(x3, mod3, g, shift_idx, scale_idx, bn, tt):
    n, t, d = x3.shape
    nt = t // tt
    vmem = 2 * (bn * tt * d * 4 + bn * tt * d * 2 + bn * SUBLANES * d * 4) + 3 * bn * tt * d * 4
    return pl.pallas_call(
        functools.partial(_norm_mod_body, shift_idx=shift_idx, scale_idx=scale_idx),
        grid=(n // bn, nt),
        in_specs=[pl.BlockSpec((bn, tt, d), lambda i, j: (i, j, 0)),
                  pl.BlockSpec((bn, N_MOD, d), lambda i, j: (i, 0, 0)),
                  pl.BlockSpec((1, d), lambda i, j: (0, 0))],
        out_specs=pl.BlockSpec((bn * tt, d), lambda i, j: (i * nt + j, 0)),
        out_shape=jax.ShapeDtypeStruct((n * t, d), BF16),
        compiler_params=_params(("parallel", "parallel"), vmem),
        name="norm_mod",
    )(x3, mod3, g.reshape(1, d))


def _final_norm_body(x_ref, g_ref, o_ref):
    o_ref[...] = _rms(x_ref[...]) * g_ref[...]


def _final_norm(x2, g, tm=256):
    m, d = x2.shape
    tm = min(tm, m)
    return pl.pallas_call(
        _final_norm_body,
        grid=(m // tm,),
        in_specs=[pl.BlockSpec((tm, d), lambda i: (i, 0)),
                  pl.BlockSpec((1, d), lambda i: (0, 0))],
        out_specs=pl.BlockSpec((tm, d), lambda i: (i, 0)),
        out_shape=jax.ShapeDtypeStruct((m, d), F32),
        compiler_params=_params(("parallel",), 6 * tm * d * 4),
        name="final_norm",
    )(x2, g.reshape(1, d))


def _matmul_body(a_ref, w_ref, o_ref, wout_ref=None):
    o_ref[...] = _dot(a_ref[...], _weight_tile(w_ref, wout_ref)).astype(o_ref.dtype)


def _matmul(a, w, n_cols, tm, tn, emit_w):
    m, k = a.shape
    n_i = m // tm
    assert n_i == 1 or not emit_w
    out_specs = [pl.BlockSpec((tm, tn), lambda i, j: (i, j))]
    out_shape = [jax.ShapeDtypeStruct((m, n_cols), F32)]
    if emit_w:
        out_specs.append(pl.BlockSpec((k, tn), lambda i, j: (0, j)))
        out_shape.append(jax.ShapeDtypeStruct((k, n_cols), BF16))
    vmem = (min(n_i, 2) * tm * k * 2 + 2 * _nbytes((k, tn), w.dtype) + (2 * k * tn * 2 if emit_w else 0)
            + 3 * tm * tn * 4)
    return pl.pallas_call(
        _matmul_body,
        grid=(n_i, n_cols // tn),
        in_specs=[_row_spec((tm, k), lambda i, j: (i, 0), n_i),
                  pl.BlockSpec((k, tn), lambda i, j: (0, j))],
        out_specs=out_specs, out_shape=out_shape,
        compiler_params=_params(("arbitrary", "arbitrary"), vmem + 4 * MIB),
        name="qkv",
    )(a, w)


def _silu(x):
    return x * jax.nn.sigmoid(x)


def _gated_body(*refs, n_w, emit_w, is_prompt, tiles_per_seq, n_chunks):
    h_ref = refs[0]
    w_refs = refs[1:1 + n_w]
    cw_ref = refs[1 + n_w]
    pos = 2 + n_w
    st_ref = None
    if not is_prompt:
        st_ref = refs[pos]
        pos += 1
    o_ref, aux_ref = refs[pos], refs[pos + 1]
    pos += 2
    wout_refs = refs[pos:pos + n_w] if emit_w else (None,) * n_w
    carry_ref = refs[-1] if is_prompt else None

    ws = [_weight_tile(r, o) for r, o in zip(w_refs, wout_refs)]
    cw = cw_ref[...]
    tm, tn = o_ref.shape
    rm = tm // n_chunks
    nb = rm // SUBLANES
    t = lax.broadcasted_iota(jnp.int32, (nb, SUBLANES, tn), 1)
    if is_prompt:
        i, j = pl.program_id(0), pl.program_id(1)

        @pl.when((i % tiles_per_seq) == 0)
        def _():
            carry_ref[j] = jnp.zeros((SUBLANES, tn), F32)

        tail = carry_ref[j]
    for c in range(n_chunks):
        h = h_ref[c * rm:(c + 1) * rm, :]
        if n_w == 3:
            v = _dot(h, ws[1]) * _dot(h, ws[2])
        else:
            v = _dot(h, ws[0])
        v3 = v.reshape(nb, SUBLANES, tn)
        if is_prompt:
            ext = jnp.concatenate([tail[None], v3], axis=0)
            e1 = pltpu.roll(ext, 1, axis=1)
            e2 = pltpu.roll(ext, 2, axis=1)
            v1 = jnp.where(t == 0, e1[:nb], e1[1:])
            v2 = jnp.where(t < 2, e2[:nb], e2[1:])
            tail = v3[nb - 1]
        else:
            p0 = st_ref[c * nb:(c + 1) * nb, 0:1, :]
            p1 = st_ref[c * nb:(c + 1) * nb, 1:2, :]
            v1 = jnp.where(t == 0, p1, pltpu.roll(v3, 1, axis=1))
            v2 = jnp.where(t == 0, p0, jnp.where(t == 1, p1, pltpu.roll(v3, 2, axis=1)))
            aux_ref[c * nb:(c + 1) * nb] = v3[:, SUBLANES - (CONV_K - 1):, :]
        y = (cw[0:1, :] * v2 + cw[1:2, :] * v1 + cw[2:3, :] * v3).reshape(rm, tn)
        if n_w == 3:
            out = _dot(h, ws[0]) * y
        else:
            out = _silu(y) * _dot(h, ws[1])
        o_ref[c * rm:(c + 1) * rm, :] = out.astype(o_ref.dtype)
    if is_prompt:
        carry_ref[j] = tail
        aux_ref[0] = tail


def _gated_proj(h, ws, col_offsets, n_cols, conv_w, state, seq_len, tm, tn, emit_w, name, n_chunks=1):
    m, k = h.shape
    n_w = len(ws)
    n_i, nj = m // tm, pl.cdiv(n_cols, tn)
    assert all(o % tn == 0 for o in col_offsets)
    offs = [o // tn for o in col_offsets]
    is_prompt = state is None
    spec = ((lambda shape, f: pl.BlockSpec(shape, f)) if is_prompt else
            (lambda shape, f: pl.BlockSpec(shape, lambda j, i: f(i, j))))
    w_specs = [spec((k, tn), functools.partial(lambda i, j, o: (0, o + j), o=o)) for o in offs]
    h_spec = _row_spec((tm, k), (lambda i, j: (i, 0)) if is_prompt else (lambda j, i: (i, 0)), n_i)
    in_specs = [h_spec] + w_specs + [spec((CONV_K, tn), lambda i, j: (0, j))]
    out_specs = [spec((tm, tn), lambda i, j: (i, j))]
    out_shape = [jax.ShapeDtypeStruct((m, n_cols), BF16)]
    args = (h,) + tuple(ws) + (conv_w,)
    if is_prompt:
        assert seq_len % tm == 0
        tiles_per_seq = seq_len // tm
        grid = (n_i, nj)
        out_specs.append(spec((1, SUBLANES, tn), lambda i, j: (i, 0, j)))
        out_shape.append(jax.ShapeDtypeStruct((n_i, SUBLANES, n_cols), F32))
        scratch = [pltpu.VMEM((nj, SUBLANES, tn), F32)]
    else:
        assert seq_len == SUBLANES and tm % SUBLANES == 0
        tiles_per_seq = None
        grid = (nj, n_i)
        nb = tm // SUBLANES
        in_specs.append(spec((nb, CONV_K - 1, tn), lambda i, j: (i, 0, j)))
        out_specs.append(spec((nb, CONV_K - 1, tn), lambda i, j: (i, 0, j)))
        out_shape.append(jax.ShapeDtypeStruct(state.shape, F32))
        scratch = []
        args += (state,)
    if emit_w:
        out_specs += [spec((k, tn), lambda i, j: (0, j))] * n_w
        out_shape += [jax.ShapeDtypeStruct((k, n_cols), BF16)] * n_w
    vmem = (min(n_i, 2) * tm * k * 2 + 2 * sum(_nbytes((k, tn), w.dtype) for w in ws)
            + (2 * n_w * k * tn * 2 if emit_w else 0) + 2 * tm * tn * 2 + 14 * tm * tn * 4)
    return pl.pallas_call(
        functools.partial(_gated_body, n_w=n_w, emit_w=emit_w, is_prompt=is_prompt, tiles_per_seq=tiles_per_seq,
                          n_chunks=n_chunks),
        grid=grid, in_specs=in_specs, out_specs=out_specs, out_shape=out_shape,
        scratch_shapes=scratch, compiler_params=_params(("arbitrary", "arbitrary"), vmem + 4 * MIB), name=name,
    )(*args)


def _sink_softmax(s, mask, sink, axis):
    s = jnp.where(mask, s, -jnp.inf)
    mx = jnp.maximum(jnp.max(s, axis=axis, keepdims=True), sink)
    p = jnp.exp(s - mx)
    denom = jnp.sum(p, axis=axis, keepdims=True) + jnp.exp(sink - mx)
    return p * (1.0 / denom)


def _dot_tn(a, b):
    return lax.dot_general(a, b, (((0,), (0,)), ((), ())), preferred_element_type=F32)


def _block_diag_pair(tile, e):
    lo = lax.broadcasted_iota(jnp.int32, tile.shape, 1) < HEAD_DIM
    rolled = pltpu.roll(tile, HEAD_DIM, axis=1)
    zero = jnp.zeros_like(tile)
    if e == 0:
        top, bot = jnp.where(lo, tile, zero), jnp.where(lo, zero, rolled)
    else:
        top, bot = jnp.where(lo, rolled, zero), jnp.where(lo, zero, tile)
    return jnp.concatenate([top, bot], axis=0).astype(BF16)


def _prompt_attn_body(sink_ref, q_ref, kp_ref, kc_ref, vp_ref, vc_ref, o_ref, *, n_kv):
    b = pl.program_id(1)
    kk = jnp.concatenate([kp_ref[...], kc_ref[...]], axis=0)
    vv = jnp.concatenate([vp_ref[...], vc_ref[...]], axis=0)
    c = lax.broadcasted_iota(jnp.int32, (2 * BLOCK, BLOCK), 0)
    r = lax.broadcasted_iota(jnp.int32, (2 * BLOCK, BLOCK), 1)
    mask = ((c >= r) & (c <= r + WINDOW) & ((c >= BLOCK) | (b > 0)))[None]
    first = lax.broadcasted_iota(jnp.int32, (2, 1, BLOCK), 0) == 0
    pairs_per_kv = GROUP // 2
    kbd, vbd = [], []
    for kh in range(n_kv):
        sl = slice((kh // 2) * 2 * HEAD_DIM, (kh // 2 + 1) * 2 * HEAD_DIM)
        kbd.append(_block_diag_pair(kk[:, sl], kh % 2))
        vbd.append(_block_diag_pair(vv[:, sl], kh % 2))
    n_pairs = n_kv * pairs_per_kv
    qp = [(q_ref[:, i * 2 * HEAD_DIM:(i + 1) * 2 * HEAD_DIM] * SCALE).astype(BF16) for i in range(n_pairs)]
    s = [_dot_nt(kbd[i // pairs_per_kv], qp[i]).reshape(2, 2 * BLOCK, BLOCK) for i in range(n_pairs)]
    p = []
    for i in range(n_pairs):
        sink = jnp.where(first, sink_ref[2 * i], sink_ref[2 * i + 1])
        p.append(_sink_softmax(s[i], mask, sink, axis=1).astype(BF16).reshape(4 * BLOCK, BLOCK))
    for i in range(n_pairs):
        o_ref[:, i * 2 * HEAD_DIM:(i + 1) * 2 * HEAD_DIM] = _dot_tn(p[i], vbd[i // pairs_per_kv]).astype(o_ref.dtype)


def _prompt_attention(qkv, sinks, n, t, attn_w, kv_dim):
    nb = t // BLOCK
    kcol = attn_w // kv_dim
    cur = lambda col: (lambda i, b: (i * nb + b, col))
    prev = lambda col: (lambda i, b: (i * nb + jnp.maximum(b - 1, 0), col))
    return pl.pallas_call(
        functools.partial(_prompt_attn_body, n_kv=kv_dim // HEAD_DIM),
        grid=(n, nb),
        in_specs=[pl.BlockSpec(memory_space=pltpu.MemorySpace.SMEM),
                  pl.BlockSpec((BLOCK, attn_w), cur(0)),
                  pl.BlockSpec((BLOCK, kv_dim), prev(kcol)),
                  pl.BlockSpec((BLOCK, kv_dim), cur(kcol)),
                  pl.BlockSpec((BLOCK, kv_dim), prev(kcol + 1)),
                  pl.BlockSpec((BLOCK, kv_dim), cur(kcol + 1))],
        out_specs=pl.BlockSpec((BLOCK, attn_w), cur(0)),
        out_shape=jax.ShapeDtypeStruct((n * t, attn_w), BF16),
        compiler_params=_params(("parallel", "parallel"), 24 * MIB),
        name="prompt_attention",
    )(sinks, qkv, qkv, qkv, qkv, qkv)


def _sample_attn_body(sink_ref, q_ref, kn_ref, vn_ref, ck_ref, cv_ref, o_ref, *, bn, n_kv, t):
    rows = t * GROUP
    keys = 2 * WINDOW
    n_pairs = n_kv // 2
    c = lax.broadcasted_iota(jnp.int32, (keys, 2 * rows), 0)
    tok = (lax.broadcasted_iota(jnp.int32, (keys, 2 * rows), 1) % rows) // GROUP
    mask = ((c < WINDOW) & (c >= tok)) | ((c >= WINDOW) & (c - WINDOW <= tok))
    pad = jnp.zeros((keys - WINDOW - t, n_kv * HEAD_DIM), F32)
    zero = jnp.zeros((rows, HEAD_DIM), F32)
    work = [(n, pr) for n in range(bn) for pr in range(n_pairs)]
    lanes = lambda a, pr: a[:, pr * 2 * HEAD_DIM:(pr + 1) * 2 * HEAD_DIM]
    kk = [jnp.concatenate([ck_ref[n], kn_ref[n * t:(n + 1) * t, :], pad], axis=0).astype(BF16) for n in range(bn)]
    vv = [jnp.concatenate([cv_ref[n], vn_ref[n * t:(n + 1) * t, :], pad], axis=0).astype(BF16) for n in range(bn)]

    def q_block_diag(n, pr):
        qa, qb = q_ref[n, 2 * pr] * SCALE, q_ref[n, 2 * pr + 1] * SCALE
        return jnp.concatenate([jnp.concatenate([qa, zero], axis=1),
                                jnp.concatenate([zero, qb], axis=1)], axis=0).astype(BF16)

    s = [_dot_nt(lanes(kk[n], pr), q_block_diag(n, pr)) for n, pr in work]
    p = [_sink_softmax(s[i], mask, sink_ref[pr], axis=0).astype(BF16) for i, (n, pr) in enumerate(work)]
    for i, (n, pr) in enumerate(work):
        o_ref[n, pr] = _dot_tn(p[i], lanes(vv[n], pr)).astype(o_ref.dtype)


def _sample_attention(q4, qkv, cache_k, cache_v, sink_lanes, n, t, attn_w, kv_dim, bn=8):
    n_kv = kv_dim // HEAD_DIM
    assert n_kv % 2 == 0
    kcol = attn_w // kv_dim
    rows = t * GROUP
    return pl.pallas_call(
        functools.partial(_sample_attn_body, bn=bn, n_kv=n_kv, t=t),
        grid=(n // bn,),
        in_specs=[pl.BlockSpec((n_kv // 2, 1, 2 * rows), lambda i: (0, 0, 0)),
                  pl.BlockSpec((bn, n_kv, rows, HEAD_DIM), lambda i: (i, 0, 0, 0)),
                  pl.BlockSpec((bn * t, kv_dim), lambda i: (i, kcol)),
                  pl.BlockSpec((bn * t, kv_dim), lambda i: (i, kcol + 1)),
                  pl.BlockSpec((bn, WINDOW, kv_dim), lambda i: (i, 0, 0)),
                  pl.BlockSpec((bn, WINDOW, kv_dim), lambda i: (i, 0, 0))],
        out_specs=pl.BlockSpec((bn, n_kv // 2, 2 * rows, 2 * HEAD_DIM), lambda i: (i, 0, 0, 0)),
        out_shape=jax.ShapeDtypeStruct((n, n_kv // 2, 2 * rows, 2 * HEAD_DIM), BF16),
        compiler_params=_params(("parallel",), 24 * MIB),
        name="sample_attention",
    )(sink_lanes, q4, qkv, qkv, cache_k, cache_v)


def _seq_blocks(seq_len, tm):
    if seq_len >= tm:
        assert seq_len % tm == 0
        tiles = seq_len // tm
        return 1, (lambda i: i // tiles)
    assert tm % seq_len == 0
    return tm // seq_len, (lambda i: i)


def _gated_residual(x, acc, mod_ref, gate_idx):
    tm, tn = acc.shape
    bn = mod_ref.shape[0]
    gate = mod_ref[:, gate_idx:gate_idx + 1, :]
    y = x.reshape(bn, tm // bn, tn) + gate * acc.reshape(bn, tm // bn, tn)
    return y.reshape(tm, tn)


def _out_proj_body(a_ref, c_ref, wa_ref, wc_ref, x_ref, mod_ref, o_ref, woa_ref=None, woc_ref=None, *, gate_idx):
    acc = _dot(a_ref[...], _weight_tile(wa_ref, woa_ref)) + _dot(c_ref[...], _weight_tile(wc_ref, woc_ref))
    o_ref[...] = _gated_residual(x_ref[...], acc, mod_ref, gate_idx)


def _out_proj(attn, conv, wa, wc, row_blocks, x, mod3, gate_idx, seq_len, tm, tn, emit_w):
    m, ka = attn.shape
    assert conv.shape[1] == ka
    d = x.shape[1]
    n_i = m // tm
    assert n_i == 1 or not emit_w
    bn, seq_of = _seq_blocks(seq_len, tm)
    ra, rc = row_blocks
    out_specs = [pl.BlockSpec((tm, tn), lambda i, j: (i, j))]
    out_shape = [jax.ShapeDtypeStruct((m, d), F32)]
    if emit_w:
        out_specs += [pl.BlockSpec((ka, tn), lambda i, j: (0, j))] * 2
        out_shape += [jax.ShapeDtypeStruct((ka, d), BF16)] * 2
    vmem = (min(n_i, 2) * 2 * tm * ka * 2 + 4 * _nbytes((ka, tn), wa.dtype) + (4 * ka * tn * 2 if emit_w else 0)
            + 7 * tm * tn * 4)
    return pl.pallas_call(
        functools.partial(_out_proj_body, gate_idx=gate_idx),
        grid=(n_i, d // tn),
        in_specs=[_row_spec((tm, ka), lambda i, j: (i, 0), n_i),
                  _row_spec((tm, ka), lambda i, j: (i, 0), n_i),
                  pl.BlockSpec((ka, tn), lambda i, j: (ra, j)),
                  pl.BlockSpec((ka, tn), lambda i, j: (rc, j)),
                  pl.BlockSpec((tm, tn), lambda i, j: (i, j)),
                  pl.BlockSpec((bn, N_MOD, tn), lambda i, j: (seq_of(i), 0, j))],
        out_specs=out_specs, out_shape=out_shape,
        compiler_params=_params(("arbitrary", "arbitrary"), vmem + 4 * MIB),
        name="out_proj",
    )(attn, conv, wa, wc, x, mod3)


def _ffn_down_body(a_ref, w_ref, x_ref, mod_ref, o_ref, wout_ref=None, *, gate_idx, nk):
    if nk == 1:
        acc = _dot(a_ref[...], _weight_tile(w_ref, wout_ref))
        o_ref[...] = _gated_residual(x_ref[...], acc, mod_ref, gate_idx)
    else:
        kk = pl.program_id(2)
        tk = w_ref.shape[0]
        a = a_ref[:, pl.ds(pl.multiple_of(kk * tk, 128), tk)]
        acc = _dot(a, _weight_tile(w_ref, wout_ref))

        @pl.when(kk == 0)
        def _():
            o_ref[...] = _gated_residual(x_ref[...], acc, mod_ref, gate_idx)

        @pl.when(kk > 0)
        def _():
            o_ref[...] = _gated_residual(o_ref[...], acc, mod_ref, gate_idx)


def _ffn_down(act, w_down, x, mod3, gate_idx, seq_len, tm, tn, nk, emit_w):
    m, k = act.shape
    d = x.shape[1]
    tk = k // nk
    assert tk * nk == k and (m == tm or not emit_w)
    bn, seq_of = _seq_blocks(seq_len, tm)
    out_specs = [pl.BlockSpec((tm, tn), lambda i, j, kk: (i, j))]
    out_shape = [jax.ShapeDtypeStruct((m, d), F32)]
    if emit_w:
        out_specs.append(pl.BlockSpec((tk, tn), lambda i, j, kk: (kk, j)))
        out_shape.append(jax.ShapeDtypeStruct((k, d), BF16))
    assert tk % 128 == 0
    n_i = m // tm
    vmem = (min(n_i, 2) * tm * k * 2 + 2 * _nbytes((tk, tn), w_down.dtype) + (2 * tk * tn * 2 if emit_w else 0)
            + 7 * tm * tn * 4)
    return pl.pallas_call(
        functools.partial(_ffn_down_body, gate_idx=gate_idx, nk=nk),
        grid=(n_i, d // tn, nk),
        in_specs=[_row_spec((tm, k), lambda i, j, kk: (i, 0), n_i),
                  pl.BlockSpec((tk, tn), lambda i, j, kk: (kk, j)),
                  pl.BlockSpec((tm, tn), lambda i, j, kk: (i, j)),
                  pl.BlockSpec((bn, N_MOD, tn), lambda i, j, kk: (seq_of(i), 0, j))],
        out_specs=out_specs, out_shape=out_shape,
        compiler_params=_params(("arbitrary", "arbitrary", "arbitrary"), vmem + 4 * MIB),
        name="ffn_down",
    )(act, w_down, x, mod3)


def _layer(x3, mod3, conv_state, ffn_state, cache_k, cache_v, w, is_prompt):
    n, t, d = x3.shape
    m = n * t
    attn_w = d // 2
    d_conv = d - attn_w
    n_q = attn_w // HEAD_DIM
    n_kv = max(1, n_q // GROUP)
    kv_dim = n_kv * HEAD_DIM
    off_b = attn_w + 2 * kv_dim
    x2 = x3.reshape(m, d)
    emit = not is_prompt

    if is_prompt:
        tm = min(1024, t)
        bn, tt = 1, min(256, t)
        tn_qkv, tn_conv, tn_out, tn_act, tn_down, nk_down = 512, 512, 512, 512, 512, 1
        w_conv, off_conv = (w["w_b"], w["w_c"], w["w_h"]), (0, 0, 0)
        w_act, off_act = (w["w_g"], w["w_u"]), (0, 0)
        w_oa, w_oc, out_rows = w["w_oa"], w["w_oc"], (0, 0)
        w_qkv, w_down = w["w_qkv"], w["w_d"]
        d_ff = w_down.shape[0]
    else:
        assert t == SUBLANES
        tm = min(1024, m)
        bn, tt = min(32, n), t
        tn_qkv, tn_conv, tn_out, tn_act, tn_down, nk_down = 256, 256, 256, 256, 256, 2
        d_ff = w["w_down"].shape[0]
        w_conv, off_conv = (w["w_in"],) * 3, (off_b, off_b + d_conv, off_b + 2 * d_conv)
        w_act, off_act = (w["w_gate_up"],) * 2, (0, d_ff)
        w_oa = w_oc = w["w_out"]
        out_rows = (0, 1)
        w_qkv, w_down = w["w_in"], w["w_down"]
    tm_conv = min(512, tm)
    tm_down = min(512, tm) if is_prompt else tm

    h = _norm_mod(x3, mod3, w["g_mix"], 0, 1, bn, tt)
    qkv, *wb_qkv = _matmul(h, w_qkv, off_b, tm, tn_qkv, emit)
    conv, conv_aux, *wb_conv = _gated_proj(h, w_conv, off_conv, d_conv, w["conv_w"], conv_state, t, tm_conv,
                                           tn_conv, emit, "conv_gate", n_chunks=2 if is_prompt else 1)
    if is_prompt:
        attn = _prompt_attention(qkv, w["sinks"], n, t, attn_w, kv_dim)
    else:
        q4 = qkv[:, :attn_w].reshape(n, t, n_kv, GROUP, HEAD_DIM).transpose(0, 2, 1, 3, 4)
        q4 = q4.reshape(n, n_kv, t * GROUP, HEAD_DIM)
        sink_lanes = jnp.tile(w["sinks"].reshape(n_kv // 2, 2, 1, GROUP), (1, 1, t, 1))
        sink_lanes = sink_lanes.reshape(n_kv // 2, 1, 2 * t * GROUP)
        o_pair = _sample_attention(q4, qkv, cache_k.reshape(n, WINDOW, kv_dim), cache_v.reshape(n, WINDOW, kv_dim),
                                   sink_lanes, n, t, attn_w, kv_dim, bn=min(8, n))
        o_pair = o_pair.reshape(n, n_kv // 2, 2, t * GROUP, 2, HEAD_DIM)
        o4 = jnp.stack([o_pair[:, :, e, :, e, :] for e in range(2)], axis=2)
        attn = o4.reshape(n, n_kv, t, GROUP, HEAD_DIM).transpose(0, 2, 1, 3, 4).reshape(m, attn_w)
    x1, *wb_out = _out_proj(attn, conv, w_oa, w_oc, out_rows, x2, mod3, 2, t, tm, tn_out, emit)

    h2 = _norm_mod(x1.reshape(n, t, d), mod3, w["g_ffn"], 3, 4, bn, tt)
    act, ffn_aux, *wb_act = _gated_proj(h2, w_act, off_act, d_ff, w["ffn_conv_w"], ffn_state, t, tm, tn_act, emit,
                                        "ffn_act", n_chunks=4 if is_prompt else 1)
    xo, *wb_down = _ffn_down(act, w_down, x1, mod3, 5, t, tm_down, tn_down, nk_down, emit)

    keep = min(WINDOW, t)
    kv_new = qkv.reshape(n, t, off_b)[:, t - keep:, attn_w:]
    k_new = kv_new[:, :, :kv_dim].reshape(n, keep, n_kv, HEAD_DIM)
    v_new = kv_new[:, :, kv_dim:].reshape(n, keep, n_kv, HEAD_DIM)
    if is_prompt:
        k_state, v_state = k_new, v_new
        conv_state_out = conv_aux[t // tm_conv - 1::t // tm_conv, SUBLANES - (CONV_K - 1):, :]
        ffn_state_out = ffn_aux[t // tm - 1::t // tm, SUBLANES - (CONV_K - 1):, :]
        return xo, (k_state, v_state, conv_state_out, ffn_state_out)
    k_state = jnp.concatenate([cache_k, k_new], axis=1)[:, t:]
    v_state = jnp.concatenate([cache_v, v_new], axis=1)[:, t:]
    wb = dict(w, w_qkv=wb_qkv[0], w_b=wb_conv[0], w_c=wb_conv[1], w_h=wb_conv[2], w_oa=wb_out[0], w_oc=wb_out[1],
              w_g=wb_act[0], w_u=wb_act[1], w_d=wb_down[0])
    return xo, (k_state, v_state, conv_aux, ffn_aux), wb


def kernel(x_prompt, x_sample, cache_k_win, cache_v_win, state_conv, state_ffn_conv, c_prompt, c_sample, w_mod, b_mod, g_mix, g_ffn, w_in, conv_w, sinks, w_out, w_gate_up, ffn_conv_w, w_down, g_final):
    depth = w_mod.shape[0]
    np_, tp, d = x_prompt.shape
    ns, ts, _ = x_sample.shape
    assert tp % BLOCK == 0 and cache_k_win.shape[2] == WINDOW

    n_c = np_ + ns
    n_c_pad = -(-n_c // SUBLANES) * SUBLANES
    c_all = jnp.concatenate([c_prompt, c_sample, jnp.zeros((n_c_pad - n_c, d), F32)], axis=0)

    xp, xs = x_prompt, x_sample
    outs_p, outs_s = [], []
    for l in range(depth):
        w = dict(g_mix=g_mix[l], g_ffn=g_ffn[l], conv_w=conv_w[l], sinks=sinks[l], ffn_conv_w=ffn_conv_w[l],
                 w_in=w_in[l], w_out=w_out[l], w_gate_up=w_gate_up[l], w_down=w_down[l])
        mod = _mod(c_all, w_mod[l], b_mod[l]).reshape(n_c_pad, N_MOD, d)
        xs2, st_s, wb = _layer(xs, mod[np_:n_c], state_conv[l], state_ffn_conv[l], cache_k_win[l], cache_v_win[l],
                               w, False)
        xp2, st_p = _layer(xp, mod[:np_], None, None, None, None, wb, True)
        xp, xs = xp2.reshape(xp.shape), xs2.reshape(xs.shape)
        outs_p.append(st_p)
        outs_s.append(st_s)

    y_prompt = _final_norm(xp.reshape(np_ * tp, d), g_final).reshape(np_, tp, d)
    y_sample = _final_norm(xs.reshape(ns * ts, d), g_final).reshape(ns, ts, d)
    stack = lambda outs, k: jnp.stack([o[k] for o in outs])
    return (y_prompt, y_sample,
            stack(outs_p, 0), stack(outs_p, 1), stack(outs_p, 2), stack(outs_p, 3),
            stack(outs_s, 0), stack(outs_s, 1), stack(outs_s, 2), stack(outs_s, 3))
```

```python
import functools

import jax
import jax.numpy as jnp
from jax import lax
from jax.experimental import pallas as pl
from jax.experimental.pallas import tpu as pltpu

F32 = jnp.float32
BF16 = jnp.bfloat16

HEAD_DIM = 64
SCALE = HEAD_DIM ** -0.5
GROUP = 8
WINDOW = 128
BLOCK = 128
CONV_K = 3
N_MOD = 6
EPS = 1e-5
SUBLANES = 8
MIB = 1024 * 1024
VMEM_CAP = 56 * MIB


def _params(semantics, vmem_bytes):
    return pltpu.CompilerParams(dimension_semantics=semantics,
                                vmem_limit_bytes=int(min(max(vmem_bytes, 16 * MIB), VMEM_CAP)))


def _dot(a, b):
    return jnp.dot(a, b, preferred_element_type=F32)


def _dot_nt(a, b):
    return lax.dot_general(a, b, (((1,), (1,)), ((), ())), preferred_element_type=F32)


def _weight_tile(w_ref, wout_ref):
    if wout_ref is None:
        return w_ref[...]
    wout_ref[...] = w_ref[...].astype(BF16)
    return wout_ref[...]


def _row_spec(shape, index_map, n_row_tiles):
    if n_row_tiles == 1:
        return pl.BlockSpec(shape, index_map, pipeline_mode=pl.Buffered(1))
    return pl.BlockSpec(shape, index_map)


def _nbytes(shape, dtype):
    n = jnp.dtype(dtype).itemsize
    for s in shape:
        n *= s
    return n


def _mod_body(c_ref, *refs):
    w_refs, b_ref, o_ref = refs[:N_MOD], refs[N_MOD], refs[N_MOD + 1]
    c = c_ref[...]
    a = (c * jax.nn.sigmoid(c)).astype(BF16)
    for k in range(N_MOD):
        o_ref[:, k, :] = _dot(a, w_refs[k][...].astype(BF16)) + b_ref[k:k + 1, :]


def _mod(c_all, w_mod, b_mod, tn=128):
    nc, d = c_all.shape
    nj = d // tn
    w_specs = [pl.BlockSpec((d, tn), functools.partial(lambda j, k: (0, k * nj + j), k=k)) for k in range(N_MOD)]
    vmem = 2 * (nc * d * 4 + N_MOD * d * tn * 4 + nc * SUBLANES * tn * 4) + N_MOD * d * tn * 2 + nc * d * 4
    return pl.pallas_call(
        _mod_body,
        grid=(nj,),
        in_specs=[pl.BlockSpec((nc, d), lambda j: (0, 0))] + w_specs + [pl.BlockSpec((N_MOD, tn), lambda j: (0, j))],
        out_specs=pl.BlockSpec((nc, N_MOD, tn), lambda j: (0, 0, j)),
        out_shape=jax.ShapeDtypeStruct((nc, N_MOD, d), F32),
        compiler_params=_params(("parallel",), vmem + 4 * MIB),
        name="mod",
    )(c_all, *([w_mod] * N_MOD), b_mod.reshape(N_MOD, d))


def _rms(x):
    return x * lax.rsqrt(jnp.mean(x * x, axis=-1, keepdims=True) + EPS)


def _norm_mod_body(x_ref, mod_ref, g_ref, o_ref, *, shift_idx, scale_idx):
    x = x_ref[...]
    y = _rms(x) * g_ref[...]
    h = y * (1.0 + mod_ref[:, scale_idx:scale_idx + 1, :]) + mod_ref[:, shift_idx:shift_idx + 1, :]
    o_ref[...] = h.reshape(o_ref.shape).astype(o_ref.dtype)


def _norm_mod(x3, mod3, g, shift_idx, scale_idx, bn, tt):
    n, t, d = x3.shape
    nt = t // tt
    vmem = 2 * (bn * tt * d * 4 + bn * tt * d * 2 + bn * SUBLANES * d * 4) + 3 * bn * tt * d * 4
    return pl.pallas_call(
        functools.partial(_norm_mod_body, shift_idx=shift_idx, scale_idx=scale_idx),
        grid=(n // bn, nt),
        in_specs=[pl.BlockSpec((bn, tt, d), lambda i, j: (i, j, 0)),
                  pl.BlockSpec((bn, N_MOD, d), lambda i, j: (i, 0, 0)),
                  pl.BlockSpec((1, d), lambda i, j: (0, 0))],
        out_specs=pl.BlockSpec((bn * tt, d), lambda i, j: (i * nt + j, 0)),
        out_shape=jax.ShapeDtypeStruct((n * t, d), BF16),
        compiler_params=_params(("parallel", "parallel"), vmem),
        name="norm_mod",
    )(x3, mod3, g.reshape(1, d))


def _final_norm_body(x_ref, g_ref, o_ref):
    o_ref[...] = _rms(x_ref[...]) * g_ref[...]


def _final_norm(x2, g, tm=512):
    m, d = x2.shape
    tm = min(tm, m)
    return pl.pallas_call(
        _final_norm_body,
        grid=(m // tm,),
        in_specs=[pl.BlockSpec((tm, d), lambda i: (i, 0)),
                  pl.BlockSpec((1, d), lambda i: (0, 0))],
        out_specs=pl.BlockSpec((tm, d), lambda i: (i, 0)),
        out_shape=jax.ShapeDtypeStruct((m, d), F32),
        compiler_params=_params(("parallel",), 6 * tm * d * 4),
        name="final_norm",
    )(x2, g.reshape(1, d))


def _matmul_body(a_ref, w_ref, o_ref, wout_ref=None):
    o_ref[...] = _dot(a_ref[...], _weight_tile(w_ref, wout_ref)).astype(o_ref.dtype)


def _matmul(a, w, n_cols, tm, tn, emit_w):
    m, k = a.shape
    n_i = m // tm
    assert n_i == 1 or not emit_w
    out_specs = [pl.BlockSpec((tm, tn), lambda i, j: (i, j))]
    out_shape = [jax.ShapeDtypeStruct((m, n_cols), F32)]
    if emit_w:
        out_specs.append(pl.BlockSpec((k, tn), lambda i, j: (0, j)))
        out_shape.append(jax.ShapeDtypeStruct((k, n_cols), BF16))
    vmem = (min(n_i, 2) * tm * k * 2 + 2 * _nbytes((k, tn), w.dtype) + (2 * k * tn * 2 if emit_w else 0)
            + 3 * tm * tn * 4)
    return pl.pallas_call(
        _matmul_body,
        grid=(n_i, n_cols // tn),
        in_specs=[_row_spec((tm, k), lambda i, j: (i, 0), n_i),
                  pl.BlockSpec((k, tn), lambda i, j: (0, j))],
        out_specs=out_specs, out_shape=out_shape,
        compiler_params=_params(("arbitrary", "arbitrary"), vmem + 4 * MIB),
        name="qkv",
    )(a, w)


def _silu(x):
    return x * jax.nn.sigmoid(x)


def _gated_body(*refs, n_w, emit_w, is_prompt, tiles_per_seq, n_chunks):
    h_ref = refs[0]
    w_refs = refs[1:1 + n_w]
    cw_ref = refs[1 + n_w]
    pos = 2 + n_w
    st_ref = None
    if not is_prompt:
        st_ref = refs[pos]
        pos += 1
    o_ref, aux_ref = refs[pos], refs[pos + 1]
    pos += 2
    wout_refs = refs[pos:pos + n_w] if emit_w else (None,) * n_w
    carry_ref = refs[-1] if is_prompt else None

    ws = [_weight_tile(r, o) for r, o in zip(w_refs, wout_refs)]
    cw = cw_ref[...]
    tm, tn = o_ref.shape
    rm = tm // n_chunks
    nb = rm // SUBLANES
    t = lax.broadcasted_iota(jnp.int32, (nb, SUBLANES, tn), 1)
    if is_prompt:
        i, j = pl.program_id(0), pl.program_id(1)

        @pl.when((i % tiles_per_seq) == 0)
        def _():
            carry_ref[j] = jnp.zeros((SUBLANES, tn), F32)

        tail = carry_ref[j]
    for c in range(n_chunks):
        h = h_ref[c * rm:(c + 1) * rm, :]
        if n_w == 3:
            v = _dot(h, ws[1]) * _dot(h, ws[2])
        else:
            v = _dot(h, ws[0])
        v3 = v.reshape(nb, SUBLANES, tn)
        if is_prompt:
            ext = jnp.concatenate([tail[None], v3], axis=0)
            e1 = pltpu.roll(ext, 1, axis=1)
            e2 = pltpu.roll(ext, 2, axis=1)
            v1 = jnp.where(t == 0, e1[:nb], e1[1:])
            v2 = jnp.where(t < 2, e2[:nb], e2[1:])
            tail = v3[nb - 1]
        else:
            p0 = st_ref[c * nb:(c + 1) * nb, 0:1, :]
            p1 = st_ref[c * nb:(c + 1) * nb, 1:2, :]
            v1 = jnp.where(t == 0, p1, pltpu.roll(v3, 1, axis=1))
            v2 = jnp.where(t == 0, p0, jnp.where(t == 1, p1, pltpu.roll(v3, 2, axis=1)))
            aux_ref[c * nb:(c + 1) * nb] = v3[:, SUBLANES - (CONV_K - 1):, :]
        y = (cw[0:1, :] * v2 + cw[1:2, :] * v1 + cw[2:3, :] * v3).reshape(rm, tn)
        if n_w == 3:
            out = _dot(h, ws[0]) * y
        else:
            out = _silu(y) * _dot(h, ws[1])
        o_ref[c * rm:(c + 1) * rm, :] = out.astype(o_ref.dtype)
    if is_prompt:
        carry_ref[j] = tail
        aux_ref[0] = tail


def _gated_proj(h, ws, col_offsets, n_cols, conv_w, state, seq_len, tm, tn, emit_w, name, n_chunks=1):
    m, k = h.shape
    n_w = len(ws)
    n_i, nj = m // tm, pl.cdiv(n_cols, tn)
    assert all(o % tn == 0 for o in col_offsets)
    offs = [o // tn for o in col_offsets]
    is_prompt = state is None
    spec = ((lambda shape, f: pl.BlockSpec(shape, f)) if is_prompt else
            (lambda shape, f: pl.BlockSpec(shape, lambda j, i: f(i, j))))
    w_specs = [spec((k, tn), functools.partial(lambda i, j, o: (0, o + j), o=o)) for o in offs]
    h_spec = _row_spec((tm, k), (lambda i, j: (i, 0)) if is_prompt else (lambda j, i: (i, 0)), n_i)
    in_specs = [h_spec] + w_specs + [spec((CONV_K, tn), lambda i, j: (0, j))]
    out_specs = [spec((tm, tn), lambda i, j: (i, j))]
    out_shape = [jax.ShapeDtypeStruct((m, n_cols), BF16)]
    args = (h,) + tuple(ws) + (conv_w,)
    if is_prompt:
        assert seq_len % tm == 0
        tiles_per_seq = seq_len // tm
        grid = (n_i, nj)
        out_specs.append(spec((1, SUBLANES, tn), lambda i, j: (i, 0, j)))
        out_shape.append(jax.ShapeDtypeStruct((n_i, SUBLANES, n_cols), F32))
        scratch = [pltpu.VMEM((nj, SUBLANES, tn), F32)]
    else:
        assert seq_len == SUBLANES and tm % SUBLANES == 0
        tiles_per_seq = None
        grid = (nj, n_i)
        nb = tm // SUBLANES
        in_specs.append(spec((nb, CONV_K - 1, tn), lambda i, j: (i, 0, j)))
        out_specs.append(spec((nb, CONV_K - 1, tn), lambda i, j: (i, 0, j)))
        out_shape.append(jax.ShapeDtypeStruct(state.shape, F32))
        scratch = []
        args += (state,)
    if emit_w:
        out_specs += [spec((k, tn), lambda i, j: (0, j))] * n_w
        out_shape += [jax.ShapeDtypeStruct((k, n_cols), BF16)] * n_w
    vmem = (min(n_i, 2) * tm * k * 2 + 2 * sum(_nbytes((k, tn), w.dtype) for w in ws)
            + (2 * n_w * k * tn * 2 if emit_w else 0) + 2 * tm * tn * 2 + 14 * tm * tn * 4)
    return pl.pallas_call(
        functools.partial(_gated_body, n_w=n_w, emit_w=emit_w, is_prompt=is_prompt, tiles_per_seq=tiles_per_seq,
                          n_chunks=n_chunks),
        grid=grid, in_specs=in_specs, out_specs=out_specs, out_shape=out_shape,
        scratch_shapes=scratch, compiler_params=_params(("arbitrary", "arbitrary"), vmem + 4 * MIB), name=name,
    )(*args)


def _sink_softmax(s, mask, sink, axis):
    s = jnp.where(mask, s, -jnp.inf)
    mx = jnp.maximum(jnp.max(s, axis=axis, keepdims=True), sink)
    p = jnp.exp(s - mx)
    denom = jnp.sum(p, axis=axis, keepdims=True) + jnp.exp(sink - mx)
    return p * (1.0 / denom)


def _dot_tn(a, b):
    return lax.dot_general(a, b, (((0,), (0,)), ((), ())), preferred_element_type=F32)


def _block_diag_pair(tile, e):
    lo = lax.broadcasted_iota(jnp.int32, tile.shape, 1) < HEAD_DIM
    rolled = pltpu.roll(tile, HEAD_DIM, axis=1)
    zero = jnp.zeros_like(tile)
    if e == 0:
        top, bot = jnp.where(lo, tile, zero), jnp.where(lo, zero, rolled)
    else:
        top, bot = jnp.where(lo, rolled, zero), jnp.where(lo, zero, tile)
    return jnp.concatenate([top, bot], axis=0).astype(BF16)


def _prompt_attn_body(sink_ref, q_ref, kp_ref, kc_ref, vp_ref, vc_ref, o_ref, *, n_kv):
    b = pl.program_id(1)
    kk = jnp.concatenate([kp_ref[...], kc_ref[...]], axis=0)
    vv = jnp.concatenate([vp_ref[...], vc_ref[...]], axis=0)
    c = lax.broadcasted_iota(jnp.int32, (2 * BLOCK, BLOCK), 0)
    r = lax.broadcasted_iota(jnp.int32, (2 * BLOCK, BLOCK), 1)
    mask = ((c >= r) & (c <= r + WINDOW) & ((c >= BLOCK) | (b > 0)))[None]
    first = lax.broadcasted_iota(jnp.int32, (2, 1, BLOCK), 0) == 0
    pairs_per_kv = GROUP // 2
    kbd, vbd = [], []
    for kh in range(n_kv):
        sl = slice((kh // 2) * 2 * HEAD_DIM, (kh // 2 + 1) * 2 * HEAD_DIM)
        kbd.append(_block_diag_pair(kk[:, sl], kh % 2))
        vbd.append(_block_diag_pair(vv[:, sl], kh % 2))
    n_pairs = n_kv * pairs_per_kv
    qp = [(q_ref[:, i * 2 * HEAD_DIM:(i + 1) * 2 * HEAD_DIM] * SCALE).astype(BF16) for i in range(n_pairs)]
    s = [_dot_nt(kbd[i // pairs_per_kv], qp[i]).reshape(2, 2 * BLOCK, BLOCK) for i in range(n_pairs)]
    p = []
    for i in range(n_pairs):
        sink = jnp.where(first, sink_ref[2 * i], sink_ref[2 * i + 1])
        p.append(_sink_softmax(s[i], mask, sink, axis=1).astype(BF16).reshape(4 * BLOCK, BLOCK))
    for i in range(n_pairs):
        o_ref[:, i * 2 * HEAD_DIM:(i + 1) * 2 * HEAD_DIM] = _dot_tn(p[i], vbd[i // pairs_per_kv]).astype(o_ref.dtype)


def _prompt_attention(qkv, sinks, n, t, attn_w, kv_dim):
    nb = t // BLOCK
    kcol = attn_w // kv_dim
    cur = lambda col: (lambda i, b: (i * nb + b, col))
    prev = lambda col: (lambda i, b: (i * nb + jnp.maximum(b - 1, 0), col))
    return pl.pallas_call(
        functools.partial(_prompt_attn_body, n_kv=kv_dim // HEAD_DIM),
        grid=(n, nb),
        in_specs=[pl.BlockSpec(memory_space=pltpu.MemorySpace.SMEM),
                  pl.BlockSpec((BLOCK, attn_w), cur(0)),
                  pl.BlockSpec((BLOCK, kv_dim), prev(kcol)),
                  pl.BlockSpec((BLOCK, kv_dim), cur(kcol)),
                  pl.BlockSpec((BLOCK, kv_dim), prev(kcol + 1)),
                  pl.BlockSpec((BLOCK, kv_dim), cur(kcol + 1))],
        out_specs=pl.BlockSpec((BLOCK, attn_w), cur(0)),
        out_shape=jax.ShapeDtypeStruct((n * t, attn_w), BF16),
        compiler_params=_params(("parallel", "parallel"), 24 * MIB),
        name="prompt_attention",
    )(sinks, qkv, qkv, qkv, qkv, qkv)


def _sample_attn_body(sink_ref, q_ref, kn_ref, vn_ref, ck_ref, cv_ref, o_ref, *, bn, n_kv, t):
    keys = 2 * WINDOW
    pair_w = 2 * GROUP * HEAD_DIM
    n_pairs = n_kv // 2
    c = lax.broadcasted_iota(jnp.int32, (keys, 2 * HEAD_DIM), 0)
    tok = lax.broadcasted_iota(jnp.int32, (keys, 2 * HEAD_DIM), 1) % t
    mask = ((c < WINDOW) & (c >= tok)) | ((c >= WINDOW) & (c - WINDOW <= tok))
    pad = jnp.zeros((keys - WINDOW - t, n_kv * HEAD_DIM), F32)
    lo = lax.broadcasted_iota(jnp.int32, (t, 2 * HEAD_DIM), 1) < HEAD_DIM
    zero = jnp.zeros((t, 2 * HEAD_DIM), F32)
    work = [(n, pr) for n in range(bn) for pr in range(n_pairs)]
    lanes = lambda a, pr: a[:, pr * 2 * HEAD_DIM:(pr + 1) * 2 * HEAD_DIM]
    kk = [jnp.concatenate([ck_ref[n], kn_ref[n * t:(n + 1) * t, :], pad], axis=0).astype(BF16) for n in range(bn)]
    vv = [jnp.concatenate([cv_ref[n], vn_ref[n * t:(n + 1) * t, :], pad], axis=0).astype(BF16) for n in range(bn)]

    def q_block_diag(n, pr):
        pieces = []
        for hh in range(2 * GROUP):
            e, v = hh // GROUP, hh // 2
            x = q_ref[n * t:(n + 1) * t, pr * pair_w + v * 2 * HEAD_DIM:pr * pair_w + (v + 1) * 2 * HEAD_DIM] * SCALE
            if hh % 2 != e:
                x = pltpu.roll(x, HEAD_DIM, axis=1)
            pieces.append(jnp.where(lo, x, zero) if e == 0 else jnp.where(lo, zero, x))
        return jnp.concatenate(pieces, axis=0).astype(BF16)

    s = [_dot_nt(lanes(kk[n], pr), q_block_diag(n, pr)) for n, pr in work]
    p = [_sink_softmax(s[i], mask, sink_ref[pr], axis=0).astype(BF16) for i, (n, pr) in enumerate(work)]
    for i, (n, pr) in enumerate(work):
        o = _dot_tn(p[i], lanes(vv[n], pr))
        slabs = []
        for v in range(GROUP):
            halves = []
            for hh in (2 * v, 2 * v + 1):
                x = o[hh * t:(hh + 1) * t, :]
                halves.append(pltpu.roll(x, HEAD_DIM, axis=1) if hh % 2 != hh // GROUP else x)
            slabs.append(jnp.where(lo, halves[0], halves[1]))
        o_ref[n * t:(n + 1) * t, pr * pair_w:(pr + 1) * pair_w] = jnp.concatenate(slabs, axis=1)


def _sample_attention(qkv, cache_k, cache_v, sink_lanes, n, t, attn_w, kv_dim, bn=8):
    n_kv = kv_dim // HEAD_DIM
    assert n_kv % 2 == 0 and t == SUBLANES
    kcol = attn_w // kv_dim
    return pl.pallas_call(
        functools.partial(_sample_attn_body, bn=bn, n_kv=n_kv, t=t),
        grid=(n // bn,),
        in_specs=[pl.BlockSpec((n_kv // 2, 1, 2 * HEAD_DIM), lambda i: (0, 0, 0)),
                  pl.BlockSpec((bn * t, attn_w), lambda i: (i, 0)),
                  pl.BlockSpec((bn * t, kv_dim), lambda i: (i, kcol)),
                  pl.BlockSpec((bn * t, kv_dim), lambda i: (i, kcol + 1)),
                  pl.BlockSpec((bn, WINDOW, kv_dim), lambda i: (i, 0, 0)),
                  pl.BlockSpec((bn, WINDOW, kv_dim), lambda i: (i, 0, 0))],
        out_specs=pl.BlockSpec((bn * t, attn_w), lambda i: (i, 0)),
        out_shape=jax.ShapeDtypeStruct((n * t, attn_w), F32),
        compiler_params=_params(("parallel",), 24 * MIB),
        name="sample_attention",
    )(sink_lanes, qkv, qkv, qkv, cache_k, cache_v)


def _seq_blocks(seq_len, tm):
    if seq_len >= tm:
        assert seq_len % tm == 0
        tiles = seq_len // tm
        return 1, (lambda i: i // tiles)
    assert tm % seq_len == 0
    return tm // seq_len, (lambda i: i)


def _gated_residual(x, acc, mod_ref, gate_idx):
    tm, tn = acc.shape
    bn = mod_ref.shape[0]
    gate = mod_ref[:, gate_idx:gate_idx + 1, :]
    y = x.reshape(bn, tm // bn, tn) + gate * acc.reshape(bn, tm // bn, tn)
    return y.reshape(tm, tn)


def _out_proj_body(a_ref, c_ref, wa_ref, wc_ref, x_ref, mod_ref, o_ref, woa_ref=None, woc_ref=None, *, gate_idx):
    acc = (_dot(a_ref[...].astype(BF16), _weight_tile(wa_ref, woa_ref))
           + _dot(c_ref[...], _weight_tile(wc_ref, woc_ref)))
    o_ref[...] = _gated_residual(x_ref[...], acc, mod_ref, gate_idx)


def _out_proj(attn, conv, wa, wc, row_blocks, x, mod3, gate_idx, seq_len, tm, tn, emit_w):
    m, ka = attn.shape
    assert conv.shape[1] == ka
    d = x.shape[1]
    n_i = m // tm
    assert n_i == 1 or not emit_w
    bn, seq_of = _seq_blocks(seq_len, tm)
    ra, rc = row_blocks
    out_specs = [pl.BlockSpec((tm, tn), lambda i, j: (i, j))]
    out_shape = [jax.ShapeDtypeStruct((m, d), F32)]
    if emit_w:
        out_specs += [pl.BlockSpec((ka, tn), lambda i, j: (0, j))] * 2
        out_shape += [jax.ShapeDtypeStruct((ka, d), BF16)] * 2
    vmem = (min(n_i, 2) * (_nbytes((tm, ka), attn.dtype) + tm * ka * 2) + tm * ka * 2
            + 4 * _nbytes((ka, tn), wa.dtype) + (4 * ka * tn * 2 if emit_w else 0) + 7 * tm * tn * 4)
    return pl.pallas_call(
        functools.partial(_out_proj_body, gate_idx=gate_idx),
        grid=(n_i, d // tn),
        in_specs=[_row_spec((tm, ka), lambda i, j: (i, 0), n_i),
                  _row_spec((tm, ka), lambda i, j: (i, 0), n_i),
                  pl.BlockSpec((ka, tn), lambda i, j: (ra, j)),
                  pl.BlockSpec((ka, tn), lambda i, j: (rc, j)),
                  pl.BlockSpec((tm, tn), lambda i, j: (i, j)),
                  pl.BlockSpec((bn, N_MOD, tn), lambda i, j: (seq_of(i), 0, j))],
        out_specs=out_specs, out_shape=out_shape,
        compiler_params=_params(("arbitrary", "arbitrary"), vmem + 4 * MIB),
        name="out_proj",
    )(attn, conv, wa, wc, x, mod3)


def _ffn_down_body(a_ref, w_ref, x_ref, mod_ref, o_ref, wout_ref=None, *, gate_idx, nk):
    if nk == 1:
        acc = _dot(a_ref[...], _weight_tile(w_ref, wout_ref))
        o_ref[...] = _gated_residual(x_ref[...], acc, mod_ref, gate_idx)
    else:
        kk = pl.program_id(2)
        tk = w_ref.shape[0]
        a = a_ref[:, pl.ds(pl.multiple_of(kk * tk, 128), tk)]
        acc = _dot(a, _weight_tile(w_ref, wout_ref))

        @pl.when(kk == 0)
        def _():
            o_ref[...] = _gated_residual(x_ref[...], acc, mod_ref, gate_idx)

        @pl.when(kk > 0)
        def _():
            o_ref[...] = _gated_residual(o_ref[...], acc, mod_ref, gate_idx)


def _ffn_down(act, w_down, x, mod3, gate_idx, seq_len, tm, tn, nk, emit_w):
    m, k = act.shape
    d = x.shape[1]
    tk = k // nk
    assert tk * nk == k and (m == tm or not emit_w)
    bn, seq_of = _seq_blocks(seq_len, tm)
    out_specs = [pl.BlockSpec((tm, tn), lambda i, j, kk: (i, j))]
    out_shape = [jax.ShapeDtypeStruct((m, d), F32)]
    if emit_w:
        out_specs.append(pl.BlockSpec((tk, tn), lambda i, j, kk: (kk, j)))
        out_shape.append(jax.ShapeDtypeStruct((k, d), BF16))
    assert tk % 128 == 0
    n_i = m // tm
    vmem = (min(n_i, 2) * tm * k * 2 + 2 * _nbytes((tk, tn), w_down.dtype) + (2 * tk * tn * 2 if emit_w else 0)
            + 7 * tm * tn * 4)
    return pl.pallas_call(
        functools.partial(_ffn_down_body, gate_idx=gate_idx, nk=nk),
        grid=(n_i, d // tn, nk),
        in_specs=[_row_spec((tm, k), lambda i, j, kk: (i, 0), n_i),
                  pl.BlockSpec((tk, tn), lambda i, j, kk: (kk, j)),
                  pl.BlockSpec((tm, tn), lambda i, j, kk: (i, j)),
                  pl.BlockSpec((bn, N_MOD, tn), lambda i, j, kk: (seq_of(i), 0, j))],
        out_specs=out_specs, out_shape=out_shape,
        compiler_params=_params(("arbitrary", "arbitrary", "arbitrary"), vmem + 4 * MIB),
        name="ffn_down",
    )(act, w_down, x, mod3)


def _layer(x3, mod3, conv_state, ffn_state, cache_k, cache_v, w, is_prompt):
    n, t, d = x3.shape
    m = n * t
    attn_w = d // 2
    d_conv = d - attn_w
    n_q = attn_w // HEAD_DIM
    n_kv = max(1, n_q // GROUP)
    kv_dim = n_kv * HEAD_DIM
    off_b = attn_w + 2 * kv_dim
    x2 = x3.reshape(m, d)
    emit = not is_prompt

    if is_prompt:
        tm = min(1024, t)
        bn, tt = 1, min(512, t)
        tn_qkv, tn_conv, tn_out, tn_act, tn_down, nk_down = 512, 512, 512, 512, 512, 1
        w_conv, off_conv = (w["w_b"], w["w_c"], w["w_h"]), (0, 0, 0)
        w_act, off_act = (w["w_g"], w["w_u"]), (0, 0)
        w_oa, w_oc, out_rows = w["w_oa"], w["w_oc"], (0, 0)
        w_qkv, w_down = w["w_qkv"], w["w_d"]
        d_ff = w_down.shape[0]
    else:
        assert t == SUBLANES
        tm = min(1024, m)
        bn, tt = min(32, n), t
        tn_qkv, tn_conv, tn_out, tn_act, tn_down, nk_down = 256, 256, 256, 256, 256, 2
        d_ff = w["w_down"].shape[0]
        w_conv, off_conv = (w["w_in"],) * 3, (off_b, off_b + d_conv, off_b + 2 * d_conv)
        w_act, off_act = (w["w_gate_up"],) * 2, (0, d_ff)
        w_oa = w_oc = w["w_out"]
        out_rows = (0, 1)
        w_qkv, w_down = w["w_in"], w["w_down"]
    tm_conv = min(512, tm)
    tm_down = min(512, tm) if is_prompt else tm

    h = _norm_mod(x3, mod3, w["g_mix"], 0, 1, bn, tt)
    qkv, *wb_qkv = _matmul(h, w_qkv, off_b, tm, tn_qkv, emit)
    conv, conv_aux, *wb_conv = _gated_proj(h, w_conv, off_conv, d_conv, w["conv_w"], conv_state, t, tm_conv,
                                           tn_conv, emit, "conv_gate", n_chunks=2 if is_prompt else 1)
    if is_prompt:
        attn = _prompt_attention(qkv, w["sinks"], n, t, attn_w, kv_dim)
    else:
        sink_lanes = jnp.repeat(w["sinks"].reshape(n_kv // 2, 1, 2 * GROUP), t, axis=2)
        attn = _sample_attention(qkv, cache_k.reshape(n, WINDOW, kv_dim), cache_v.reshape(n, WINDOW, kv_dim),
                                 sink_lanes, n, t, attn_w, kv_dim, bn=min(8, n))
    x1, *wb_out = _out_proj(attn, conv, w_oa, w_oc, out_rows, x2, mod3, 2, t, tm, tn_out, emit)

    h2 = _norm_mod(x1.reshape(n, t, d), mod3, w["g_ffn"], 3, 4, bn, tt)
    act, ffn_aux, *wb_act = _gated_proj(h2, w_act, off_act, d_ff, w["ffn_conv_w"], ffn_state, t, tm, tn_act, emit,
                                        "ffn_act", n_chunks=4 if is_prompt else 1)
    xo, *wb_down = _ffn_down(act, w_down, x1, mod3, 5, t, tm_down, tn_down, nk_down, emit)

    keep = min(WINDOW, t)
    kv_new = qkv.reshape(n, t, off_b)[:, t - keep:, attn_w:]
    k_new = kv_new[:, :, :kv_dim].reshape(n, keep, n_kv, HEAD_DIM)
    v_new = kv_new[:, :, kv_dim:].reshape(n, keep, n_kv, HEAD_DIM)
    if is_prompt:
        k_state, v_state = k_new, v_new
        conv_state_out = conv_aux[t // tm_conv - 1::t // tm_conv, SUBLANES - (CONV_K - 1):, :]
        ffn_state_out = ffn_aux[t // tm - 1::t // tm, SUBLANES - (CONV_K - 1):, :]
        return xo, (k_state, v_state, conv_state_out, ffn_state_out)
    k_state = jnp.concatenate([cache_k, k_new], axis=1)[:, t:]
    v_state = jnp.concatenate([cache_v, v_new], axis=1)[:, t:]
    wb = dict(w, w_qkv=wb_qkv[0], w_b=wb_conv[0], w_c=wb_conv[1], w_h=wb_conv[2], w_oa=wb_out[0], w_oc=wb_out[1],
              w_g=wb_act[0], w_u=wb_act[1], w_d=wb_down[0])
    return xo, (k_state, v_state, conv_aux, ffn_aux), wb


def kernel(x_prompt, x_sample, cache_k_win, cache_v_win, state_conv, state_ffn_conv, c_prompt, c_sample, w_mod, b_mod, g_mix, g_ffn, w_in, conv_w, sinks, w_out, w_gate_up, ffn_conv_w, w_down, g_final):
    depth = w_mod.shape[0]
    np_, tp, d = x_prompt.shape
    ns, ts, _ = x_sample.shape
    assert tp % BLOCK == 0 and cache_k_win.shape[2] == WINDOW

    n_c = np_ + ns
    n_c_pad = -(-n_c // SUBLANES) * SUBLANES
    c_all = jnp.concatenate([c_prompt, c_sample, jnp.zeros((n_c_pad - n_c, d), F32)], axis=0)

    xp, xs = x_prompt, x_sample
    outs_p, outs_s = [], []
    for l in range(depth):
        w = dict(g_mix=g_mix[l], g_ffn=g_ffn[l], conv_w=conv_w[l], sinks=sinks[l], ffn_conv_w=ffn_conv_w[l],
                 w_in=w_in[l], w_out=w_out[l], w_gate_up=w_gate_up[l], w_down=w_down[l])
        mod = _mod(c_all, w_mod[l], b_mod[l])
        xs2, st_s, wb = _layer(xs, mod[np_:n_c], state_conv[l], state_ffn_conv[l], cache_k_win[l], cache_v_win[l],
                               w, False)
        xp2, st_p = _layer(xp, mod[:np_], None, None, None, None, wb, True)
        xp, xs = xp2.reshape(xp.shape), xs2.reshape(xs.shape)
        outs_p.append(st_p)
        outs_s.append(st_s)

    y_prompt = _final_norm(xp.reshape(np_ * tp, d), g_final).reshape(np_, tp, d)
    y_sample = _final_norm(xs.reshape(ns * ts, d), g_final).reshape(ns, ts, d)
    stack = lambda outs, k: jnp.stack([o[k] for o in outs])
    return (y_prompt, y_sample,
            stack(outs_p, 0), stack(outs_p, 1), stack(outs_p, 2), stack(outs_p, 3),
            stack(outs_s, 0), stack(outs_s, 1), stack(outs_s, 2), stack(outs_s, 3))
```

```python
import functools

import jax
import jax.numpy as jnp
from jax import lax
from jax.experimental import pallas as pl
from jax.experimental.pallas import tpu as pltpu

F32 = jnp.float32
BF16 = jnp.bfloat16

HEAD_DIM = 64
SCALE = HEAD_DIM ** -0.5
GROUP = 8
WINDOW = 128
BLOCK = 128
CONV_K = 3
N_MOD = 6
EPS = 1e-5
SUBLANES = 8
MIB = 1024 * 1024
VMEM_CAP = 56 * MIB


def _params(semantics, vmem_bytes):
    return pltpu.CompilerParams(dimension_semantics=semantics,
                                vmem_limit_bytes=int(min(max(vmem_bytes, 16 * MIB), VMEM_CAP)))


def _dot(a, b):
    return jnp.dot(a, b, preferred_element_type=F32)


def _dot_nt(a, b):
    return lax.dot_general(a, b, (((1,), (1,)), ((), ())), preferred_element_type=F32)


def _weight_ref(w_ref, wout_ref):
    if wout_ref is None:
        return w_ref
    wout_ref[...] = w_ref[...].astype(BF16)
    return wout_ref


def _row_spec(shape, index_map, n_row_tiles):
    if n_row_tiles == 1:
        return pl.BlockSpec(shape, index_map, pipeline_mode=pl.Buffered(1))
    return pl.BlockSpec(shape, index_map)


def _nbytes(shape, dtype):
    n = jnp.dtype(dtype).itemsize
    for s in shape:
        n *= s
    return n


def _mod_body(c_ref, *refs):
    w_refs, b_ref, o_ref = refs[:N_MOD], refs[N_MOD], refs[N_MOD + 1]
    c = c_ref[...]
    a = (c * jax.nn.sigmoid(c)).astype(BF16)
    for k in range(N_MOD):
        o_ref[:, k, :] = _dot(a, w_refs[k][...].astype(BF16)) + b_ref[k:k + 1, :]


def _mod(c_all, w_mod, b_mod, tn=128):
    nc, d = c_all.shape
    nj = d // tn
    w_specs = [pl.BlockSpec((d, tn), functools.partial(lambda j, k: (0, k * nj + j), k=k)) for k in range(N_MOD)]
    vmem = 2 * (nc * d * 4 + N_MOD * d * tn * 4 + nc * SUBLANES * tn * 4) + N_MOD * d * tn * 2 + nc * d * 4
    return pl.pallas_call(
        _mod_body,
        grid=(nj,),
        in_specs=[pl.BlockSpec((nc, d), lambda j: (0, 0))] + w_specs + [pl.BlockSpec((N_MOD, tn), lambda j: (0, j))],
        out_specs=pl.BlockSpec((nc, N_MOD, tn), lambda j: (0, 0, j)),
        out_shape=jax.ShapeDtypeStruct((nc, N_MOD, d), F32),
        compiler_params=_params(("parallel",), vmem + 4 * MIB),
        name="mod",
    )(c_all, *([w_mod] * N_MOD), b_mod.reshape(N_MOD, d))


def _rms(x):
    return x * lax.rsqrt(jnp.mean(x * x, axis=-1, keepdims=True) + EPS)


def _norm_mod_body(x_ref, mod_ref, g_ref, o_ref, *, shift_idx, scale_idx):
    x = x_ref[...]
    y = _rms(x) * g_ref[...]
    h = y * (1.0 + mod_ref[:, scale_idx:scale_idx + 1, :]) + mod_ref[:, shift_idx:shift_idx + 1, :]
    o_ref[...] = h.reshape(o_ref.shape).astype(o_ref.dtype)


def _norm_mod(x3, mod3, g, shift_idx, scale_idx, bn, tt):
    n, t, d = x3.shape
    mod, row0 = mod3
    assert row0 % bn == 0
    b0 = row0 // bn
    nt = t // tt
    vmem = 2 * (bn * tt * d * 4 + bn * tt * d * 2 + bn * SUBLANES * d * 4) + 3 * bn * tt * d * 4
    return pl.pallas_call(
        functools.partial(_norm_mod_body, shift_idx=shift_idx, scale_idx=scale_idx),
        grid=(n // bn, nt),
        in_specs=[pl.BlockSpec((bn, tt, d), lambda i, j: (i, j, 0)),
                  pl.BlockSpec((bn, N_MOD, d), lambda i, j: (b0 + i, 0, 0)),
                  pl.BlockSpec((1, d), lambda i, j: (0, 0))],
        out_specs=pl.BlockSpec((bn * tt, d), lambda i, j: (i * nt + j, 0)),
        out_shape=jax.ShapeDtypeStruct((n * t, d), BF16),
        compiler_params=_params(("parallel", "parallel"), vmem),
        name="norm_mod",
    )(x3, mod, g.reshape(1, d))


def _final_norm_body(x_ref, g_ref, o_ref):
    o_ref[...] = _rms(x_ref[...]) * g_ref[...]


def _final_norm(x2, g, tm=512):
    m, d = x2.shape
    tm = min(tm, m)
    return pl.pallas_call(
        _final_norm_body,
        grid=(m // tm,),
        in_specs=[pl.BlockSpec((tm, d), lambda i: (i, 0)),
                  pl.BlockSpec((1, d), lambda i: (0, 0))],
        out_specs=pl.BlockSpec((tm, d), lambda i: (i, 0)),
        out_shape=jax.ShapeDtypeStruct((m, d), F32),
        compiler_params=_params(("parallel",), 6 * tm * d * 4),
        name="final_norm",
    )(x2, g.reshape(1, d))


def _matmul_body(a_ref, w_ref, o_ref, wout_ref=None):
    o_ref[...] = _dot(a_ref[...], _weight_ref(w_ref, wout_ref)[...]).astype(o_ref.dtype)


def _matmul(a, w, n_cols, tm, tn, emit_w):
    m, k = a.shape
    n_i = m // tm
    assert n_i == 1 or not emit_w
    out_specs = [pl.BlockSpec((tm, tn), lambda i, j: (i, j))]
    out_shape = [jax.ShapeDtypeStruct((m, n_cols), F32)]
    if emit_w:
        out_specs.append(pl.BlockSpec((k, tn), lambda i, j: (0, j)))
        out_shape.append(jax.ShapeDtypeStruct((k, n_cols), BF16))
    vmem = (min(n_i, 2) * tm * k * 2 + 2 * _nbytes((k, tn), w.dtype) + (2 * k * tn * 2 if emit_w else 0)
            + 3 * tm * tn * 4)
    return pl.pallas_call(
        _matmul_body,
        grid=(n_i, n_cols // tn),
        in_specs=[_row_spec((tm, k), lambda i, j: (i, 0), n_i),
                  pl.BlockSpec((k, tn), lambda i, j: (0, j))],
        out_specs=out_specs, out_shape=out_shape,
        compiler_params=_params(("arbitrary", "arbitrary"), vmem + 4 * MIB),
        name="qkv",
    )(a, w)


def _silu(x):
    return x * jax.nn.sigmoid(x)


def _gated_body(*refs, n_w, emit_w, is_prompt, tiles_per_seq):
    h_ref = refs[0]
    w_refs = refs[1:1 + n_w]
    cw_ref = refs[1 + n_w]
    pos = 2 + n_w
    st_ref = None
    if not is_prompt:
        st_ref = refs[pos]
        pos += 1
    o_ref, aux_ref = refs[pos], refs[pos + 1]
    pos += 2
    wout_refs = refs[pos:pos + n_w] if emit_w else (None,) * n_w
    carry_ref = refs[-1] if is_prompt else None

    ws = [_weight_ref(r, o) for r, o in zip(w_refs, wout_refs)]
    cw = cw_ref[...]
    tm, tn = o_ref.shape
    nb = tm // SUBLANES
    t = lax.broadcasted_iota(jnp.int32, (nb, SUBLANES, tn), 1)
    if is_prompt:
        i, j = pl.program_id(0), pl.program_id(1)

        @pl.when((i % tiles_per_seq) == 0)
        def _():
            carry_ref[j] = jnp.zeros((SUBLANES, tn), F32)

        tail = carry_ref[j]
    h = h_ref[...]
    if n_w == 3:
        v = _dot(h, ws[1][...]) * _dot(h, ws[2][...])
    else:
        v = _dot(h, ws[0][...])
    v3 = v.reshape(nb, SUBLANES, tn)
    if is_prompt:
        ext = jnp.concatenate([tail[None], v3], axis=0)
        e1 = pltpu.roll(ext, 1, axis=1)
        e2 = pltpu.roll(ext, 2, axis=1)
        v1 = jnp.where(t == 0, e1[:nb], e1[1:])
        v2 = jnp.where(t < 2, e2[:nb], e2[1:])
        carry_ref[j] = v3[nb - 1]
        aux_ref[0] = v3[nb - 1]
    else:
        p0 = st_ref[:, 0:1, :]
        p1 = st_ref[:, 1:2, :]
        v1 = jnp.where(t == 0, p1, pltpu.roll(v3, 1, axis=1))
        v2 = jnp.where(t == 0, p0, jnp.where(t == 1, p1, pltpu.roll(v3, 2, axis=1)))
        aux_ref[...] = v3[:, SUBLANES - (CONV_K - 1):, :]
    y = (cw[0:1, :] * v2 + cw[1:2, :] * v1 + cw[2:3, :] * v3).reshape(tm, tn)
    if n_w == 3:
        out = _dot(h, ws[0][...]) * y
    else:
        out = _silu(y) * _dot(h, ws[1][...])
    o_ref[...] = out.astype(o_ref.dtype)


def _gated_proj(h, ws, col_offsets, n_cols, conv_w, state, seq_len, tm, tn, emit_w, name):
    m, k = h.shape
    n_w = len(ws)
    n_i, nj = m // tm, pl.cdiv(n_cols, tn)
    assert all(o % tn == 0 for o in col_offsets)
    offs = [o // tn for o in col_offsets]
    is_prompt = state is None
    spec = ((lambda shape, f: pl.BlockSpec(shape, f)) if is_prompt else
            (lambda shape, f: pl.BlockSpec(shape, lambda j, i: f(i, j))))
    w_specs = [spec((k, tn), functools.partial(lambda i, j, o: (0, o + j), o=o)) for o in offs]
    h_spec = _row_spec((tm, k), (lambda i, j: (i, 0)) if is_prompt else (lambda j, i: (i, 0)), n_i)
    in_specs = [h_spec] + w_specs + [spec((CONV_K, tn), lambda i, j: (0, j))]
    out_specs = [spec((tm, tn), lambda i, j: (i, j))]
    out_shape = [jax.ShapeDtypeStruct((m, n_cols), BF16)]
    args = (h,) + tuple(ws) + (conv_w,)
    if is_prompt:
        assert seq_len % tm == 0
        tiles_per_seq = seq_len // tm
        grid = (n_i, nj)
        out_specs.append(spec((1, SUBLANES, tn), lambda i, j: (i, 0, j)))
        out_shape.append(jax.ShapeDtypeStruct((n_i, SUBLANES, n_cols), F32))
        scratch = [pltpu.VMEM((nj, SUBLANES, tn), F32)]
    else:
        assert seq_len == SUBLANES and tm % SUBLANES == 0
        tiles_per_seq = None
        grid = (nj, n_i)
        nb = tm // SUBLANES
        in_specs.append(spec((nb, CONV_K - 1, tn), lambda i, j: (i, 0, j)))
        out_specs.append(spec((nb, CONV_K - 1, tn), lambda i, j: (i, 0, j)))
        out_shape.append(jax.ShapeDtypeStruct(state.shape, F32))
        scratch = []
        args += (state,)
    if emit_w:
        out_specs += [spec((k, tn), lambda i, j: (0, j))] * n_w
        out_shape += [jax.ShapeDtypeStruct((k, n_cols), BF16)] * n_w
    vmem = (min(n_i, 2) * tm * k * 2 + 2 * sum(_nbytes((k, tn), w.dtype) for w in ws)
            + (2 * n_w * k * tn * 2 if emit_w else 0) + 2 * tm * tn * 2 + 14 * tm * tn * 4)
    return pl.pallas_call(
        functools.partial(_gated_body, n_w=n_w, emit_w=emit_w, is_prompt=is_prompt, tiles_per_seq=tiles_per_seq),
        grid=grid, in_specs=in_specs, out_specs=out_specs, out_shape=out_shape,
        scratch_shapes=scratch, compiler_params=_params(("arbitrary", "arbitrary"), vmem + 4 * MIB), name=name,
    )(*args)


def _sink_softmax(s, mask, sink, axis):
    s = jnp.where(mask, s, -jnp.inf)
    mx = jnp.maximum(jnp.max(s, axis=axis, keepdims=True), sink)
    p = jnp.exp(s - mx)
    denom = jnp.sum(p, axis=axis, keepdims=True) + jnp.exp(sink - mx)
    return p * (1.0 / denom)


def _dot_tn(a, b):
    return lax.dot_general(a, b, (((0,), (0,)), ((), ())), preferred_element_type=F32)


def _block_diag_pair(tile, e):
    lo = lax.broadcasted_iota(jnp.int32, tile.shape, 1) < HEAD_DIM
    rolled = pltpu.roll(tile, HEAD_DIM, axis=1)
    zero = jnp.zeros_like(tile)
    if e == 0:
        top, bot = jnp.where(lo, tile, zero), jnp.where(lo, zero, rolled)
    else:
        top, bot = jnp.where(lo, rolled, zero), jnp.where(lo, zero, tile)
    return jnp.concatenate([top, bot], axis=0).astype(BF16)


def _prompt_attn_body(sink_ref, q_ref, kp_ref, kc_ref, vp_ref, vc_ref, o_ref, *, n_kv):
    b = pl.program_id(1)
    kk = jnp.concatenate([kp_ref[...], kc_ref[...]], axis=0)
    vv = jnp.concatenate([vp_ref[...], vc_ref[...]], axis=0)
    c = lax.broadcasted_iota(jnp.int32, (2 * BLOCK, BLOCK), 0)
    r = lax.broadcasted_iota(jnp.int32, (2 * BLOCK, BLOCK), 1)
    mask = ((c >= r) & (c <= r + WINDOW) & ((c >= BLOCK) | (b > 0)))[None]
    first = lax.broadcasted_iota(jnp.int32, (2, 1, BLOCK), 0) == 0
    pairs_per_kv = GROUP // 2
    kbd, vbd = [], []
    for kh in range(n_kv):
        sl = slice((kh // 2) * 2 * HEAD_DIM, (kh // 2 + 1) * 2 * HEAD_DIM)
        kbd.append(_block_diag_pair(kk[:, sl], kh % 2))
        vbd.append(_block_diag_pair(vv[:, sl], kh % 2))
    n_pairs = n_kv * pairs_per_kv
    qp = [(q_ref[:, i * 2 * HEAD_DIM:(i + 1) * 2 * HEAD_DIM] * SCALE).astype(BF16) for i in range(n_pairs)]
    s = [_dot_nt(kbd[i // pairs_per_kv], qp[i]).reshape(2, 2 * BLOCK, BLOCK) for i in range(n_pairs)]
    p = []
    for i in range(n_pairs):
        sink = jnp.where(first, sink_ref[2 * i], sink_ref[2 * i + 1])
        p.append(_sink_softmax(s[i], mask, sink, axis=1).astype(BF16).reshape(4 * BLOCK, BLOCK))
    for i in range(n_pairs):
        o_ref[:, i * 2 * HEAD_DIM:(i + 1) * 2 * HEAD_DIM] = _dot_tn(p[i], vbd[i // pairs_per_kv]).astype(o_ref.dtype)


def _prompt_attention(qkv, sinks, n, t, attn_w, kv_dim):
    nb = t // BLOCK
    kcol = attn_w // kv_dim
    cur = lambda col: (lambda i, b: (i * nb + b, col))
    prev = lambda col: (lambda i, b: (i * nb + jnp.maximum(b - 1, 0), col))
    return pl.pallas_call(
        functools.partial(_prompt_attn_body, n_kv=kv_dim // HEAD_DIM),
        grid=(n, nb),
        in_specs=[pl.BlockSpec(memory_space=pltpu.MemorySpace.SMEM),
                  pl.BlockSpec((BLOCK, attn_w), cur(0)),
                  pl.BlockSpec((BLOCK, kv_dim), prev(kcol)),
                  pl.BlockSpec((BLOCK, kv_dim), cur(kcol)),
                  pl.BlockSpec((BLOCK, kv_dim), prev(kcol + 1)),
                  pl.BlockSpec((BLOCK, kv_dim), cur(kcol + 1))],
        out_specs=pl.BlockSpec((BLOCK, attn_w), cur(0)),
        out_shape=jax.ShapeDtypeStruct((n * t, attn_w), BF16),
        compiler_params=_params(("parallel", "parallel"), 24 * MIB),
        name="prompt_attention",
    )(sinks, qkv, qkv, qkv, qkv, qkv)


def _sample_attn_body(sink_ref, q_ref, kn_ref, vn_ref, ck_ref, cv_ref, o_ref, *, bn, n_kv, t):
    keys = 2 * WINDOW
    pair_w = 2 * GROUP * HEAD_DIM
    n_pairs = n_kv // 2
    c = lax.broadcasted_iota(jnp.int32, (keys, 2 * HEAD_DIM), 0)
    tok = lax.broadcasted_iota(jnp.int32, (keys, 2 * HEAD_DIM), 1) % t
    mask = ((c < WINDOW) & (c >= tok)) | ((c >= WINDOW) & (c - WINDOW <= tok))
    pad = jnp.zeros((keys - WINDOW - t, n_kv * HEAD_DIM), F32)
    lo = lax.broadcasted_iota(jnp.int32, (t, 2 * HEAD_DIM), 1) < HEAD_DIM
    zero = jnp.zeros((t, 2 * HEAD_DIM), F32)
    work = [(n, pr) for n in range(bn) for pr in range(n_pairs)]
    lanes = lambda a, pr: a[:, pr * 2 * HEAD_DIM:(pr + 1) * 2 * HEAD_DIM]
    kk = [jnp.concatenate([ck_ref[n], kn_ref[n * t:(n + 1) * t, :], pad], axis=0).astype(BF16) for n in range(bn)]
    vv = [jnp.concatenate([cv_ref[n], vn_ref[n * t:(n + 1) * t, :], pad], axis=0).astype(BF16) for n in range(bn)]

    def q_block_diag(n, pr):
        pieces = []
        for hh in range(2 * GROUP):
            e, v = hh // GROUP, hh // 2
            x = q_ref[n * t:(n + 1) * t, pr * pair_w + v * 2 * HEAD_DIM:pr * pair_w + (v + 1) * 2 * HEAD_DIM] * SCALE
            if hh % 2 != e:
                x = pltpu.roll(x, HEAD_DIM, axis=1)
            pieces.append(jnp.where(lo, x, zero) if e == 0 else jnp.where(lo, zero, x))
        return jnp.concatenate(pieces, axis=0).astype(BF16)

    s = [_dot_nt(lanes(kk[n], pr), q_block_diag(n, pr)) for n, pr in work]
    p = [_sink_softmax(s[i], mask, sink_ref[pr], axis=0).astype(BF16) for i, (n, pr) in enumerate(work)]
    for i, (n, pr) in enumerate(work):
        o = _dot_tn(p[i], lanes(vv[n], pr))
        slabs = []
        for v in range(GROUP):
            halves = []
            for hh in (2 * v, 2 * v + 1):
                x = o[hh * t:(hh + 1) * t, :]
                halves.append(pltpu.roll(x, HEAD_DIM, axis=1) if hh % 2 != hh // GROUP else x)
            slabs.append(jnp.where(lo, halves[0], halves[1]))
        o_ref[n * t:(n + 1) * t, pr * pair_w:(pr + 1) * pair_w] = jnp.concatenate(slabs, axis=1)


def _sample_attention(qkv, cache_k, cache_v, sink_lanes, n, t, attn_w, kv_dim, bn=8):
    n_kv = kv_dim // HEAD_DIM
    assert n_kv % 2 == 0 and t == SUBLANES
    kcol = attn_w // kv_dim
    return pl.pallas_call(
        functools.partial(_sample_attn_body, bn=bn, n_kv=n_kv, t=t),
        grid=(n // bn,),
        in_specs=[pl.BlockSpec((n_kv // 2, 1, 2 * HEAD_DIM), lambda i: (0, 0, 0)),
                  pl.BlockSpec((bn * t, attn_w), lambda i: (i, 0)),
                  pl.BlockSpec((bn * t, kv_dim), lambda i: (i, kcol)),
                  pl.BlockSpec((bn * t, kv_dim), lambda i: (i, kcol + 1)),
                  pl.BlockSpec((bn, WINDOW, kv_dim), lambda i: (i, 0, 0)),
                  pl.BlockSpec((bn, WINDOW, kv_dim), lambda i: (i, 0, 0))],
        out_specs=pl.BlockSpec((bn * t, attn_w), lambda i: (i, 0)),
        out_shape=jax.ShapeDtypeStruct((n * t, attn_w), F32),
        compiler_params=_params(("parallel",), 24 * MIB),
        name="sample_attention",
    )(sink_lanes, qkv, qkv, qkv, cache_k, cache_v)


def _seq_blocks(seq_len, tm, mod3):
    mod, row0 = mod3
    if seq_len >= tm:
        assert seq_len % tm == 0
        tiles = seq_len // tm
        return mod, 1, (lambda i: row0 + i // tiles)
    assert tm % seq_len == 0 and row0 % (tm // seq_len) == 0
    b0 = row0 // (tm // seq_len)
    return mod, tm // seq_len, (lambda i: b0 + i)


def _gated_residual(x, acc, mod_ref, gate_idx):
    tm, tn = acc.shape
    bn = mod_ref.shape[0]
    gate = mod_ref[:, gate_idx:gate_idx + 1, :]
    y = x.reshape(bn, tm // bn, tn) + gate * acc.reshape(bn, tm // bn, tn)
    return y.reshape(tm, tn)


def _out_proj_body(a_ref, c_ref, wa_ref, wc_ref, x_ref, mod_ref, o_ref, woa_ref=None, woc_ref=None, *, gate_idx):
    acc = (_dot(a_ref[...].astype(BF16), _weight_ref(wa_ref, woa_ref)[...])
           + _dot(c_ref[...], _weight_ref(wc_ref, woc_ref)[...]))
    o_ref[...] = _gated_residual(x_ref[...], acc, mod_ref, gate_idx)


def _out_proj(attn, conv, wa, wc, row_blocks, x, mod3, gate_idx, seq_len, tm, tn, emit_w):
    m, ka = attn.shape
    assert conv.shape[1] == ka
    d = x.shape[1]
    n_i = m // tm
    assert n_i == 1 or not emit_w
    mod, bn, seq_of = _seq_blocks(seq_len, tm, mod3)
    ra, rc = row_blocks
    out_specs = [pl.BlockSpec((tm, tn), lambda i, j: (i, j))]
    out_shape = [jax.ShapeDtypeStruct((m, d), F32)]
    if emit_w:
        out_specs += [pl.BlockSpec((ka, tn), lambda i, j: (0, j))] * 2
        out_shape += [jax.ShapeDtypeStruct((ka, d), BF16)] * 2
    vmem = (min(n_i, 2) * (_nbytes((tm, ka), attn.dtype) + tm * ka * 2) + tm * ka * 2
            + 4 * _nbytes((ka, tn), wa.dtype) + (4 * ka * tn * 2 if emit_w else 0) + 7 * tm * tn * 4)
    return pl.pallas_call(
        functools.partial(_out_proj_body, gate_idx=gate_idx),
        grid=(n_i, d // tn),
        in_specs=[_row_spec((tm, ka), lambda i, j: (i, 0), n_i),
                  _row_spec((tm, ka), lambda i, j: (i, 0), n_i),
                  pl.BlockSpec((ka, tn), lambda i, j: (ra, j)),
                  pl.BlockSpec((ka, tn), lambda i, j: (rc, j)),
                  pl.BlockSpec((tm, tn), lambda i, j: (i, j)),
                  pl.BlockSpec((bn, N_MOD, tn), lambda i, j: (seq_of(i), 0, j))],
        out_specs=out_specs, out_shape=out_shape,
        compiler_params=_params(("arbitrary", "arbitrary"), vmem + 4 * MIB),
        name="out_proj",
    )(attn, conv, wa, wc, x, mod)


def _ffn_down_body(a_ref, w_ref, x_ref, mod_ref, o_ref, wout_ref=None, *, gate_idx, nk):
    if nk == 1:
        acc = _dot(a_ref[...], _weight_ref(w_ref, wout_ref)[...])
        o_ref[...] = _gated_residual(x_ref[...], acc, mod_ref, gate_idx)
    else:
        kk = pl.program_id(2)
        tk = w_ref.shape[0]
        a = a_ref[:, pl.ds(pl.multiple_of(kk * tk, 128), tk)]
        acc = _dot(a, _weight_ref(w_ref, wout_ref)[...])

        @pl.when(kk == 0)
        def _():
            o_ref[...] = _gated_residual(x_ref[...], acc, mod_ref, gate_idx)

        @pl.when(kk > 0)
        def _():
            o_ref[...] = _gated_residual(o_ref[...], acc, mod_ref, gate_idx)


def _ffn_down(act, w_down, x, mod3, gate_idx, seq_len, tm, tn, nk, emit_w):
    m, k = act.shape
    d = x.shape[1]
    tk = k // nk
    assert tk * nk == k and (m == tm or not emit_w)
    mod, bn, seq_of = _seq_blocks(seq_len, tm, mod3)
    out_specs = [pl.BlockSpec((tm, tn), lambda i, j, kk: (i, j))]
    out_shape = [jax.ShapeDtypeStruct((m, d), F32)]
    if emit_w:
        out_specs.append(pl.BlockSpec((tk, tn), lambda i, j, kk: (kk, j)))
        out_shape.append(jax.ShapeDtypeStruct((k, d), BF16))
    assert tk % 128 == 0
    n_i = m // tm
    vmem = (min(n_i, 2) * tm * k * 2 + 2 * _nbytes((tk, tn), w_down.dtype) + (2 * tk * tn * 2 if emit_w else 0)
            + 7 * tm * tn * 4)
    return pl.pallas_call(
        functools.partial(_ffn_down_body, gate_idx=gate_idx, nk=nk),
        grid=(n_i, d // tn, nk),
        in_specs=[_row_spec((tm, k), lambda i, j, kk: (i, 0), n_i),
                  pl.BlockSpec((tk, tn), lambda i, j, kk: (kk, j)),
                  pl.BlockSpec((tm, tn), lambda i, j, kk: (i, j)),
                  pl.BlockSpec((bn, N_MOD, tn), lambda i, j, kk: (seq_of(i), 0, j))],
        out_specs=out_specs, out_shape=out_shape,
        compiler_params=_params(("arbitrary", "arbitrary", "arbitrary"), vmem + 4 * MIB),
        name="ffn_down",
    )(act, w_down, x, mod)


def _layer(x3, mod3, conv_state, ffn_state, cache_k, cache_v, w, is_prompt):
    n, t, d = x3.shape
    m = n * t
    attn_w = d // 2
    d_conv = d - attn_w
    n_q = attn_w // HEAD_DIM
    n_kv = max(1, n_q // GROUP)
    kv_dim = n_kv * HEAD_DIM
    off_b = attn_w + 2 * kv_dim
    x2 = x3.reshape(m, d)
    emit = not is_prompt

    if is_prompt:
        tm = min(1024, t)
        bn, tt = 1, min(512, t)
        tn_qkv, tn_conv, tn_out, tn_act, tn_down, nk_down = 512, 512, 512, 512, 512, 1
        w_conv, off_conv = (w["w_b"], w["w_c"], w["w_h"]), (0, 0, 0)
        w_act, off_act = (w["w_g"], w["w_u"]), (0, 0)
        w_oa, w_oc, out_rows = w["w_oa"], w["w_oc"], (0, 0)
        w_qkv, w_down = w["w_qkv"], w["w_d"]
        d_ff = w_down.shape[0]
    else:
        assert t == SUBLANES
        tm = min(1024, m)
        bn, tt = min(32, n), t
        tn_qkv, tn_conv, tn_out, tn_act, tn_down, nk_down = 256, 256, 256, 256, 256, 2
        d_ff = w["w_down"].shape[0]
        w_conv, off_conv = (w["w_in"],) * 3, (off_b, off_b + d_conv, off_b + 2 * d_conv)
        w_act, off_act = (w["w_gate_up"],) * 2, (0, d_ff)
        w_oa = w_oc = w["w_out"]
        out_rows = (0, 1)
        w_qkv, w_down = w["w_in"], w["w_down"]
    tm_conv = min(512, tm)
    tm_down = min(512, tm) if is_prompt else tm

    h = _norm_mod(x3, mod3, w["g_mix"], 0, 1, bn, tt)
    qkv, *wb_qkv = _matmul(h, w_qkv, off_b, tm, tn_qkv, emit)
    conv, conv_aux, *wb_conv = _gated_proj(h, w_conv, off_conv, d_conv, w["conv_w"], conv_state, t, tm_conv,
                                           tn_conv, emit, "conv_gate")
    if is_prompt:
        attn = _prompt_attention(qkv, w["sinks"], n, t, attn_w, kv_dim)
    else:
        sink_lanes = jnp.repeat(w["sinks"].reshape(n_kv // 2, 1, 2 * GROUP), t, axis=2)
        attn = _sample_attention(qkv, cache_k.reshape(n, WINDOW, kv_dim), cache_v.reshape(n, WINDOW, kv_dim),
                                 sink_lanes, n, t, attn_w, kv_dim, bn=min(8, n))
    x1, *wb_out = _out_proj(attn, conv, w_oa, w_oc, out_rows, x2, mod3, 2, t, tm, tn_out, emit)

    h2 = _norm_mod(x1.reshape(n, t, d), mod3, w["g_ffn"], 3, 4, bn, tt)
    act, ffn_aux, *wb_act = _gated_proj(h2, w_act, off_act, d_ff, w["ffn_conv_w"], ffn_state, t, tm, tn_act, emit,
                                        "ffn_act")
    xo, *wb_down = _ffn_down(act, w_down, x1, mod3, 5, t, tm_down, tn_down, nk_down, emit)

    keep = min(WINDOW, t)
    kv_new = qkv.reshape(n, t, off_b)[:, t - keep:, attn_w:]
    k_new = kv_new[:, :, :kv_dim].reshape(n, keep, n_kv, HEAD_DIM)
    v_new = kv_new[:, :, kv_dim:].reshape(n, keep, n_kv, HEAD_DIM)
    if is_prompt:
        k_state, v_state = k_new, v_new
        conv_state_out = conv_aux[t // tm_conv - 1::t // tm_conv, SUBLANES - (CONV_K - 1):, :]
        ffn_state_out = ffn_aux[t // tm - 1::t // tm, SUBLANES - (CONV_K - 1):, :]
        return xo, (k_state, v_state, conv_state_out, ffn_state_out)
    k_state = jnp.concatenate([cache_k, k_new], axis=1)[:, t:]
    v_state = jnp.concatenate([cache_v, v_new], axis=1)[:, t:]
    wb = dict(w, w_qkv=wb_qkv[0], w_b=wb_conv[0], w_c=wb_conv[1], w_h=wb_conv[2], w_oa=wb_out[0], w_oc=wb_out[1],
              w_g=wb_act[0], w_u=wb_act[1], w_d=wb_down[0])
    return xo, (k_state, v_state, conv_aux, ffn_aux), wb


def kernel(x_prompt, x_sample, cache_k_win, cache_v_win, state_conv, state_ffn_conv, c_prompt, c_sample, w_mod, b_mod, g_mix, g_ffn, w_in, conv_w, sinks, w_out, w_gate_up, ffn_conv_w, w_down, g_final):
    depth = w_mod.shape[0]
    np_, tp, d = x_prompt.shape
    ns, ts, _ = x_sample.shape
    assert tp % BLOCK == 0 and cache_k_win.shape[2] == WINDOW

    n_c = np_ + ns
    n_c_pad = -(-n_c // SUBLANES) * SUBLANES
    c_all = jnp.concatenate([c_sample, c_prompt, jnp.zeros((n_c_pad - n_c, d), F32)], axis=0)

    xp, xs = x_prompt, x_sample
    outs_p, outs_s = [], []
    for l in range(depth):
        w = dict(g_mix=g_mix[l], g_ffn=g_ffn[l], conv_w=conv_w[l], sinks=sinks[l], ffn_conv_w=ffn_conv_w[l],
                 w_in=w_in[l], w_out=w_out[l], w_gate_up=w_gate_up[l], w_down=w_down[l])
        mod = _mod(c_all, w_mod[l], b_mod[l])
        xs2, st_s, wb = _layer(xs, (mod, 0), state_conv[l], state_ffn_conv[l], cache_k_win[l], cache_v_win[l],
                               w, False)
        xp2, st_p = _layer(xp, (mod, ns), None, None, None, None, wb, True)
        xp, xs = xp2.reshape(xp.shape), xs2.reshape(xs.shape)
        outs_p.append(st_p)
        outs_s.append(st_s)

    y_prompt = _final_norm(xp.reshape(np_ * tp, d), g_final).reshape(np_, tp, d)
    y_sample = _final_norm(xs.reshape(ns * ts, d), g_final).reshape(ns, ts, d)
    stack = lambda outs, k: jnp.stack([o[k] for o in outs])
    return (y_prompt, y_sample,
            stack(outs_p, 0), stack(outs_p, 1), stack(outs_p, 2), stack(outs_p, 3),
            stack(outs_s, 0), stack(outs_s, 1), stack(outs_s, 2), stack(outs_s, 3))
```

```python
import functools

import jax
import jax.numpy as jnp
from jax import lax
from jax.experimental import pallas as pl
from jax.experimental.pallas import tpu as pltpu

F32 = jnp.float32
BF16 = jnp.bfloat16

HEAD_DIM = 64
SCALE = HEAD_DIM ** -0.5
GROUP = 8
WINDOW = 128
BLOCK = 128
CONV_K = 3
N_MOD = 6
EPS = 1e-5
SUBLANES = 8
MIB = 1024 * 1024
VMEM_CAP = 56 * MIB


def _params(semantics, vmem_bytes):
    return pltpu.CompilerParams(dimension_semantics=semantics,
                                vmem_limit_bytes=int(min(max(vmem_bytes, 16 * MIB), VMEM_CAP)))


def _dot(a, b):
    return jnp.dot(a, b, preferred_element_type=F32)


def _dot_nt(a, b):
    return lax.dot_general(a, b, (((1,), (1,)), ((), ())), preferred_element_type=F32)


def _weight_ref(w_ref, wout_ref):
    if wout_ref is None:
        return w_ref
    wout_ref[...] = w_ref[...].astype(BF16)
    return wout_ref


def _row_spec(shape, index_map, n_row_tiles):
    if n_row_tiles == 1:
        return pl.BlockSpec(shape, index_map, pipeline_mode=pl.Buffered(1))
    return pl.BlockSpec(shape, index_map)


def _nbytes(shape, dtype):
    n = jnp.dtype(dtype).itemsize
    for s in shape:
        n *= s
    return n


def _mod_body(c_ref, *refs):
    w_refs, b_ref, o_ref = refs[:N_MOD], refs[N_MOD], refs[N_MOD + 1]
    c = c_ref[...]
    a = (c * jax.nn.sigmoid(c)).astype(BF16)
    for k in range(N_MOD):
        o_ref[:, k, :] = _dot(a, w_refs[k][...].astype(BF16)) + b_ref[k:k + 1, :]


def _mod(c_all, w_mod, b_mod, tn=128):
    nc, d = c_all.shape
    nj = d // tn
    w_specs = [pl.BlockSpec((d, tn), functools.partial(lambda j, k: (0, k * nj + j), k=k)) for k in range(N_MOD)]
    vmem = 2 * (nc * d * 4 + N_MOD * d * tn * 4 + nc * SUBLANES * tn * 4) + N_MOD * d * tn * 2 + nc * d * 4
    return pl.pallas_call(
        _mod_body,
        grid=(nj,),
        in_specs=[pl.BlockSpec((nc, d), lambda j: (0, 0))] + w_specs + [pl.BlockSpec((N_MOD, tn), lambda j: (0, j))],
        out_specs=pl.BlockSpec((nc, N_MOD, tn), lambda j: (0, 0, j)),
        out_shape=jax.ShapeDtypeStruct((nc, N_MOD, d), F32),
        compiler_params=_params(("parallel",), vmem + 4 * MIB),
        name="mod",
    )(c_all, *([w_mod] * N_MOD), b_mod.reshape(N_MOD, d))


def _rms(x):
    return x * lax.rsqrt(jnp.mean(x * x, axis=-1, keepdims=True) + EPS)


def _norm_mod_body(x_ref, mod_ref, g_ref, o_ref, *, shift_idx, scale_idx):
    x = x_ref[...]
    y = _rms(x) * g_ref[...]
    h = y * (1.0 + mod_ref[:, scale_idx:scale_idx + 1, :]) + mod_ref[:, shift_idx:shift_idx + 1, :]
    o_ref[...] = h.reshape(o_ref.shape).astype(o_ref.dtype)


def _norm_mod(x3, mod3, g, shift_idx, scale_idx, bn, tt):
    n, t, d = x3.shape
    mod, row0 = mod3
    assert row0 % bn == 0
    b0 = row0 // bn
    nt = t // tt
    vmem = 2 * (bn * tt * d * 4 + bn * tt * d * 2 + bn * SUBLANES * d * 4) + 3 * bn * tt * d * 4
    return pl.pallas_call(
        functools.partial(_norm_mod_body, shift_idx=shift_idx, scale_idx=scale_idx),
        grid=(n // bn, nt),
        in_specs=[pl.BlockSpec((bn, tt, d), lambda i, j: (i, j, 0)),
                  pl.BlockSpec((bn, N_MOD, d), lambda i, j: (b0 + i, 0, 0)),
                  pl.BlockSpec((1, d), lambda i, j: (0, 0))],
        out_specs=pl.BlockSpec((bn * tt, d), lambda i, j: (i * nt + j, 0)),
        out_shape=jax.ShapeDtypeStruct((n * t, d), BF16),
        compiler_params=_params(("parallel", "parallel"), vmem),
        name="norm_mod",
    )(x3, mod, g.reshape(1, d))


def _final_norm_body(x_ref, g_ref, o_ref):
    o_ref[...] = _rms(x_ref[...]) * g_ref[...]


def _final_norm(x2, g, tm=512):
    m, d = x2.shape
    tm = min(tm, m)
    return pl.pallas_call(
        _final_norm_body,
        grid=(m // tm,),
        in_specs=[pl.BlockSpec((tm, d), lambda i: (i, 0)),
                  pl.BlockSpec((1, d), lambda i: (0, 0))],
        out_specs=pl.BlockSpec((tm, d), lambda i: (i, 0)),
        out_shape=jax.ShapeDtypeStruct((m, d), F32),
        compiler_params=_params(("parallel",), 6 * tm * d * 4),
        name="final_norm",
    )(x2, g.reshape(1, d))


def _matmul_body(a_ref, w_ref, o_ref, wout_ref=None):
    o_ref[...] = _dot(a_ref[...], _weight_ref(w_ref, wout_ref)[...]).astype(o_ref.dtype)


def _matmul(a, w, n_cols, tm, tn, emit_w):
    m, k = a.shape
    n_i = m // tm
    assert n_i == 1 or not emit_w
    out_specs = [pl.BlockSpec((tm, tn), lambda i, j: (i, j))]
    out_shape = [jax.ShapeDtypeStruct((m, n_cols), F32)]
    if emit_w:
        out_specs.append(pl.BlockSpec((k, tn), lambda i, j: (0, j)))
        out_shape.append(jax.ShapeDtypeStruct((k, n_cols), BF16))
    vmem = (min(n_i, 2) * tm * k * 2 + 2 * _nbytes((k, tn), w.dtype) + (2 * k * tn * 2 if emit_w else 0)
            + 3 * tm * tn * 4)
    return pl.pallas_call(
        _matmul_body,
        grid=(n_i, n_cols // tn),
        in_specs=[_row_spec((tm, k), lambda i, j: (i, 0), n_i),
                  pl.BlockSpec((k, tn), lambda i, j: (0, j))],
        out_specs=out_specs, out_shape=out_shape,
        compiler_params=_params(("arbitrary", "arbitrary"), vmem + 4 * MIB),
        name="qkv",
    )(a, w)


def _silu(x):
    return x * jax.nn.sigmoid(x)


def _gated_body(*refs, n_w, emit_w, is_prompt, tiles_per_seq):
    h_ref = refs[0]
    w_refs = refs[1:1 + n_w]
    cw_ref = refs[1 + n_w]
    pos = 2 + n_w
    st_ref = None
    if not is_prompt:
        st_ref = refs[pos]
        pos += 1
    o_ref, aux_ref = refs[pos], refs[pos + 1]
    pos += 2
    wout_refs = refs[pos:pos + n_w] if emit_w else (None,) * n_w
    carry_ref = refs[-1] if is_prompt else None

    ws = [_weight_ref(r, o) for r, o in zip(w_refs, wout_refs)]
    cw = cw_ref[...]
    tm, tn = o_ref.shape
    rm = tm // 2
    nb = rm // SUBLANES
    t = lax.broadcasted_iota(jnp.int32, (nb, SUBLANES, tn), 1)
    if is_prompt:
        i, j = pl.program_id(0), pl.program_id(1)

        @pl.when((i % tiles_per_seq) == 0)
        def _():
            carry_ref[j] = jnp.zeros((SUBLANES, tn), F32)

        tail = carry_ref[j]
    for r0 in (0, rm):
        h = h_ref[r0:r0 + rm, :]
        if n_w == 3:
            v = _dot(h, ws[1][...]) * _dot(h, ws[2][...])
        else:
            v = _dot(h, ws[0][...])
        v3 = v.reshape(nb, SUBLANES, tn)
        if is_prompt:
            ext = jnp.concatenate([tail[None], v3], axis=0)
            e1 = pltpu.roll(ext, 1, axis=1)
            e2 = pltpu.roll(ext, 2, axis=1)
            v1 = jnp.where(t == 0, e1[:nb], e1[1:])
            v2 = jnp.where(t < 2, e2[:nb], e2[1:])
            tail = v3[nb - 1]
        else:
            g0 = r0 // SUBLANES
            p0 = st_ref[g0:g0 + nb, 0:1, :]
            p1 = st_ref[g0:g0 + nb, 1:2, :]
            v1 = jnp.where(t == 0, p1, pltpu.roll(v3, 1, axis=1))
            v2 = jnp.where(t == 0, p0, jnp.where(t == 1, p1, pltpu.roll(v3, 2, axis=1)))
            aux_ref[g0:g0 + nb] = v3[:, SUBLANES - (CONV_K - 1):, :]
        y = (cw[0:1, :] * v2 + cw[1:2, :] * v1 + cw[2:3, :] * v3).reshape(rm, tn)
        if n_w == 3:
            out = _dot(h, ws[0][...]) * y
        else:
            out = _silu(y) * _dot(h, ws[1][...])
        o_ref[r0:r0 + rm, :] = out.astype(o_ref.dtype)
    if is_prompt:
        carry_ref[j] = tail
        aux_ref[0] = tail


def _gated_proj(h, ws, col_offsets, n_cols, conv_w, state, seq_len, tm, tn, emit_w, name):
    m, k = h.shape
    n_w = len(ws)
    n_i, nj = m // tm, pl.cdiv(n_cols, tn)
    assert all(o % tn == 0 for o in col_offsets)
    offs = [o // tn for o in col_offsets]
    is_prompt = state is None
    spec = ((lambda shape, f: pl.BlockSpec(shape, f)) if is_prompt else
            (lambda shape, f: pl.BlockSpec(shape, lambda j, i: f(i, j))))
    w_specs = [spec((k, tn), functools.partial(lambda i, j, o: (0, o + j), o=o)) for o in offs]
    h_spec = _row_spec((tm, k), (lambda i, j: (i, 0)) if is_prompt else (lambda j, i: (i, 0)), n_i)
    in_specs = [h_spec] + w_specs + [spec((CONV_K, tn), lambda i, j: (0, j))]
    out_specs = [spec((tm, tn), lambda i, j: (i, j))]
    out_shape = [jax.ShapeDtypeStruct((m, n_cols), BF16)]
    args = (h,) + tuple(ws) + (conv_w,)
    if is_prompt:
        assert seq_len % tm == 0
        tiles_per_seq = seq_len // tm
        grid = (n_i, nj)
        out_specs.append(spec((1, SUBLANES, tn), lambda i, j: (i, 0, j)))
        out_shape.append(jax.ShapeDtypeStruct((n_i, SUBLANES, n_cols), F32))
        scratch = [pltpu.VMEM((nj, SUBLANES, tn), F32)]
    else:
        assert seq_len == SUBLANES and tm % SUBLANES == 0
        tiles_per_seq = None
        grid = (nj, n_i)
        nb = tm // SUBLANES
        in_specs.append(spec((nb, CONV_K - 1, tn), lambda i, j: (i, 0, j)))
        out_specs.append(spec((nb, CONV_K - 1, tn), lambda i, j: (i, 0, j)))
        out_shape.append(jax.ShapeDtypeStruct(state.shape, F32))
        scratch = []
        args += (state,)
    if emit_w:
        out_specs += [spec((k, tn), lambda i, j: (0, j))] * n_w
        out_shape += [jax.ShapeDtypeStruct((k, n_cols), BF16)] * n_w
    vmem = (min(n_i, 2) * tm * k * 2 + 2 * sum(_nbytes((k, tn), w.dtype) for w in ws)
            + (2 * n_w * k * tn * 2 if emit_w else 0) + 2 * tm * tn * 2 + 14 * (tm // 2) * tn * 4)
    return pl.pallas_call(
        functools.partial(_gated_body, n_w=n_w, emit_w=emit_w, is_prompt=is_prompt, tiles_per_seq=tiles_per_seq),
        grid=grid, in_specs=in_specs, out_specs=out_specs, out_shape=out_shape,
        scratch_shapes=scratch, compiler_params=_params(("arbitrary", "arbitrary"), vmem + 4 * MIB), name=name,
    )(*args)


def _sink_softmax(s, mask, sink, axis):
    s = jnp.where(mask, s, -jnp.inf)
    mx = jnp.maximum(jnp.max(s, axis=axis, keepdims=True), sink)
    p = jnp.exp(s - mx)
    denom = jnp.sum(p, axis=axis, keepdims=True) + jnp.exp(sink - mx)
    return p * (1.0 / denom)


def _dot_tn(a, b):
    return lax.dot_general(a, b, (((0,), (0,)), ((), ())), preferred_element_type=F32)


def _block_diag_pair(tile, e):
    lo = lax.broadcasted_iota(jnp.int32, tile.shape, 1) < HEAD_DIM
    rolled = pltpu.roll(tile, HEAD_DIM, axis=1)
    zero = jnp.zeros_like(tile)
    if e == 0:
        top, bot = jnp.where(lo, tile, zero), jnp.where(lo, zero, rolled)
    else:
        top, bot = jnp.where(lo, rolled, zero), jnp.where(lo, zero, tile)
    return jnp.concatenate([top, bot], axis=0).astype(BF16)


def _prompt_attn_body(sink_ref, q_ref, kp_ref, kc_ref, vp_ref, vc_ref, o_ref, *, n_kv):
    b = pl.program_id(1)
    kk = jnp.concatenate([kp_ref[...], kc_ref[...]], axis=0)
    vv = jnp.concatenate([vp_ref[...], vc_ref[...]], axis=0)
    c = lax.broadcasted_iota(jnp.int32, (2 * BLOCK, BLOCK), 0)
    r = lax.broadcasted_iota(jnp.int32, (2 * BLOCK, BLOCK), 1)
    mask = ((c >= r) & (c <= r + WINDOW) & ((c >= BLOCK) | (b > 0)))[None]
    first = lax.broadcasted_iota(jnp.int32, (2, 1, BLOCK), 0) == 0
    pairs_per_kv = GROUP // 2
    kbd, vbd = [], []
    for kh in range(n_kv):
        sl = slice((kh // 2) * 2 * HEAD_DIM, (kh // 2 + 1) * 2 * HEAD_DIM)
        kbd.append(_block_diag_pair(kk[:, sl], kh % 2))
        vbd.append(_block_diag_pair(vv[:, sl], kh % 2))
    n_pairs = n_kv * pairs_per_kv
    qp = [(q_ref[:, i * 2 * HEAD_DIM:(i + 1) * 2 * HEAD_DIM] * SCALE).astype(BF16) for i in range(n_pairs)]
    s = [_dot_nt(kbd[i // pairs_per_kv], qp[i]).reshape(2, 2 * BLOCK, BLOCK) for i in range(n_pairs)]
    p = []
    for i in range(n_pairs):
        sink = jnp.where(first, sink_ref[2 * i], sink_ref[2 * i + 1])
        p.append(_sink_softmax(s[i], mask, sink, axis=1).astype(BF16).reshape(4 * BLOCK, BLOCK))
    for i in range(n_pairs):
        o_ref[:, i * 2 * HEAD_DIM:(i + 1) * 2 * HEAD_DIM] = _dot_tn(p[i], vbd[i // pairs_per_kv]).astype(o_ref.dtype)


def _prompt_attention(qkv, sinks, n, t, attn_w, kv_dim):
    nb = t // BLOCK
    kcol = attn_w // kv_dim
    cur = lambda col: (lambda i, b: (i * nb + b, col))
    prev = lambda col: (lambda i, b: (i * nb + jnp.maximum(b - 1, 0), col))
    return pl.pallas_call(
        functools.partial(_prompt_attn_body, n_kv=kv_dim // HEAD_DIM),
        grid=(n, nb),
        in_specs=[pl.BlockSpec(memory_space=pltpu.MemorySpace.SMEM),
                  pl.BlockSpec((BLOCK, attn_w), cur(0)),
                  pl.BlockSpec((BLOCK, kv_dim), prev(kcol)),
                  pl.BlockSpec((BLOCK, kv_dim), cur(kcol)),
                  pl.BlockSpec((BLOCK, kv_dim), prev(kcol + 1)),
                  pl.BlockSpec((BLOCK, kv_dim), cur(kcol + 1))],
        out_specs=pl.BlockSpec((BLOCK, attn_w), cur(0)),
        out_shape=jax.ShapeDtypeStruct((n * t, attn_w), BF16),
        compiler_params=_params(("parallel", "parallel"), 24 * MIB),
        name="prompt_attention",
    )(sinks, qkv, qkv, qkv, qkv, qkv)


def _sample_attn_body(sink_ref, q_ref, kn_ref, vn_ref, ck_ref, cv_ref, o_ref, *, bn, n_kv, t):
    keys = 2 * WINDOW
    pair_w = 2 * GROUP * HEAD_DIM
    n_pairs = n_kv // 2
    c = lax.broadcasted_iota(jnp.int32, (keys, 2 * HEAD_DIM), 0)
    tok = lax.broadcasted_iota(jnp.int32, (keys, 2 * HEAD_DIM), 1) % t
    mask = ((c < WINDOW) & (c >= tok)) | ((c >= WINDOW) & (c - WINDOW <= tok))
    pad = jnp.zeros((keys - WINDOW - t, n_kv * HEAD_DIM), F32)
    lo = lax.broadcasted_iota(jnp.int32, (t, 2 * HEAD_DIM), 1) < HEAD_DIM
    zero = jnp.zeros((t, 2 * HEAD_DIM), F32)
    work = [(n, pr) for n in range(bn) for pr in range(n_pairs)]
    lanes = lambda a, pr: a[:, pr * 2 * HEAD_DIM:(pr + 1) * 2 * HEAD_DIM]
    kk = [jnp.concatenate([ck_ref[n], kn_ref[n * t:(n + 1) * t, :], pad], axis=0).astype(BF16) for n in range(bn)]
    vv = [jnp.concatenate([cv_ref[n], vn_ref[n * t:(n + 1) * t, :], pad], axis=0).astype(BF16) for n in range(bn)]

    def q_block_diag(n, pr):
        pieces = []
        for hh in range(2 * GROUP):
            e, v = hh // GROUP, hh // 2
            x = q_ref[n * t:(n + 1) * t, pr * pair_w + v * 2 * HEAD_DIM:pr * pair_w + (v + 1) * 2 * HEAD_DIM] * SCALE
            if hh % 2 != e:
                x = pltpu.roll(x, HEAD_DIM, axis=1)
            pieces.append(jnp.where(lo, x, zero) if e == 0 else jnp.where(lo, zero, x))
        return jnp.concatenate(pieces, axis=0).astype(BF16)

    s = [_dot_nt(lanes(kk[n], pr), q_block_diag(n, pr)) for n, pr in work]
    p = [_sink_softmax(s[i], mask, sink_ref[pr], axis=0).astype(BF16) for i, (n, pr) in enumerate(work)]
    for i, (n, pr) in enumerate(work):
        o = _dot_tn(p[i], lanes(vv[n], pr))
        slabs = []
        for v in range(GROUP):
            halves = []
            for hh in (2 * v, 2 * v + 1):
                x = o[hh * t:(hh + 1) * t, :]
                halves.append(pltpu.roll(x, HEAD_DIM, axis=1) if hh % 2 != hh // GROUP else x)
            slabs.append(jnp.where(lo, halves[0], halves[1]))
        o_ref[n * t:(n + 1) * t, pr * pair_w:(pr + 1) * pair_w] = jnp.concatenate(slabs, axis=1)


def _sample_attention(qkv, cache_k, cache_v, sink_lanes, n, t, attn_w, kv_dim, bn=8):
    n_kv = kv_dim // HEAD_DIM
    assert n_kv % 2 == 0 and t == SUBLANES
    kcol = attn_w // kv_dim
    return pl.pallas_call(
        functools.partial(_sample_attn_body, bn=bn, n_kv=n_kv, t=t),
        grid=(n // bn,),
        in_specs=[pl.BlockSpec((n_kv // 2, 1, 2 * HEAD_DIM), lambda i: (0, 0, 0)),
                  pl.BlockSpec((bn * t, attn_w), lambda i: (i, 0)),
                  pl.BlockSpec((bn * t, kv_dim), lambda i: (i, kcol)),
                  pl.BlockSpec((bn * t, kv_dim), lambda i: (i, kcol + 1)),
                  pl.BlockSpec((bn, WINDOW, kv_dim), lambda i: (i, 0, 0)),
                  pl.BlockSpec((bn, WINDOW, kv_dim), lambda i: (i, 0, 0))],
        out_specs=pl.BlockSpec((bn * t, attn_w), lambda i: (i, 0)),
        out_shape=jax.ShapeDtypeStruct((n * t, attn_w), F32),
        compiler_params=_params(("parallel",), 24 * MIB),
        name="sample_attention",
    )(sink_lanes, qkv, qkv, qkv, cache_k, cache_v)


def _seq_blocks(seq_len, tm, mod3):
    mod, row0 = mod3
    if seq_len >= tm:
        assert seq_len % tm == 0
        tiles = seq_len // tm
        return mod, 1, (lambda i: row0 + i // tiles)
    assert tm % seq_len == 0 and row0 % (tm // seq_len) == 0
    b0 = row0 // (tm // seq_len)
    return mod, tm // seq_len, (lambda i: b0 + i)


def _gated_residual(x, acc, mod_ref, gate_idx):
    tm, tn = acc.shape
    bn = mod_ref.shape[0]
    gate = mod_ref[:, gate_idx:gate_idx + 1, :]
    y = x.reshape(bn, tm // bn, tn) + gate * acc.reshape(bn, tm // bn, tn)
    return y.reshape(tm, tn)


def _out_proj_body(a_ref, c_ref, wa_ref, wc_ref, x_ref, mod_ref, o_ref, woa_ref=None, woc_ref=None, *, gate_idx):
    acc = (_dot(a_ref[...].astype(BF16), _weight_ref(wa_ref, woa_ref)[...])
           + _dot(c_ref[...], _weight_ref(wc_ref, woc_ref)[...]))
    o_ref[...] = _gated_residual(x_ref[...], acc, mod_ref, gate_idx)


def _out_proj(attn, conv, wa, wc, row_blocks, x, mod3, gate_idx, seq_len, tm, tn, emit_w):
    m, ka = attn.shape
    assert conv.shape[1] == ka
    d = x.shape[1]
    n_i = m // tm
    assert n_i == 1 or not emit_w
    mod, bn, seq_of = _seq_blocks(seq_len, tm, mod3)
    ra, rc = row_blocks
    out_specs = [pl.BlockSpec((tm, tn), lambda i, j: (i, j))]
    out_shape = [jax.ShapeDtypeStruct((m, d), F32)]
    if emit_w:
        out_specs += [pl.BlockSpec((ka, tn), lambda i, j: (0, j))] * 2
        out_shape += [jax.ShapeDtypeStruct((ka, d), BF16)] * 2
    vmem = (min(n_i, 2) * (_nbytes((tm, ka), attn.dtype) + tm * ka * 2) + tm * ka * 2
            + 4 * _nbytes((ka, tn), wa.dtype) + (4 * ka * tn * 2 if emit_w else 0) + 7 * tm * tn * 4)
    return pl.pallas_call(
        functools.partial(_out_proj_body, gate_idx=gate_idx),
        grid=(n_i, d // tn),
        in_specs=[_row_spec((tm, ka), lambda i, j: (i, 0), n_i),
                  _row_spec((tm, ka), lambda i, j: (i, 0), n_i),
                  pl.BlockSpec((ka, tn), lambda i, j: (ra, j)),
                  pl.BlockSpec((ka, tn), lambda i, j: (rc, j)),
                  pl.BlockSpec((tm, tn), lambda i, j: (i, j)),
                  pl.BlockSpec((bn, N_MOD, tn), lambda i, j: (seq_of(i), 0, j))],
        out_specs=out_specs, out_shape=out_shape,
        compiler_params=_params(("arbitrary", "arbitrary"), vmem + 4 * MIB),
        name="out_proj",
    )(attn, conv, wa, wc, x, mod)


def _ffn_down_body(a_ref, w_ref, x_ref, mod_ref, o_ref, wout_ref=None, *, gate_idx, nk):
    if nk == 1:
        acc = _dot(a_ref[...], _weight_ref(w_ref, wout_ref)[...])
        o_ref[...] = _gated_residual(x_ref[...], acc, mod_ref, gate_idx)
    else:
        kk = pl.program_id(2)

        @pl.when(kk == 0)
        def _():
            o_ref[...] = x_ref[...]

        tk = w_ref.shape[0]
        a = a_ref[:, pl.ds(pl.multiple_of(kk * tk, 128), tk)]
        acc = _dot(a, _weight_ref(w_ref, wout_ref)[...])
        o_ref[...] = _gated_residual(o_ref[...], acc, mod_ref, gate_idx)


def _ffn_down(act, w_down, x, mod3, gate_idx, seq_len, tm, tn, nk, emit_w):
    m, k = act.shape
    d = x.shape[1]
    tk = k // nk
    assert tk * nk == k and (m == tm or not emit_w)
    mod, bn, seq_of = _seq_blocks(seq_len, tm, mod3)
    out_specs = [pl.BlockSpec((tm, tn), lambda i, j, kk: (i, j))]
    out_shape = [jax.ShapeDtypeStruct((m, d), F32)]
    if emit_w:
        out_specs.append(pl.BlockSpec((tk, tn), lambda i, j, kk: (kk, j)))
        out_shape.append(jax.ShapeDtypeStruct((k, d), BF16))
    assert tk % 128 == 0
    n_i = m // tm
    vmem = (min(n_i, 2) * tm * k * 2 + 2 * _nbytes((tk, tn), w_down.dtype) + (2 * tk * tn * 2 if emit_w else 0)
            + 7 * tm * tn * 4)
    return pl.pallas_call(
        functools.partial(_ffn_down_body, gate_idx=gate_idx, nk=nk),
        grid=(n_i, d // tn, nk),
        in_specs=[_row_spec((tm, k), lambda i, j, kk: (i, 0), n_i),
                  pl.BlockSpec((tk, tn), lambda i, j, kk: (kk, j)),
                  pl.BlockSpec((tm, tn), lambda i, j, kk: (i, j)),
                  pl.BlockSpec((bn, N_MOD, tn), lambda i, j, kk: (seq_of(i), 0, j))],
        out_specs=out_specs, out_shape=out_shape,
        compiler_params=_params(("arbitrary", "arbitrary", "arbitrary"), vmem + 4 * MIB),
        name="ffn_down",
    )(act, w_down, x, mod)


ROW_TILE = 1024
HALF_ROW_TILE = 512
COL_TILE_BF16 = 512
COL_TILE_F32 = 256
NORM_ROWS = 512
NORM_SEQS = 32


def _tiles(is_prompt, n, t):
    if is_prompt:
        tm = min(ROW_TILE, t)
        half = min(HALF_ROW_TILE, tm)
        return tm, half, half, COL_TILE_BF16, 1, 1, min(NORM_ROWS, t)
    assert t == SUBLANES
    tm = min(ROW_TILE, n * t)
    return tm, tm, tm, COL_TILE_F32, 2, min(NORM_SEQS, n), t


def _layer(x3, mod3, conv_state, ffn_state, cache_k, cache_v, w, is_prompt):
    n, t, d = x3.shape
    m = n * t
    attn_w = d // 2
    d_conv = d - attn_w
    n_q = attn_w // HEAD_DIM
    n_kv = max(1, n_q // GROUP)
    kv_dim = n_kv * HEAD_DIM
    off_b = attn_w + 2 * kv_dim
    x2 = x3.reshape(m, d)
    emit = not is_prompt

    tm, tm_conv, tm_down, tn, nk_down, bn, tt = _tiles(is_prompt, n, t)
    if is_prompt:
        w_conv, off_conv = (w["w_b"], w["w_c"], w["w_h"]), (0, 0, 0)
        w_act, off_act = (w["w_g"], w["w_u"]), (0, 0)
        w_oa, w_oc, out_rows = w["w_oa"], w["w_oc"], (0, 0)
        w_qkv, w_down = w["w_qkv"], w["w_d"]
        d_ff = w_down.shape[0]
    else:
        d_ff = w["w_down"].shape[0]
        w_conv, off_conv = (w["w_in"],) * 3, (off_b, off_b + d_conv, off_b + 2 * d_conv)
        w_act, off_act = (w["w_gate_up"],) * 2, (0, d_ff)
        w_oa = w_oc = w["w_out"]
        out_rows = (0, 1)
        w_qkv, w_down = w["w_in"], w["w_down"]

    h = _norm_mod(x3, mod3, w["g_mix"], 0, 1, bn, tt)
    qkv, *wb_qkv = _matmul(h, w_qkv, off_b, tm, tn, emit)
    conv, conv_aux, *wb_conv = _gated_proj(h, w_conv, off_conv, d_conv, w["conv_w"], conv_state, t, tm_conv,
                                           tn, emit, "conv_gate")
    if is_prompt:
        attn = _prompt_attention(qkv, w["sinks"], n, t, attn_w, kv_dim)
    else:
        sink_lanes = jnp.repeat(w["sinks"].reshape(n_kv // 2, 1, 2 * GROUP), t, axis=2)
        attn = _sample_attention(qkv, cache_k.reshape(n, WINDOW, kv_dim), cache_v.reshape(n, WINDOW, kv_dim),
                                 sink_lanes, n, t, attn_w, kv_dim, bn=min(8, n))
    x1, *wb_out = _out_proj(attn, conv, w_oa, w_oc, out_rows, x2, mod3, 2, t, tm, tn, emit)

    h2 = _norm_mod(x1.reshape(n, t, d), mod3, w["g_ffn"], 3, 4, bn, tt)
    act, ffn_aux, *wb_act = _gated_proj(h2, w_act, off_act, d_ff, w["ffn_conv_w"], ffn_state, t, tm, tn, emit,
                                        "ffn_act")
    xo, *wb_down = _ffn_down(act, w_down, x1, mod3, 5, t, tm_down, tn, nk_down, emit)

    keep = min(WINDOW, t)
    kv_new = qkv.reshape(n, t, off_b)[:, t - keep:, attn_w:]
    k_new = kv_new[:, :, :kv_dim].reshape(n, keep, n_kv, HEAD_DIM)
    v_new = kv_new[:, :, kv_dim:].reshape(n, keep, n_kv, HEAD_DIM)
    if is_prompt:
        k_state, v_state = k_new, v_new
        conv_state_out = conv_aux[t // tm_conv - 1::t // tm_conv, SUBLANES - (CONV_K - 1):, :]
        ffn_state_out = ffn_aux[t // tm - 1::t // tm, SUBLANES - (CONV_K - 1):, :]
        return xo, (k_state, v_state, conv_state_out, ffn_state_out)
    k_state = jnp.concatenate([cache_k, k_new], axis=1)[:, t:]
    v_state = jnp.concatenate([cache_v, v_new], axis=1)[:, t:]
    wb = dict(w, w_qkv=wb_qkv[0], w_b=wb_conv[0], w_c=wb_conv[1], w_h=wb_conv[2], w_oa=wb_out[0], w_oc=wb_out[1],
              w_g=wb_act[0], w_u=wb_act[1], w_d=wb_down[0])
    return xo, (k_state, v_state, conv_aux, ffn_aux), wb


def kernel(x_prompt, x_sample, cache_k_win, cache_v_win, state_conv, state_ffn_conv, c_prompt, c_sample, w_mod, b_mod, g_mix, g_ffn, w_in, conv_w, sinks, w_out, w_gate_up, ffn_conv_w, w_down, g_final):
    depth = w_mod.shape[0]
    np_, tp, d = x_prompt.shape
    ns, ts, _ = x_sample.shape
    assert tp % BLOCK == 0 and cache_k_win.shape[2] == WINDOW

    n_c = np_ + ns
    n_c_pad = -(-n_c // SUBLANES) * SUBLANES
    c_all = jnp.concatenate([c_sample, c_prompt, jnp.zeros((n_c_pad - n_c, d), F32)], axis=0)

    xp, xs = x_prompt, x_sample
    outs_p, outs_s = [], []
    for l in range(depth):
        w = dict(g_mix=g_mix[l], g_ffn=g_ffn[l], conv_w=conv_w[l], sinks=sinks[l], ffn_conv_w=ffn_conv_w[l],
                 w_in=w_in[l], w_out=w_out[l], w_gate_up=w_gate_up[l], w_down=w_down[l])
        mod = _mod(c_all, w_mod[l], b_mod[l])
        xs2, st_s, wb = _layer(xs, (mod, 0), state_conv[l], state_ffn_conv[l], cache_k_win[l], cache_v_win[l],
                               w, False)
        xp2, st_p = _layer(xp, (mod, ns), None, None, None, None, wb, True)
        xp, xs = xp2.reshape(xp.shape), xs2.reshape(xs.shape)
        outs_p.append(st_p)
        outs_s.append(st_s)

    y_prompt = _final_norm(xp.reshape(np_ * tp, d), g_final).reshape(np_, tp, d)
    y_sample = _final_norm(xs.reshape(ns * ts, d), g_final).reshape(ns, ts, d)
    stack = lambda outs, k: jnp.stack([o[k] for o in outs])
    return (y_prompt, y_sample,
            stack(outs_p, 0), stack(outs_p, 1), stack(outs_p, 2), stack(outs_p, 3),
            stack(outs_s, 0), stack(outs_s, 1), stack(outs_s, 2), stack(outs_s, 3))
```

```python
import functools

import jax
import jax.numpy as jnp
from jax import lax
from jax.experimental import pallas as pl
from jax.experimental.pallas import tpu as pltpu

F32 = jnp.float32
BF16 = jnp.bfloat16

HEAD_DIM = 64
SCALE = HEAD_DIM ** -0.5
GROUP = 8
WINDOW = 128
BLOCK = 128
CONV_K = 3
N_MOD = 6
EPS = 1e-5
SUBLANES = 8
MIB = 1024 * 1024
VMEM_CAP = 56 * MIB


def _params(semantics, vmem_bytes):
    return pltpu.CompilerParams(dimension_semantics=semantics,
                                vmem_limit_bytes=int(min(max(vmem_bytes, 16 * MIB), VMEM_CAP)))


def _dot(a, b):
    return jnp.dot(a, b, preferred_element_type=F32)


def _dot_nt(a, b):
    return lax.dot_general(a, b, (((1,), (1,)), ((), ())), preferred_element_type=F32)


def _weight_ref(w_ref, wout_ref):
    if wout_ref is None:
        return w_ref
    wout_ref[...] = w_ref[...].astype(BF16)
    return wout_ref


def _row_spec(shape, index_map, n_row_tiles):
    if n_row_tiles == 1:
        return pl.BlockSpec(shape, index_map, pipeline_mode=pl.Buffered(1))
    return pl.BlockSpec(shape, index_map)


def _nbytes(shape, dtype):
    n = jnp.dtype(dtype).itemsize
    for s in shape:
        n *= s
    return n


def _mod_body(c_ref, *refs):
    w_refs, b_ref, o_ref = refs[:N_MOD], refs[N_MOD], refs[N_MOD + 1]
    c = c_ref[...]
    a = (c * jax.nn.sigmoid(c)).astype(BF16)
    for k in range(N_MOD):
        o_ref[:, k, :] = _dot(a, w_refs[k][...].astype(BF16)) + b_ref[k:k + 1, :]


def _mod(c_all, w_mod, b_mod, tn=128):
    nc, d = c_all.shape
    nj = d // tn
    w_specs = [pl.BlockSpec((d, tn), functools.partial(lambda j, k: (0, k * nj + j), k=k)) for k in range(N_MOD)]
    vmem = 2 * (nc * d * 4 + N_MOD * d * tn * 4 + nc * SUBLANES * tn * 4) + N_MOD * d * tn * 2 + nc * d * 4
    return pl.pallas_call(
        _mod_body,
        grid=(nj,),
        in_specs=[pl.BlockSpec((nc, d), lambda j: (0, 0))] + w_specs + [pl.BlockSpec((N_MOD, tn), lambda j: (0, j))],
        out_specs=pl.BlockSpec((nc, N_MOD, tn), lambda j: (0, 0, j)),
        out_shape=jax.ShapeDtypeStruct((nc, N_MOD, d), F32),
        compiler_params=_params(("parallel",), vmem + 4 * MIB),
        name="mod",
    )(c_all, *([w_mod] * N_MOD), b_mod.reshape(N_MOD, d))


def _rms(x):
    return x * lax.rsqrt(jnp.mean(x * x, axis=-1, keepdims=True) + EPS)


def _norm_mod_body(x_ref, mod_ref, g_ref, o_ref, *, shift_idx, scale_idx):
    x = x_ref[...]
    y = _rms(x) * g_ref[...]
    h = y * (1.0 + mod_ref[:, scale_idx:scale_idx + 1, :]) + mod_ref[:, shift_idx:shift_idx + 1, :]
    o_ref[...] = h.reshape(o_ref.shape).astype(o_ref.dtype)


def _norm_mod(x3, mod3, g, shift_idx, scale_idx, bn, tt):
    n, t, d = x3.shape
    mod, row0 = mod3
    assert row0 % bn == 0
    b0 = row0 // bn
    nt = t // tt
    vmem = 2 * (bn * tt * d * 4 + bn * tt * d * 2 + bn * SUBLANES * d * 4) + 3 * bn * tt * d * 4
    return pl.pallas_call(
        functools.partial(_norm_mod_body, shift_idx=shift_idx, scale_idx=scale_idx),
        grid=(n // bn, nt),
        in_specs=[pl.BlockSpec((bn, tt, d), lambda i, j: (i, j, 0)),
                  pl.BlockSpec((bn, N_MOD, d), lambda i, j: (b0 + i, 0, 0)),
                  pl.BlockSpec((1, d), lambda i, j: (0, 0))],
        out_specs=pl.BlockSpec((bn * tt, d), lambda i, j: (i * nt + j, 0)),
        out_shape=jax.ShapeDtypeStruct((n * t, d), BF16),
        compiler_params=_params(("parallel", "parallel"), vmem),
        name="norm_mod",
    )(x3, mod, g.reshape(1, d))


def _final_norm_body(x_ref, g_ref, o_ref):
    o_ref[...] = _rms(x_ref[...]) * g_ref[...]


def _final_norm(x2, g, tm=512):
    m, d = x2.shape
    tm = min(tm, m)
    return pl.pallas_call(
        _final_norm_body,
        grid=(m // tm,),
        in_specs=[pl.BlockSpec((tm, d), lambda i: (i, 0)),
                  pl.BlockSpec((1, d), lambda i: (0, 0))],
        out_specs=pl.BlockSpec((tm, d), lambda i: (i, 0)),
        out_shape=jax.ShapeDtypeStruct((m, d), F32),
        compiler_params=_params(("parallel",), 6 * tm * d * 4),
        name="final_norm",
    )(x2, g.reshape(1, d))


def _matmul_body(a_ref, w_ref, o_ref, wout_ref=None):
    o_ref[...] = _dot(a_ref[...], _weight_ref(w_ref, wout_ref)[...]).astype(o_ref.dtype)


def _matmul(a, w, n_cols, tm, tn, emit_w):
    m, k = a.shape
    n_i = m // tm
    assert n_i == 1 or not emit_w
    out_specs = [pl.BlockSpec((tm, tn), lambda i, j: (i, j))]
    out_shape = [jax.ShapeDtypeStruct((m, n_cols), F32)]
    if emit_w:
        out_specs.append(pl.BlockSpec((k, tn), lambda i, j: (0, j)))
        out_shape.append(jax.ShapeDtypeStruct((k, n_cols), BF16))
    vmem = (min(n_i, 2) * tm * k * 2 + 2 * _nbytes((k, tn), w.dtype) + (2 * k * tn * 2 if emit_w else 0)
            + 3 * tm * tn * 4)
    return pl.pallas_call(
        _matmul_body,
        grid=(n_i, n_cols // tn),
        in_specs=[_row_spec((tm, k), lambda i, j: (i, 0), n_i),
                  pl.BlockSpec((k, tn), lambda i, j: (0, j))],
        out_specs=out_specs, out_shape=out_shape,
        compiler_params=_params(("arbitrary", "arbitrary"), vmem + 4 * MIB),
        name="qkv",
    )(a, w)


def _silu(x):
    return x * jax.nn.sigmoid(x)


def _gated_body(*refs, n_w, emit_w, is_prompt, tiles_per_seq):
    h_ref = refs[0]
    w_refs = refs[1:1 + n_w]
    cw_ref = refs[1 + n_w]
    pos = 2 + n_w
    st_ref = None
    if not is_prompt:
        st_ref = refs[pos]
        pos += 1
    o_ref, aux_ref = refs[pos], refs[pos + 1]
    pos += 2
    wout_refs = refs[pos:pos + n_w] if emit_w else (None,) * n_w
    carry_ref = refs[-1] if is_prompt else None

    ws = [_weight_ref(r, o) for r, o in zip(w_refs, wout_refs)]
    cw = cw_ref[...]
    tm, tn = o_ref.shape
    rm = tm // 2
    nb = rm // SUBLANES
    t = lax.broadcasted_iota(jnp.int32, (nb, SUBLANES, tn), 1)
    if is_prompt:
        i, j = pl.program_id(0), pl.program_id(1)

        @pl.when((i % tiles_per_seq) == 0)
        def _():
            carry_ref[j] = jnp.zeros((SUBLANES, tn), F32)

        tail = carry_ref[j]
    for r0 in (0, rm):
        h = h_ref[r0:r0 + rm, :]
        if n_w == 3:
            v = _dot(h, ws[1][...]) * _dot(h, ws[2][...])
        else:
            v = _dot(h, ws[0][...])
        v3 = v.reshape(nb, SUBLANES, tn)
        if is_prompt:
            ext = jnp.concatenate([tail[None], v3], axis=0)
            e1 = pltpu.roll(ext, 1, axis=1)
            e2 = pltpu.roll(ext, 2, axis=1)
            v1 = jnp.where(t == 0, e1[:nb], e1[1:])
            v2 = jnp.where(t < 2, e2[:nb], e2[1:])
            tail = v3[nb - 1]
        else:
            g0 = r0 // SUBLANES
            p0 = st_ref[g0:g0 + nb, 0:1, :]
            p1 = st_ref[g0:g0 + nb, 1:2, :]
            v1 = jnp.where(t == 0, p1, pltpu.roll(v3, 1, axis=1))
            v2 = jnp.where(t == 0, p0, jnp.where(t == 1, p1, pltpu.roll(v3, 2, axis=1)))
            aux_ref[g0:g0 + nb] = v3[:, SUBLANES - (CONV_K - 1):, :]
        y = (cw[0:1, :] * v2 + cw[1:2, :] * v1 + cw[2:3, :] * v3).reshape(rm, tn)
        if n_w == 3:
            out = _dot(h, ws[0][...]) * y
        else:
            out = _silu(y) * _dot(h, ws[1][...])
        o_ref[r0:r0 + rm, :] = out.astype(o_ref.dtype)
    if is_prompt:
        carry_ref[j] = tail
        aux_ref[0] = tail


def _gated_proj(h, ws, col_offsets, n_cols, conv_w, state, seq_len, tm, tn, emit_w, name):
    m, k = h.shape
    n_w = len(ws)
    n_i, nj = m // tm, pl.cdiv(n_cols, tn)
    assert all(o % tn == 0 for o in col_offsets)
    offs = [o // tn for o in col_offsets]
    is_prompt = state is None
    spec = ((lambda shape, f: pl.BlockSpec(shape, f)) if is_prompt else
            (lambda shape, f: pl.BlockSpec(shape, lambda j, i: f(i, j))))
    w_specs = [spec((k, tn), functools.partial(lambda i, j, o: (0, o + j), o=o)) for o in offs]
    h_spec = _row_spec((tm, k), (lambda i, j: (i, 0)) if is_prompt else (lambda j, i: (i, 0)), n_i)
    in_specs = [h_spec] + w_specs + [spec((CONV_K, tn), lambda i, j: (0, j))]
    out_specs = [spec((tm, tn), lambda i, j: (i, j))]
    out_shape = [jax.ShapeDtypeStruct((m, n_cols), BF16)]
    args = (h,) + tuple(ws) + (conv_w,)
    if is_prompt:
        assert seq_len % tm == 0
        tiles_per_seq = seq_len // tm
        grid = (n_i, nj)
        out_specs.append(spec((1, SUBLANES, tn), lambda i, j: (i, 0, j)))
        out_shape.append(jax.ShapeDtypeStruct((n_i, SUBLANES, n_cols), F32))
        scratch = [pltpu.VMEM((nj, SUBLANES, tn), F32)]
    else:
        assert seq_len == SUBLANES and tm % SUBLANES == 0
        tiles_per_seq = None
        grid = (nj, n_i)
        nb = tm // SUBLANES
        in_specs.append(spec((nb, CONV_K - 1, tn), lambda i, j: (i, 0, j)))
        out_specs.append(spec((nb, CONV_K - 1, tn), lambda i, j: (i, 0, j)))
        out_shape.append(jax.ShapeDtypeStruct(state.shape, F32))
        scratch = []
        args += (state,)
    if emit_w:
        out_specs += [spec((k, tn), lambda i, j: (0, j))] * n_w
        out_shape += [jax.ShapeDtypeStruct((k, n_cols), BF16)] * n_w
    vmem = (min(n_i, 2) * tm * k * 2 + 2 * sum(_nbytes((k, tn), w.dtype) for w in ws)
            + (2 * n_w * k * tn * 2 if emit_w else 0) + 2 * tm * tn * 2 + 14 * (tm // 2) * tn * 4)
    return pl.pallas_call(
        functools.partial(_gated_body, n_w=n_w, emit_w=emit_w, is_prompt=is_prompt, tiles_per_seq=tiles_per_seq),
        grid=grid, in_specs=in_specs, out_specs=out_specs, out_shape=out_shape,
        scratch_shapes=scratch, compiler_params=_params(("arbitrary", "arbitrary"), vmem + 4 * MIB), name=name,
    )(*args)


def _sink_softmax(s, mask, sink, axis):
    s = jnp.where(mask, s, -jnp.inf)
    mx = jnp.maximum(jnp.max(s, axis=axis, keepdims=True), sink)
    p = jnp.exp(s - mx)
    denom = jnp.sum(p, axis=axis, keepdims=True) + jnp.exp(sink - mx)
    return p * (1.0 / denom)


def _dot_tn(a, b):
    return lax.dot_general(a, b, (((0,), (0,)), ((), ())), preferred_element_type=F32)


def _block_diag_pair(tile, e):
    lo = lax.broadcasted_iota(jnp.int32, tile.shape, 1) < HEAD_DIM
    rolled = pltpu.roll(tile, HEAD_DIM, axis=1)
    zero = jnp.zeros_like(tile)
    if e == 0:
        top, bot = jnp.where(lo, tile, zero), jnp.where(lo, zero, rolled)
    else:
        top, bot = jnp.where(lo, rolled, zero), jnp.where(lo, zero, tile)
    return jnp.concatenate([top, bot], axis=0).astype(BF16)


def _prompt_attn_body(sink_ref, q_ref, kp_ref, kc_ref, vp_ref, vc_ref, o_ref, *, n_kv, blocks):
    step = pl.program_id(1)
    kall = jnp.concatenate([kp_ref[...], kc_ref[...]], axis=0)
    vall = jnp.concatenate([vp_ref[...], vc_ref[...]], axis=0)
    c = lax.broadcasted_iota(jnp.int32, (2 * BLOCK, BLOCK), 0)
    r = lax.broadcasted_iota(jnp.int32, (2 * BLOCK, BLOCK), 1)
    band = (c >= r) & (c <= r + WINDOW)
    first = lax.broadcasted_iota(jnp.int32, (2, 1, BLOCK), 0) == 0
    pairs_per_kv = GROUP // 2
    n_pairs = n_kv * pairs_per_kv
    for u in range(blocks):
        kk = kall[u * BLOCK:(u + 2) * BLOCK, :]
        vv = vall[u * BLOCK:(u + 2) * BLOCK, :]
        mask = (band & ((c >= BLOCK) | (step * blocks + u > 0)))[None]
        kbd, vbd = [], []
        for kh in range(n_kv):
            sl = slice((kh // 2) * 2 * HEAD_DIM, (kh // 2 + 1) * 2 * HEAD_DIM)
            kbd.append(_block_diag_pair(kk[:, sl], kh % 2))
            vbd.append(_block_diag_pair(vv[:, sl], kh % 2))
        rows = slice(u * BLOCK, (u + 1) * BLOCK)
        qp = [(q_ref[rows, i * 2 * HEAD_DIM:(i + 1) * 2 * HEAD_DIM] * SCALE).astype(BF16) for i in range(n_pairs)]
        s = [_dot_nt(kbd[i // pairs_per_kv], qp[i]).reshape(2, 2 * BLOCK, BLOCK) for i in range(n_pairs)]
        p = []
        for i in range(n_pairs):
            sink = jnp.where(first, sink_ref[2 * i], sink_ref[2 * i + 1])
            p.append(_sink_softmax(s[i], mask, sink, axis=1).astype(BF16).reshape(4 * BLOCK, BLOCK))
        for i in range(n_pairs):
            o_ref[rows, i * 2 * HEAD_DIM:(i + 1) * 2 * HEAD_DIM] = _dot_tn(p[i], vbd[i // pairs_per_kv]).astype(o_ref.dtype)


def _prompt_attention(qkv, sinks, n, t, attn_w, kv_dim):
    nb = t // BLOCK
    blocks = next(b for b in (4, 2, 1) if nb % b == 0)
    ns = nb // blocks
    kcol = attn_w // kv_dim
    cur = lambda col: (lambda i, b: (i * ns + b, col))
    prev = lambda col: (lambda i, b: (i * nb + jnp.maximum(b * blocks - 1, 0), col))
    return pl.pallas_call(
        functools.partial(_prompt_attn_body, n_kv=kv_dim // HEAD_DIM, blocks=blocks),
        grid=(n, ns),
        in_specs=[pl.BlockSpec(memory_space=pltpu.MemorySpace.SMEM),
                  pl.BlockSpec((blocks * BLOCK, attn_w), cur(0)),
                  pl.BlockSpec((BLOCK, kv_dim), prev(kcol)),
                  pl.BlockSpec((blocks * BLOCK, kv_dim), cur(kcol)),
                  pl.BlockSpec((BLOCK, kv_dim), prev(kcol + 1)),
                  pl.BlockSpec((blocks * BLOCK, kv_dim), cur(kcol + 1))],
        out_specs=pl.BlockSpec((blocks * BLOCK, attn_w), cur(0)),
        out_shape=jax.ShapeDtypeStruct((n * t, attn_w), BF16),
        compiler_params=_params(("parallel", "parallel"), 32 * MIB),
        name="prompt_attention",
    )(sinks, qkv, qkv, qkv, qkv, qkv)


def _sample_attn_body(sink_ref, q_ref, kn_ref, vn_ref, ck_ref, cv_ref, o_ref, *, bn, n_kv, t):
    keys = 2 * WINDOW
    pair_w = 2 * GROUP * HEAD_DIM
    n_pairs = n_kv // 2
    c = lax.broadcasted_iota(jnp.int32, (keys, 2 * HEAD_DIM), 0)
    tok = lax.broadcasted_iota(jnp.int32, (keys, 2 * HEAD_DIM), 1) % t
    mask = ((c < WINDOW) & (c >= tok)) | ((c >= WINDOW) & (c - WINDOW <= tok))
    pad = jnp.zeros((keys - WINDOW - t, n_kv * HEAD_DIM), F32)
    lo = lax.broadcasted_iota(jnp.int32, (t, 2 * HEAD_DIM), 1) < HEAD_DIM
    zero = jnp.zeros((t, 2 * HEAD_DIM), F32)
    work = [(n, pr) for n in range(bn) for pr in range(n_pairs)]
    lanes = lambda a, pr: a[:, pr * 2 * HEAD_DIM:(pr + 1) * 2 * HEAD_DIM]
    kk = [jnp.concatenate([ck_ref[n], kn_ref[n * t:(n + 1) * t, :], pad], axis=0).astype(BF16) for n in range(bn)]
    vv = [jnp.concatenate([cv_ref[n], vn_ref[n * t:(n + 1) * t, :], pad], axis=0).astype(BF16) for n in range(bn)]

    def q_block_diag(n, pr):
        pieces = []
        for hh in range(2 * GROUP):
            e, v = hh // GROUP, hh // 2
            x = q_ref[n * t:(n + 1) * t, pr * pair_w + v * 2 * HEAD_DIM:pr * pair_w + (v + 1) * 2 * HEAD_DIM] * SCALE
            if hh % 2 != e:
                x = pltpu.roll(x, HEAD_DIM, axis=1)
            pieces.append(jnp.where(lo, x, zero) if e == 0 else jnp.where(lo, zero, x))
        return jnp.concatenate(pieces, axis=0).astype(BF16)

    s = [_dot_nt(lanes(kk[n], pr), q_block_diag(n, pr)) for n, pr in work]
    p = [_sink_softmax(s[i], mask, sink_ref[pr], axis=0).astype(BF16) for i, (n, pr) in enumerate(work)]
    for i, (n, pr) in enumerate(work):
        o = _dot_tn(p[i], lanes(vv[n], pr))
        slabs = []
        for v in range(GROUP):
            halves = []
            for hh in (2 * v, 2 * v + 1):
                x = o[hh * t:(hh + 1) * t, :]
                halves.append(pltpu.roll(x, HEAD_DIM, axis=1) if hh % 2 != hh // GROUP else x)
            slabs.append(jnp.where(lo, halves[0], halves[1]))
        o_ref[n * t:(n + 1) * t, pr * pair_w:(pr + 1) * pair_w] = jnp.concatenate(slabs, axis=1)


def _sample_attention(qkv, cache_k, cache_v, sink_lanes, n, t, attn_w, kv_dim, bn=8):
    n_kv = kv_dim // HEAD_DIM
    assert n_kv % 2 == 0 and t == SUBLANES
    kcol = attn_w // kv_dim
    return pl.pallas_call(
        functools.partial(_sample_attn_body, bn=bn, n_kv=n_kv, t=t),
        grid=(n // bn,),
        in_specs=[pl.BlockSpec((n_kv // 2, 1, 2 * HEAD_DIM), lambda i: (0, 0, 0)),
                  pl.BlockSpec((bn * t, attn_w), lambda i: (i, 0)),
                  pl.BlockSpec((bn * t, kv_dim), lambda i: (i, kcol)),
                  pl.BlockSpec((bn * t, kv_dim), lambda i: (i, kcol + 1)),
                  pl.BlockSpec((bn, WINDOW, kv_dim), lambda i: (i, 0, 0)),
                  pl.BlockSpec((bn, WINDOW, kv_dim), lambda i: (i, 0, 0))],
        out_specs=pl.BlockSpec((bn * t, attn_w), lambda i: (i, 0)),
        out_shape=jax.ShapeDtypeStruct((n * t, attn_w), F32),
        compiler_params=_params(("parallel",), 24 * MIB),
        name="sample_attention",
    )(sink_lanes, qkv, qkv, qkv, cache_k, cache_v)


def _seq_blocks(seq_len, tm, mod3):
    mod, row0 = mod3
    if seq_len >= tm:
        assert seq_len % tm == 0
        tiles = seq_len // tm
        return mod, 1, (lambda i: row0 + i // tiles)
    assert tm % seq_len == 0 and row0 % (tm // seq_len) == 0
    b0 = row0 // (tm // seq_len)
    return mod, tm // seq_len, (lambda i: b0 + i)


def _gated_residual(x, acc, mod_ref, gate_idx):
    tm, tn = acc.shape
    bn = mod_ref.shape[0]
    gate = mod_ref[:, gate_idx:gate_idx + 1, :]
    y = x.reshape(bn, tm // bn, tn) + gate * acc.reshape(bn, tm // bn, tn)
    return y.reshape(tm, tn)


def _out_proj_body(a_ref, c_ref, wa_ref, wc_ref, x_ref, mod_ref, o_ref, woa_ref=None, woc_ref=None, *, gate_idx):
    acc = (_dot(a_ref[...].astype(BF16), _weight_ref(wa_ref, woa_ref)[...])
           + _dot(c_ref[...], _weight_ref(wc_ref, woc_ref)[...]))
    o_ref[...] = _gated_residual(x_ref[...], acc, mod_ref, gate_idx)


def _out_proj(attn, conv, wa, wc, row_blocks, x, mod3, gate_idx, seq_len, tm, tn, emit_w):
    m, ka = attn.shape
    assert conv.shape[1] == ka
    d = x.shape[1]
    n_i = m // tm
    assert n_i == 1 or not emit_w
    mod, bn, seq_of = _seq_blocks(seq_len, tm, mod3)
    ra, rc = row_blocks
    out_specs = [pl.BlockSpec((tm, tn), lambda i, j: (i, j))]
    out_shape = [jax.ShapeDtypeStruct((m, d), F32)]
    if emit_w:
        out_specs += [pl.BlockSpec((ka, tn), lambda i, j: (0, j))] * 2
        out_shape += [jax.ShapeDtypeStruct((ka, d), BF16)] * 2
    vmem = (min(n_i, 2) * (_nbytes((tm, ka), attn.dtype) + tm * ka * 2) + tm * ka * 2
            + 4 * _nbytes((ka, tn), wa.dtype) + (4 * ka * tn * 2 if emit_w else 0) + 7 * tm * tn * 4)
    return pl.pallas_call(
        functools.partial(_out_proj_body, gate_idx=gate_idx),
        grid=(n_i, d // tn),
        in_specs=[_row_spec((tm, ka), lambda i, j: (i, 0), n_i),
                  _row_spec((tm, ka), lambda i, j: (i, 0), n_i),
                  pl.BlockSpec((ka, tn), lambda i, j: (ra, j)),
                  pl.BlockSpec((ka, tn), lambda i, j: (rc, j)),
                  pl.BlockSpec((tm, tn), lambda i, j: (i, j)),
                  pl.BlockSpec((bn, N_MOD, tn), lambda i, j: (seq_of(i), 0, j))],
        out_specs=out_specs, out_shape=out_shape,
        compiler_params=_params(("arbitrary", "arbitrary"), vmem + 4 * MIB),
        name="out_proj",
    )(attn, conv, wa, wc, x, mod)


def _ffn_down_body(a_ref, w_ref, x_ref, mod_ref, o_ref, wout_ref=None, *, gate_idx, nk):
    if nk == 1:
        acc = _dot(a_ref[...], _weight_ref(w_ref, wout_ref)[...])
        o_ref[...] = _gated_residual(x_ref[...], acc, mod_ref, gate_idx)
    else:
        kk = pl.program_id(2)

        @pl.when(kk == 0)
        def _():
            o_ref[...] = x_ref[...]

        tk = w_ref.shape[0]
        a = a_ref[:, pl.ds(pl.multiple_of(kk * tk, 128), tk)]
        acc = _dot(a, _weight_ref(w_ref, wout_ref)[...])
        o_ref[...] = _gated_residual(o_ref[...], acc, mod_ref, gate_idx)


def _ffn_down(act, w_down, x, mod3, gate_idx, seq_len, tm, tn, nk, emit_w):
    m, k = act.shape
    d = x.shape[1]
    tk = k // nk
    assert tk * nk == k and (m == tm or not emit_w)
    mod, bn, seq_of = _seq_blocks(seq_len, tm, mod3)
    out_specs = [pl.BlockSpec((tm, tn), lambda i, j, kk: (i, j))]
    out_shape = [jax.ShapeDtypeStruct((m, d), F32)]
    if emit_w:
        out_specs.append(pl.BlockSpec((tk, tn), lambda i, j, kk: (kk, j)))
        out_shape.append(jax.ShapeDtypeStruct((k, d), BF16))
    assert tk % 128 == 0
    n_i = m // tm
    vmem = (min(n_i, 2) * tm * k * 2 + 2 * _nbytes((tk, tn), w_down.dtype) + (2 * tk * tn * 2 if emit_w else 0)
            + 7 * tm * tn * 4)
    return pl.pallas_call(
        functools.partial(_ffn_down_body, gate_idx=gate_idx, nk=nk),
        grid=(n_i, d // tn, nk),
        in_specs=[_row_spec((tm, k), lambda i, j, kk: (i, 0), n_i),
                  pl.BlockSpec((tk, tn), lambda i, j, kk: (kk, j)),
                  pl.BlockSpec((tm, tn), lambda i, j, kk: (i, j)),
                  pl.BlockSpec((bn, N_MOD, tn), lambda i, j, kk: (seq_of(i), 0, j))],
        out_specs=out_specs, out_shape=out_shape,
        compiler_params=_params(("arbitrary", "arbitrary", "arbitrary"), vmem + 4 * MIB),
        name="ffn_down",
    )(act, w_down, x, mod)


ROW_TILE = 1024
HALF_ROW_TILE = 512
COL_TILE_BF16 = 512
COL_TILE_F32 = 256
NORM_ROWS = 512
NORM_SEQS = 32


def _tiles(is_prompt, n, t):
    if is_prompt:
        tm = min(ROW_TILE, t)
        half = min(HALF_ROW_TILE, tm)
        return tm, half, half, COL_TILE_BF16, 1, 1, min(NORM_ROWS, t)
    assert t == SUBLANES
    tm = min(ROW_TILE, n * t)
    return tm, tm, tm, COL_TILE_F32, 2, min(NORM_SEQS, n), t


def _layer(x3, mod3, conv_state, ffn_state, cache_k, cache_v, w, is_prompt):
    n, t, d = x3.shape
    m = n * t
    attn_w = d // 2
    d_conv = d - attn_w
    n_q = attn_w // HEAD_DIM
    n_kv = max(1, n_q // GROUP)
    kv_dim = n_kv * HEAD_DIM
    off_b = attn_w + 2 * kv_dim
    x2 = x3.reshape(m, d)
    emit = not is_prompt

    tm, tm_conv, tm_down, tn, nk_down, bn, tt = _tiles(is_prompt, n, t)
    if is_prompt:
        w_conv, off_conv = (w["w_b"], w["w_c"], w["w_h"]), (0, 0, 0)
        w_act, off_act = (w["w_g"], w["w_u"]), (0, 0)
        w_oa, w_oc, out_rows = w["w_oa"], w["w_oc"], (0, 0)
        w_qkv, w_down = w["w_qkv"], w["w_d"]
        d_ff = w_down.shape[0]
    else:
        d_ff = w["w_down"].shape[0]
        w_conv, off_conv = (w["w_in"],) * 3, (off_b, off_b + d_conv, off_b + 2 * d_conv)
        w_act, off_act = (w["w_gate_up"],) * 2, (0, d_ff)
        w_oa = w_oc = w["w_out"]
        out_rows = (0, 1)
        w_qkv, w_down = w["w_in"], w["w_down"]

    h = _norm_mod(x3, mod3, w["g_mix"], 0, 1, bn, tt)
    qkv, *wb_qkv = _matmul(h, w_qkv, off_b, tm, tn, emit)
    conv, conv_aux, *wb_conv = _gated_proj(h, w_conv, off_conv, d_conv, w["conv_w"], conv_state, t, tm_conv,
                                           tn, emit, "conv_gate")
    if is_prompt:
        attn = _prompt_attention(qkv, w["sinks"], n, t, attn_w, kv_dim)
    else:
        sink_lanes = jnp.repeat(w["sinks"].reshape(n_kv // 2, 1, 2 * GROUP), t, axis=2)
        attn = _sample_attention(qkv, cache_k.reshape(n, WINDOW, kv_dim), cache_v.reshape(n, WINDOW, kv_dim),
                                 sink_lanes, n, t, attn_w, kv_dim, bn=min(8, n))
    x1, *wb_out = _out_proj(attn, conv, w_oa, w_oc, out_rows, x2, mod3, 2, t, tm, tn, emit)

    h2 = _norm_mod(x1.reshape(n, t, d), mod3, w["g_ffn"], 3, 4, bn, tt)
    act, ffn_aux, *wb_act = _gated_proj(h2, w_act, off_act, d_ff, w["ffn_conv_w"], ffn_state, t, tm, tn, emit,
                                        "ffn_act")
    xo, *wb_down = _ffn_down(act, w_down, x1, mod3, 5, t, tm_down, tn, nk_down, emit)

    keep = min(WINDOW, t)
    kv_new = qkv.reshape(n, t, off_b)[:, t - keep:, attn_w:]
    k_new = kv_new[:, :, :kv_dim].reshape(n, keep, n_kv, HEAD_DIM)
    v_new = kv_new[:, :, kv_dim:].reshape(n, keep, n_kv, HEAD_DIM)
    if is_prompt:
        k_state, v_state = k_new, v_new
        conv_state_out = conv_aux[t // tm_conv - 1::t // tm_conv, SUBLANES - (CONV_K - 1):, :]
        ffn_state_out = ffn_aux[t // tm - 1::t // tm, SUBLANES - (CONV_K - 1):, :]
        return xo, (k_state, v_state, conv_state_out, ffn_state_out)
    k_state = jnp.concatenate([cache_k, k_new], axis=1)[:, t:]
    v_state = jnp.concatenate([cache_v, v_new], axis=1)[:, t:]
    wb = dict(w, w_qkv=wb_qkv[0], w_b=wb_conv[0], w_c=wb_conv[1], w_h=wb_conv[2], w_oa=wb_out[0], w_oc=wb_out[1],
              w_g=wb_act[0], w_u=wb_act[1], w_d=wb_down[0])
    return xo, (k_state, v_state, conv_aux, ffn_aux), wb


def kernel(x_prompt, x_sample, cache_k_win, cache_v_win, state_conv, state_ffn_conv, c_prompt, c_sample, w_mod, b_mod, g_mix, g_ffn, w_in, conv_w, sinks, w_out, w_gate_up, ffn_conv_w, w_down, g_final):
    depth = w_mod.shape[0]
    np_, tp, d = x_prompt.shape
    ns, ts, _ = x_sample.shape
    assert tp % BLOCK == 0 and cache_k_win.shape[2] == WINDOW

    n_c = np_ + ns
    n_c_pad = -(-n_c // SUBLANES) * SUBLANES
    c_all = jnp.concatenate([c_sample, c_prompt, jnp.zeros((n_c_pad - n_c, d), F32)], axis=0)

    xp, xs = x_prompt, x_sample
    outs_p, outs_s = [], []
    for l in range(depth):
        w = dict(g_mix=g_mix[l], g_ffn=g_ffn[l], conv_w=conv_w[l], sinks=sinks[l], ffn_conv_w=ffn_conv_w[l],
                 w_in=w_in[l], w_out=w_out[l], w_gate_up=w_gate_up[l], w_down=w_down[l])
        mod = _mod(c_all, w_mod[l], b_mod[l])
        xs2, st_s, wb = _layer(xs, (mod, 0), state_conv[l], state_ffn_conv[l], cache_k_win[l], cache_v_win[l],
                               w, False)
        xp2, st_p = _layer(xp, (mod, ns), None, None, None, None, wb, True)
        xp, xs = xp2.reshape(xp.shape), xs2.reshape(xs.shape)
        outs_p.append(st_p)
        outs_s.append(st_s)

    y_prompt = _final_norm(xp.reshape(np_ * tp, d), g_final).reshape(np_, tp, d)
    y_sample = _final_norm(xs.reshape(ns * ts, d), g_final).reshape(ns, ts, d)
    stack = lambda outs, k: jnp.stack([o[k] for o in outs])
    return (y_prompt, y_sample,
            stack(outs_p, 0), stack(outs_p, 1), stack(outs_p, 2), stack(outs_p, 3),
            stack(outs_s, 0), stack(outs_s, 1), stack(outs_s, 2), stack(outs_s, 3))
```

```python
import functools

import jax
import jax.numpy as jnp
from jax import lax
from jax.experimental import pallas as pl
from jax.experimental.pallas import tpu as pltpu

F32 = jnp.float32
BF16 = jnp.bfloat16

HEAD_DIM = 64
SCALE = HEAD_DIM ** -0.5
GROUP = 8
WINDOW = 128
BLOCK = 128
CONV_K = 3
N_MOD = 6
EPS = 1e-5
SUBLANES = 8
MIB = 1024 * 1024
VMEM_CAP = 56 * MIB


def _params(semantics, vmem_bytes):
    return pltpu.CompilerParams(dimension_semantics=semantics,
                                vmem_limit_bytes=int(min(max(vmem_bytes, 16 * MIB), VMEM_CAP)))


def _dot(a, b):
    return jnp.dot(a, b, preferred_element_type=F32)


def _dot_nt(a, b):
    return lax.dot_general(a, b, (((1,), (1,)), ((), ())), preferred_element_type=F32)


def _weight_ref(w_ref, wout_ref):
    if wout_ref is None:
        return w_ref
    wout_ref[...] = w_ref[...].astype(BF16)
    return wout_ref


def _row_spec(shape, index_map, n_row_tiles):
    if n_row_tiles == 1:
        return pl.BlockSpec(shape, index_map, pipeline_mode=pl.Buffered(1))
    return pl.BlockSpec(shape, index_map)


def _nbytes(shape, dtype):
    n = jnp.dtype(dtype).itemsize
    for s in shape:
        n *= s
    return n


def _mod_body(c_ref, *refs):
    w_refs, b_ref, o_ref = refs[:N_MOD], refs[N_MOD], refs[N_MOD + 1]
    c = c_ref[...]
    a = (c * jax.nn.sigmoid(c)).astype(BF16)
    for k in range(N_MOD):
        o_ref[:, k, :] = _dot(a, w_refs[k][...].astype(BF16)) + b_ref[k:k + 1, :]


def _mod(c_all, w_mod, b_mod, tn=128):
    nc, d = c_all.shape
    nj = d // tn
    w_specs = [pl.BlockSpec((d, tn), functools.partial(lambda j, k: (0, k * nj + j), k=k)) for k in range(N_MOD)]
    vmem = 2 * (nc * d * 4 + N_MOD * d * tn * 4 + nc * SUBLANES * tn * 4) + N_MOD * d * tn * 2 + nc * d * 4
    return pl.pallas_call(
        _mod_body,
        grid=(nj,),
        in_specs=[pl.BlockSpec((nc, d), lambda j: (0, 0))] + w_specs + [pl.BlockSpec((N_MOD, tn), lambda j: (0, j))],
        out_specs=pl.BlockSpec((nc, N_MOD, tn), lambda j: (0, 0, j)),
        out_shape=jax.ShapeDtypeStruct((nc, N_MOD, d), F32),
        compiler_params=_params(("parallel",), vmem + 4 * MIB),
        name="mod",
    )(c_all, *([w_mod] * N_MOD), b_mod.reshape(N_MOD, d))


def _rms(x):
    return x * lax.rsqrt(jnp.mean(x * x, axis=-1, keepdims=True) + EPS)


def _norm_mod_body(x_ref, mod_ref, g_ref, o_ref, *, shift_idx, scale_idx):
    x = x_ref[...]
    gain = g_ref[...] * (1.0 + mod_ref[:, scale_idx:scale_idx + 1, :])
    h = _rms(x) * gain + mod_ref[:, shift_idx:shift_idx + 1, :]
    o_ref[...] = h.reshape(o_ref.shape).astype(o_ref.dtype)


def _norm_mod(x3, mod3, g, shift_idx, scale_idx, bn, tt):
    n, t, d = x3.shape
    mod, row0 = mod3
    assert row0 % bn == 0
    b0 = row0 // bn
    nt = t // tt
    vmem = 2 * (bn * tt * d * 4 + bn * tt * d * 2 + bn * SUBLANES * d * 4) + 3 * bn * tt * d * 4
    return pl.pallas_call(
        functools.partial(_norm_mod_body, shift_idx=shift_idx, scale_idx=scale_idx),
        grid=(n // bn, nt),
        in_specs=[pl.BlockSpec((bn, tt, d), lambda i, j: (i, j, 0)),
                  pl.BlockSpec((bn, N_MOD, d), lambda i, j: (b0 + i, 0, 0)),
                  pl.BlockSpec((1, d), lambda i, j: (0, 0))],
        out_specs=pl.BlockSpec((bn * tt, d), lambda i, j: (i * nt + j, 0)),
        out_shape=jax.ShapeDtypeStruct((n * t, d), BF16),
        compiler_params=_params(("parallel", "parallel"), vmem),
        name="norm_mod",
    )(x3, mod, g.reshape(1, d))


def _final_norm_body(x_ref, g_ref, o_ref):
    o_ref[...] = _rms(x_ref[...]) * g_ref[...]


def _final_norm(x2, g, tm=512):
    m, d = x2.shape
    tm = min(tm, m)
    return pl.pallas_call(
        _final_norm_body,
        grid=(m // tm,),
        in_specs=[pl.BlockSpec((tm, d), lambda i: (i, 0)),
                  pl.BlockSpec((1, d), lambda i: (0, 0))],
        out_specs=pl.BlockSpec((tm, d), lambda i: (i, 0)),
        out_shape=jax.ShapeDtypeStruct((m, d), F32),
        compiler_params=_params(("parallel",), 6 * tm * d * 4),
        name="final_norm",
    )(x2, g.reshape(1, d))


def _matmul_body(a_ref, w_ref, o_ref, wout_ref=None):
    o_ref[...] = _dot(a_ref[...], _weight_ref(w_ref, wout_ref)[...]).astype(o_ref.dtype)


def _matmul(a, w, n_cols, tm, tn, emit_w):
    m, k = a.shape
    n_i = m // tm
    assert n_i == 1 or not emit_w
    out_specs = [pl.BlockSpec((tm, tn), lambda i, j: (i, j))]
    out_shape = [jax.ShapeDtypeStruct((m, n_cols), F32)]
    if emit_w:
        out_specs.append(pl.BlockSpec((k, tn), lambda i, j: (0, j)))
        out_shape.append(jax.ShapeDtypeStruct((k, n_cols), BF16))
    vmem = (min(n_i, 2) * tm * k * 2 + 2 * _nbytes((k, tn), w.dtype) + (2 * k * tn * 2 if emit_w else 0)
            + 3 * tm * tn * 4)
    return pl.pallas_call(
        _matmul_body,
        grid=(n_i, n_cols // tn),
        in_specs=[_row_spec((tm, k), lambda i, j: (i, 0), n_i),
                  pl.BlockSpec((k, tn), lambda i, j: (0, j))],
        out_specs=out_specs, out_shape=out_shape,
        compiler_params=_params(("arbitrary", "arbitrary"), vmem + 4 * MIB),
        name="qkv",
    )(a, w)


def _silu(x):
    return x * jax.nn.sigmoid(x)


def _gated_body(*refs, n_w, emit_w, is_prompt, tiles_per_seq):
    h_ref = refs[0]
    w_refs = refs[1:1 + n_w]
    cw_ref = refs[1 + n_w]
    pos = 2 + n_w
    st_ref = None
    if not is_prompt:
        st_ref = refs[pos]
        pos += 1
    o_ref, aux_ref = refs[pos], refs[pos + 1]
    pos += 2
    wout_refs = refs[pos:pos + n_w] if emit_w else (None,) * n_w
    carry_ref = refs[-1] if is_prompt else None

    ws = [_weight_ref(r, o) for r, o in zip(w_refs, wout_refs)]
    cw = cw_ref[...]
    tm, tn = o_ref.shape
    rm = tm // 2
    nb = rm // SUBLANES
    t = lax.broadcasted_iota(jnp.int32, (nb, SUBLANES, tn), 1)
    if is_prompt:
        i, j = pl.program_id(0), pl.program_id(1)

        @pl.when((i % tiles_per_seq) == 0)
        def _():
            carry_ref[j] = jnp.zeros((SUBLANES, tn), F32)

        tail = carry_ref[j]
    for r0 in (0, rm):
        h = h_ref[r0:r0 + rm, :]
        if n_w == 3:
            v = _dot(h, ws[1][...]) * _dot(h, ws[2][...])
        else:
            v = _dot(h, ws[0][...])
        v3 = v.reshape(nb, SUBLANES, tn)
        if is_prompt:
            ext = jnp.concatenate([tail[None], v3], axis=0)
            e1 = pltpu.roll(ext, 1, axis=1)
            e2 = pltpu.roll(ext, 2, axis=1)
            v1 = jnp.where(t == 0, e1[:nb], e1[1:])
            v2 = jnp.where(t < 2, e2[:nb], e2[1:])
            tail = v3[nb - 1]
        else:
            g0 = r0 // SUBLANES
            p0 = st_ref[g0:g0 + nb, 0:1, :]
            p1 = st_ref[g0:g0 + nb, 1:2, :]
            v1 = jnp.where(t == 0, p1, pltpu.roll(v3, 1, axis=1))
            v2 = jnp.where(t == 0, p0, jnp.where(t == 1, p1, pltpu.roll(v3, 2, axis=1)))
            aux_ref[g0:g0 + nb] = v3[:, SUBLANES - (CONV_K - 1):, :]
        y = (cw[0:1, :] * v2 + cw[1:2, :] * v1 + cw[2:3, :] * v3).reshape(rm, tn)
        if n_w == 3:
            out = _dot(h, ws[0][...]) * y
        else:
            out = _silu(y) * _dot(h, ws[1][...])
        o_ref[r0:r0 + rm, :] = out.astype(o_ref.dtype)
    if is_prompt:
        carry_ref[j] = tail
        aux_ref[0] = tail


def _gated_proj(h, ws, col_offsets, n_cols, conv_w, state, seq_len, tm, tn, emit_w, name):
    m, k = h.shape
    n_w = len(ws)
    n_i, nj = m // tm, pl.cdiv(n_cols, tn)
    assert all(o % tn == 0 for o in col_offsets)
    offs = [o // tn for o in col_offsets]
    is_prompt = state is None
    spec = ((lambda shape, f: pl.BlockSpec(shape, f)) if is_prompt else
            (lambda shape, f: pl.BlockSpec(shape, lambda j, i: f(i, j))))
    w_specs = [spec((k, tn), functools.partial(lambda i, j, o: (0, o + j), o=o)) for o in offs]
    h_spec = _row_spec((tm, k), (lambda i, j: (i, 0)) if is_prompt else (lambda j, i: (i, 0)), n_i)
    in_specs = [h_spec] + w_specs + [spec((CONV_K, tn), lambda i, j: (0, j))]
    out_specs = [spec((tm, tn), lambda i, j: (i, j))]
    out_shape = [jax.ShapeDtypeStruct((m, n_cols), BF16)]
    args = (h,) + tuple(ws) + (conv_w,)
    if is_prompt:
        assert seq_len % tm == 0
        tiles_per_seq = seq_len // tm
        grid = (n_i, nj)
        out_specs.append(spec((1, SUBLANES, tn), lambda i, j: (i, 0, j)))
        out_shape.append(jax.ShapeDtypeStruct((n_i, SUBLANES, n_cols), F32))
        scratch = [pltpu.VMEM((nj, SUBLANES, tn), F32)]
    else:
        assert seq_len == SUBLANES and tm % SUBLANES == 0
        tiles_per_seq = None
        grid = (nj, n_i)
        nb = tm // SUBLANES
        in_specs.append(spec((nb, CONV_K - 1, tn), lambda i, j: (i, 0, j)))
        out_specs.append(spec((nb, CONV_K - 1, tn), lambda i, j: (i, 0, j)))
        out_shape.append(jax.ShapeDtypeStruct(state.shape, F32))
        scratch = []
        args += (state,)
    if emit_w:
        out_specs += [spec((k, tn), lambda i, j: (0, j))] * n_w
        out_shape += [jax.ShapeDtypeStruct((k, n_cols), BF16)] * n_w
    vmem = (min(n_i, 2) * tm * k * 2 + 2 * sum(_nbytes((k, tn), w.dtype) for w in ws)
            + (2 * n_w * k * tn * 2 if emit_w else 0) + 2 * tm * tn * 2 + 14 * (tm // 2) * tn * 4)
    return pl.pallas_call(
        functools.partial(_gated_body, n_w=n_w, emit_w=emit_w, is_prompt=is_prompt, tiles_per_seq=tiles_per_seq),
        grid=grid, in_specs=in_specs, out_specs=out_specs, out_shape=out_shape,
        scratch_shapes=scratch, compiler_params=_params(("arbitrary", "arbitrary"), vmem + 4 * MIB), name=name,
    )(*args)


def _sink_softmax(s, mask, sink, axis):
    s = jnp.where(mask, s, -jnp.inf)
    mx = jnp.maximum(jnp.max(s, axis=axis, keepdims=True), sink)
    p = jnp.exp(s - mx)
    denom = jnp.sum(p, axis=axis, keepdims=True) + jnp.exp(sink - mx)
    return p * (1.0 / denom)


def _dot_tn(a, b):
    return lax.dot_general(a, b, (((0,), (0,)), ((), ())), preferred_element_type=F32)


def _block_diag_pair(tile, e):
    lo = lax.broadcasted_iota(jnp.int32, tile.shape, 1) < HEAD_DIM
    rolled = pltpu.roll(tile, HEAD_DIM, axis=1)
    zero = jnp.zeros_like(tile)
    if e == 0:
        top, bot = jnp.where(lo, tile, zero), jnp.where(lo, zero, rolled)
    else:
        top, bot = jnp.where(lo, rolled, zero), jnp.where(lo, zero, tile)
    return jnp.concatenate([top, bot], axis=0).astype(BF16)


def _prompt_attn_body(sink_ref, q_ref, kp_ref, kc_ref, vp_ref, vc_ref, o_ref, *, n_kv, blocks):
    step = pl.program_id(1)
    kall = jnp.concatenate([kp_ref[...], kc_ref[...]], axis=0)
    vall = jnp.concatenate([vp_ref[...], vc_ref[...]], axis=0)
    c = lax.broadcasted_iota(jnp.int32, (2 * BLOCK, BLOCK), 0)
    r = lax.broadcasted_iota(jnp.int32, (2 * BLOCK, BLOCK), 1)
    band = (c >= r) & (c <= r + WINDOW)
    first = lax.broadcasted_iota(jnp.int32, (2, 1, BLOCK), 0) == 0
    pairs_per_kv = GROUP // 2
    n_pairs = n_kv * pairs_per_kv
    for u in range(blocks):
        kk = kall[u * BLOCK:(u + 2) * BLOCK, :]
        vv = vall[u * BLOCK:(u + 2) * BLOCK, :]
        mask = (band & ((c >= BLOCK) | (step * blocks + u > 0)))[None]
        kbd, vbd = [], []
        for kh in range(n_kv):
            sl = slice((kh // 2) * 2 * HEAD_DIM, (kh // 2 + 1) * 2 * HEAD_DIM)
            kbd.append(_block_diag_pair(kk[:, sl], kh % 2))
            vbd.append(_block_diag_pair(vv[:, sl], kh % 2))
        rows = slice(u * BLOCK, (u + 1) * BLOCK)
        qp = [(q_ref[rows, i * 2 * HEAD_DIM:(i + 1) * 2 * HEAD_DIM] * SCALE).astype(BF16) for i in range(n_pairs)]
        s = [_dot_nt(kbd[i // pairs_per_kv], qp[i]).reshape(2, 2 * BLOCK, BLOCK) for i in range(n_pairs)]
        p = []
        for i in range(n_pairs):
            sink = jnp.where(first, sink_ref[2 * i], sink_ref[2 * i + 1])
            p.append(_sink_softmax(s[i], mask, sink, axis=1).astype(BF16).reshape(4 * BLOCK, BLOCK))
        for i in range(n_pairs):
            o_ref[rows, i * 2 * HEAD_DIM:(i + 1) * 2 * HEAD_DIM] = _dot_tn(p[i], vbd[i // pairs_per_kv]).astype(o_ref.dtype)


def _prompt_attention(qkv, sinks, n, t, attn_w, kv_dim):
    nb = t // BLOCK
    blocks = next(b for b in (4, 2, 1) if nb % b == 0)
    ns = nb // blocks
    kcol = attn_w // kv_dim
    cur = lambda col: (lambda i, b: (i * ns + b, col))
    prev = lambda col: (lambda i, b: (i * nb + jnp.maximum(b * blocks - 1, 0), col))
    return pl.pallas_call(
        functools.partial(_prompt_attn_body, n_kv=kv_dim // HEAD_DIM, blocks=blocks),
        grid=(n, ns),
        in_specs=[pl.BlockSpec(memory_space=pltpu.MemorySpace.SMEM),
                  pl.BlockSpec((blocks * BLOCK, attn_w), cur(0)),
                  pl.BlockSpec((BLOCK, kv_dim), prev(kcol)),
                  pl.BlockSpec((blocks * BLOCK, kv_dim), cur(kcol)),
                  pl.BlockSpec((BLOCK, kv_dim), prev(kcol + 1)),
                  pl.BlockSpec((blocks * BLOCK, kv_dim), cur(kcol + 1))],
        out_specs=pl.BlockSpec((blocks * BLOCK, attn_w), cur(0)),
        out_shape=jax.ShapeDtypeStruct((n * t, attn_w), BF16),
        compiler_params=_params(("parallel", "parallel"), 32 * MIB),
        name="prompt_attention",
    )(sinks, qkv, qkv, qkv, qkv, qkv)


def _sample_attn_body(sink_ref, q_ref, kn_ref, vn_ref, ck_ref, cv_ref, o_ref, *, bn, n_kv, t):
    keys = 2 * WINDOW
    pair_w = 2 * GROUP * HEAD_DIM
    n_pairs = n_kv // 2
    c = lax.broadcasted_iota(jnp.int32, (keys, 2 * HEAD_DIM), 0)
    tok = lax.broadcasted_iota(jnp.int32, (keys, 2 * HEAD_DIM), 1) % t
    mask = ((c < WINDOW) & (c >= tok)) | ((c >= WINDOW) & (c - WINDOW <= tok))
    pad = jnp.zeros((keys - WINDOW - t, n_kv * HEAD_DIM), F32)
    lo = lax.broadcasted_iota(jnp.int32, (t, 2 * HEAD_DIM), 1) < HEAD_DIM
    zero = jnp.zeros((t, 2 * HEAD_DIM), F32)
    work = [(n, pr) for n in range(bn) for pr in range(n_pairs)]
    lanes = lambda a, pr: a[:, pr * 2 * HEAD_DIM:(pr + 1) * 2 * HEAD_DIM]
    kk = [jnp.concatenate([ck_ref[n], kn_ref[n * t:(n + 1) * t, :], pad], axis=0).astype(BF16) for n in range(bn)]
    vv = [jnp.concatenate([cv_ref[n], vn_ref[n * t:(n + 1) * t, :], pad], axis=0).astype(BF16) for n in range(bn)]

    def q_block_diag(n, pr):
        pieces = []
        for hh in range(2 * GROUP):
            e, v = hh // GROUP, hh // 2
            x = q_ref[n * t:(n + 1) * t, pr * pair_w + v * 2 * HEAD_DIM:pr * pair_w + (v + 1) * 2 * HEAD_DIM] * SCALE
            if hh % 2 != e:
                x = pltpu.roll(x, HEAD_DIM, axis=1)
            pieces.append(jnp.where(lo, x, zero) if e == 0 else jnp.where(lo, zero, x))
        return jnp.concatenate(pieces, axis=0).astype(BF16)

    s = [_dot_nt(lanes(kk[n], pr), q_block_diag(n, pr)) for n, pr in work]
    p = [_sink_softmax(s[i], mask, sink_ref[pr], axis=0).astype(BF16) for i, (n, pr) in enumerate(work)]
    for i, (n, pr) in enumerate(work):
        o = _dot_tn(p[i], lanes(vv[n], pr))
        slabs = []
        for v in range(GROUP):
            halves = []
            for hh in (2 * v, 2 * v + 1):
                x = o[hh * t:(hh + 1) * t, :]
                halves.append(pltpu.roll(x, HEAD_DIM, axis=1) if hh % 2 != hh // GROUP else x)
            slabs.append(jnp.where(lo, halves[0], halves[1]))
        o_ref[n * t:(n + 1) * t, pr * pair_w:(pr + 1) * pair_w] = jnp.concatenate(slabs, axis=1)


def _sample_attention(qkv, cache_k, cache_v, sink_lanes, n, t, attn_w, kv_dim, bn=8):
    n_kv = kv_dim // HEAD_DIM
    assert n_kv % 2 == 0 and t == SUBLANES
    kcol = attn_w // kv_dim
    return pl.pallas_call(
        functools.partial(_sample_attn_body, bn=bn, n_kv=n_kv, t=t),
        grid=(n // bn,),
        in_specs=[pl.BlockSpec((n_kv // 2, 1, 2 * HEAD_DIM), lambda i: (0, 0, 0)),
                  pl.BlockSpec((bn * t, attn_w), lambda i: (i, 0)),
                  pl.BlockSpec((bn * t, kv_dim), lambda i: (i, kcol)),
                  pl.BlockSpec((bn * t, kv_dim), lambda i: (i, kcol + 1)),
                  pl.BlockSpec((bn, WINDOW, kv_dim), lambda i: (i, 0, 0)),
                  pl.BlockSpec((bn, WINDOW, kv_dim), lambda i: (i, 0, 0))],
        out_specs=pl.BlockSpec((bn * t, attn_w), lambda i: (i, 0)),
        out_shape=jax.ShapeDtypeStruct((n * t, attn_w), F32),
        compiler_params=_params(("parallel",), 24 * MIB),
        name="sample_attention",
    )(sink_lanes, qkv, qkv, qkv, cache_k, cache_v)


def _seq_blocks(seq_len, tm, mod3):
    mod, row0 = mod3
    if seq_len >= tm:
        assert seq_len % tm == 0
        tiles = seq_len // tm
        return mod, 1, (lambda i: row0 + i // tiles)
    assert tm % seq_len == 0 and row0 % (tm // seq_len) == 0
    b0 = row0 // (tm // seq_len)
    return mod, tm // seq_len, (lambda i: b0 + i)


def _gated_residual(x, acc, mod_ref, gate_idx):
    tm, tn = acc.shape
    bn = mod_ref.shape[0]
    gate = mod_ref[:, gate_idx:gate_idx + 1, :]
    y = x.reshape(bn, tm // bn, tn) + gate * acc.reshape(bn, tm // bn, tn)
    return y.reshape(tm, tn)


def _out_proj_body(a_ref, c_ref, wa_ref, wc_ref, x_ref, mod_ref, o_ref, woa_ref=None, woc_ref=None, *, gate_idx):
    acc = (_dot(a_ref[...].astype(BF16), _weight_ref(wa_ref, woa_ref)[...])
           + _dot(c_ref[...], _weight_ref(wc_ref, woc_ref)[...]))
    o_ref[...] = _gated_residual(x_ref[...], acc, mod_ref, gate_idx)


def _out_proj(attn, conv, wa, wc, row_blocks, x, mod3, gate_idx, seq_len, tm, tn, emit_w):
    m, ka = attn.shape
    assert conv.shape[1] == ka
    d = x.shape[1]
    n_i = m // tm
    assert n_i == 1 or not emit_w
    mod, bn, seq_of = _seq_blocks(seq_len, tm, mod3)
    ra, rc = row_blocks
    out_specs = [pl.BlockSpec((tm, tn), lambda i, j: (i, j))]
    out_shape = [jax.ShapeDtypeStruct((m, d), F32)]
    if emit_w:
        out_specs += [pl.BlockSpec((ka, tn), lambda i, j: (0, j))] * 2
        out_shape += [jax.ShapeDtypeStruct((ka, d), BF16)] * 2
    vmem = (min(n_i, 2) * (_nbytes((tm, ka), attn.dtype) + tm * ka * 2) + tm * ka * 2
            + 4 * _nbytes((ka, tn), wa.dtype) + (4 * ka * tn * 2 if emit_w else 0) + 7 * tm * tn * 4)
    return pl.pallas_call(
        functools.partial(_out_proj_body, gate_idx=gate_idx),
        grid=(n_i, d // tn),
        in_specs=[_row_spec((tm, ka), lambda i, j: (i, 0), n_i),
                  _row_spec((tm, ka), lambda i, j: (i, 0), n_i),
                  pl.BlockSpec((ka, tn), lambda i, j: (ra, j)),
                  pl.BlockSpec((ka, tn), lambda i, j: (rc, j)),
                  pl.BlockSpec((tm, tn), lambda i, j: (i, j)),
                  pl.BlockSpec((bn, N_MOD, tn), lambda i, j: (seq_of(i), 0, j))],
        out_specs=out_specs, out_shape=out_shape,
        compiler_params=_params(("arbitrary", "arbitrary"), vmem + 4 * MIB),
        name="out_proj",
    )(attn, conv, wa, wc, x, mod)


def _ffn_down_body(a_ref, w_ref, x_ref, mod_ref, o_ref, wout_ref=None, *, gate_idx, nk):
    if nk == 1:
        acc = _dot(a_ref[...], _weight_ref(w_ref, wout_ref)[...])
        o_ref[...] = _gated_residual(x_ref[...], acc, mod_ref, gate_idx)
    else:
        kk = pl.program_id(2)

        @pl.when(kk == 0)
        def _():
            o_ref[...] = x_ref[...]

        tk = w_ref.shape[0]
        a = a_ref[:, pl.ds(pl.multiple_of(kk * tk, 128), tk)]
        acc = _dot(a, _weight_ref(w_ref, wout_ref)[...])
        o_ref[...] = _gated_residual(o_ref[...], acc, mod_ref, gate_idx)


def _ffn_down(act, w_down, x, mod3, gate_idx, seq_len, tm, tn, nk, emit_w):
    m, k = act.shape
    d = x.shape[1]
    tk = k // nk
    assert tk * nk == k and (m == tm or not emit_w)
    mod, bn, seq_of = _seq_blocks(seq_len, tm, mod3)
    out_specs = [pl.BlockSpec((tm, tn), lambda i, j, kk: (i, j))]
    out_shape = [jax.ShapeDtypeStruct((m, d), F32)]
    if emit_w:
        out_specs.append(pl.BlockSpec((tk, tn), lambda i, j, kk: (kk, j)))
        out_shape.append(jax.ShapeDtypeStruct((k, d), BF16))
    assert tk % 128 == 0
    n_i = m // tm
    vmem = (min(n_i, 2) * tm * k * 2 + 2 * _nbytes((tk, tn), w_down.dtype) + (2 * tk * tn * 2 if emit_w else 0)
            + 7 * tm * tn * 4)
    return pl.pallas_call(
        functools.partial(_ffn_down_body, gate_idx=gate_idx, nk=nk),
        grid=(n_i, d // tn, nk),
        in_specs=[_row_spec((tm, k), lambda i, j, kk: (i, 0), n_i),
                  pl.BlockSpec((tk, tn), lambda i, j, kk: (kk, j)),
                  pl.BlockSpec((tm, tn), lambda i, j, kk: (i, j)),
                  pl.BlockSpec((bn, N_MOD, tn), lambda i, j, kk: (seq_of(i), 0, j))],
        out_specs=out_specs, out_shape=out_shape,
        compiler_params=_params(("arbitrary", "arbitrary", "arbitrary"), vmem + 4 * MIB),
        name="ffn_down",
    )(act, w_down, x, mod)


ROW_TILE = 1024
HALF_ROW_TILE = 512
COL_TILE_BF16 = 512
COL_TILE_F32 = 256
NORM_ROWS = 512
NORM_SEQS = 32


def _tiles(is_prompt, n, t):
    if is_prompt:
        tm = min(ROW_TILE, t)
        half = min(HALF_ROW_TILE, tm)
        return tm, half, half, COL_TILE_BF16, 1, 1, min(NORM_ROWS, t)
    assert t == SUBLANES
    tm = min(ROW_TILE, n * t)
    return tm, tm, tm, COL_TILE_F32, 2, min(NORM_SEQS, n), t


def _layer(x3, mod3, conv_state, ffn_state, cache_k, cache_v, w, is_prompt):
    n, t, d = x3.shape
    m = n * t
    attn_w = d // 2
    d_conv = d - attn_w
    n_q = attn_w // HEAD_DIM
    n_kv = max(1, n_q // GROUP)
    kv_dim = n_kv * HEAD_DIM
    off_b = attn_w + 2 * kv_dim
    x2 = x3.reshape(m, d)
    emit = not is_prompt

    tm, tm_conv, tm_down, tn, nk_down, bn, tt = _tiles(is_prompt, n, t)
    if is_prompt:
        w_conv, off_conv = (w["w_b"], w["w_c"], w["w_h"]), (0, 0, 0)
        w_act, off_act = (w["w_g"], w["w_u"]), (0, 0)
        w_oa, w_oc, out_rows = w["w_oa"], w["w_oc"], (0, 0)
        w_qkv, w_down = w["w_qkv"], w["w_d"]
        d_ff = w_down.shape[0]
    else:
        d_ff = w["w_down"].shape[0]
        w_conv, off_conv = (w["w_in"],) * 3, (off_b, off_b + d_conv, off_b + 2 * d_conv)
        w_act, off_act = (w["w_gate_up"],) * 2, (0, d_ff)
        w_oa = w_oc = w["w_out"]
        out_rows = (0, 1)
        w_qkv, w_down = w["w_in"], w["w_down"]

    h = _norm_mod(x3, mod3, w["g_mix"], 0, 1, bn, tt)
    qkv, *wb_qkv = _matmul(h, w_qkv, off_b, tm, tn, emit)
    conv, conv_aux, *wb_conv = _gated_proj(h, w_conv, off_conv, d_conv, w["conv_w"], conv_state, t, tm_conv,
                                           tn, emit, "conv_gate")
    if is_prompt:
        attn = _prompt_attention(qkv, w["sinks"], n, t, attn_w, kv_dim)
    else:
        sink_lanes = jnp.repeat(w["sinks"].reshape(n_kv // 2, 1, 2 * GROUP), t, axis=2)
        attn = _sample_attention(qkv, cache_k.reshape(n, WINDOW, kv_dim), cache_v.reshape(n, WINDOW, kv_dim),
                                 sink_lanes, n, t, attn_w, kv_dim, bn=min(16, n))
    x1, *wb_out = _out_proj(attn, conv, w_oa, w_oc, out_rows, x2, mod3, 2, t, tm, tn, emit)

    h2 = _norm_mod(x1.reshape(n, t, d), mod3, w["g_ffn"], 3, 4, bn, tt)
    act, ffn_aux, *wb_act = _gated_proj(h2, w_act, off_act, d_ff, w["ffn_conv_w"], ffn_state, t, tm, tn, emit,
                                        "ffn_act")
    xo, *wb_down = _ffn_down(act, w_down, x1, mod3, 5, t, tm_down, tn, nk_down, emit)

    keep = min(WINDOW, t)
    kv_new = qkv.reshape(n, t, off_b)[:, t - keep:, attn_w:]
    k_new = kv_new[:, :, :kv_dim].reshape(n, keep, n_kv, HEAD_DIM)
    v_new = kv_new[:, :, kv_dim:].reshape(n, keep, n_kv, HEAD_DIM)
    if is_prompt:
        k_state, v_state = k_new, v_new
        conv_state_out = conv_aux[t // tm_conv - 1::t // tm_conv, SUBLANES - (CONV_K - 1):, :]
        ffn_state_out = ffn_aux[t // tm - 1::t // tm, SUBLANES - (CONV_K - 1):, :]
        return xo, (k_state, v_state, conv_state_out, ffn_state_out)
    k_state = jnp.concatenate([cache_k, k_new], axis=1)[:, t:]
    v_state = jnp.concatenate([cache_v, v_new], axis=1)[:, t:]
    wb = dict(w, w_qkv=wb_qkv[0], w_b=wb_conv[0], w_c=wb_conv[1], w_h=wb_conv[2], w_oa=wb_out[0], w_oc=wb_out[1],
              w_g=wb_act[0], w_u=wb_act[1], w_d=wb_down[0])
    return xo, (k_state, v_state, conv_aux, ffn_aux), wb


def kernel(x_prompt, x_sample, cache_k_win, cache_v_win, state_conv, state_ffn_conv, c_prompt, c_sample, w_mod, b_mod, g_mix, g_ffn, w_in, conv_w, sinks, w_out, w_gate_up, ffn_conv_w, w_down, g_final):
    depth = w_mod.shape[0]
    np_, tp, d = x_prompt.shape
    ns, ts, _ = x_sample.shape
    assert tp % BLOCK == 0 and cache_k_win.shape[2] == WINDOW

    n_c = np_ + ns
    n_c_pad = -(-n_c // SUBLANES) * SUBLANES
    c_all = jnp.concatenate([c_sample, c_prompt, jnp.zeros((n_c_pad - n_c, d), F32)], axis=0)

    xp, xs = x_prompt, x_sample
    outs_p, outs_s = [], []
    for l in range(depth):
        w = dict(g_mix=g_mix[l], g_ffn=g_ffn[l], conv_w=conv_w[l], sinks=sinks[l], ffn_conv_w=ffn_conv_w[l],
                 w_in=w_in[l], w_out=w_out[l], w_gate_up=w_gate_up[l], w_down=w_down[l])
        mod = _mod(c_all, w_mod[l], b_mod[l])
        xs2, st_s, wb = _layer(xs, (mod, 0), state_conv[l], state_ffn_conv[l], cache_k_win[l], cache_v_win[l],
                               w, False)
        xp2, st_p = _layer(xp, (mod, ns), None, None, None, None, wb, True)
        xp, xs = xp2.reshape(xp.shape), xs2.reshape(xs.shape)
        outs_p.append(st_p)
        outs_s.append(st_s)

    y_prompt = _final_norm(xp.reshape(np_ * tp, d), g_final).reshape(np_, tp, d)
    y_sample = _final_norm(xs.reshape(ns * ts, d), g_final).reshape(ns, ts, d)
    stack = lambda outs, k: jnp.stack([o[k] for o in outs])
    return (y_prompt, y_sample,
            stack(outs_p, 0), stack(outs_p, 1), stack(outs_p, 2), stack(outs_p, 3),
            stack(outs_s, 0), stack(outs_s, 1), stack(outs_s, 2), stack(outs_s, 3))
```

```python
import functools

import jax
import jax.numpy as jnp
from jax import lax
from jax.experimental import pallas as pl
from jax.experimental.pallas import tpu as pltpu

F32 = jnp.float32
BF16 = jnp.bfloat16

HEAD_DIM = 64
SCALE = HEAD_DIM ** -0.5
GROUP = 8
WINDOW = 128
BLOCK = 128
CONV_K = 3
N_MOD = 6
EPS = 1e-5
SUBLANES = 8
MIB = 1024 * 1024
VMEM_CAP = 56 * MIB


def _params(semantics, vmem_bytes):
    return pltpu.CompilerParams(dimension_semantics=semantics,
                                vmem_limit_bytes=int(min(max(vmem_bytes, 16 * MIB), VMEM_CAP)))


def _dot(a, b):
    return jnp.dot(a, b, preferred_element_type=F32)


def _dot_nt(a, b):
    return lax.dot_general(a, b, (((1,), (1,)), ((), ())), preferred_element_type=F32)


def _weight_ref(w_ref, wout_ref):
    if wout_ref is None:
        return w_ref
    wout_ref[...] = w_ref[...].astype(BF16)
    return wout_ref


def _row_spec(shape, index_map, n_row_tiles):
    if n_row_tiles == 1:
        return pl.BlockSpec(shape, index_map, pipeline_mode=pl.Buffered(1))
    return pl.BlockSpec(shape, index_map)


def _snake(nj):
    return lambda i, j: jnp.where(i % 2 == 0, j, nj - 1 - j)


def _nbytes(shape, dtype):
    n = jnp.dtype(dtype).itemsize
    for s in shape:
        n *= s
    return n


def _mod_body(c_ref, *refs):
    w_refs, b_ref, o_ref = refs[:N_MOD], refs[N_MOD], refs[N_MOD + 1]
    c = c_ref[...]
    a = (c * jax.nn.sigmoid(c)).astype(BF16)
    for k in range(N_MOD):
        o_ref[:, k, :] = _dot(a, w_refs[k][...].astype(BF16)) + b_ref[k:k + 1, :]


def _mod(c_all, w_mod, b_mod, tn=128):
    nc, d = c_all.shape
    nj = d // tn
    w_specs = [pl.BlockSpec((d, tn), functools.partial(lambda j, k: (0, k * nj + j), k=k)) for k in range(N_MOD)]
    vmem = 2 * (nc * d * 4 + N_MOD * d * tn * 4 + nc * SUBLANES * tn * 4) + N_MOD * d * tn * 2 + nc * d * 4
    return pl.pallas_call(
        _mod_body,
        grid=(nj,),
        in_specs=[pl.BlockSpec((nc, d), lambda j: (0, 0))] + w_specs + [pl.BlockSpec((N_MOD, tn), lambda j: (0, j))],
        out_specs=pl.BlockSpec((nc, N_MOD, tn), lambda j: (0, 0, j)),
        out_shape=jax.ShapeDtypeStruct((nc, N_MOD, d), F32),
        compiler_params=_params(("parallel",), vmem + 4 * MIB),
        name="mod",
    )(c_all, *([w_mod] * N_MOD), b_mod.reshape(N_MOD, d))


def _rms(x):
    return x * lax.rsqrt(jnp.mean(x * x, axis=-1, keepdims=True) + EPS)


def _norm_mod_body(x_ref, mod_ref, g_ref, o_ref, *, shift_idx, scale_idx):
    x = x_ref[...]
    gain = g_ref[...] * (1.0 + mod_ref[:, scale_idx:scale_idx + 1, :])
    h = _rms(x) * gain + mod_ref[:, shift_idx:shift_idx + 1, :]
    o_ref[...] = h.reshape(o_ref.shape).astype(o_ref.dtype)


def _norm_mod(x3, mod3, g, shift_idx, scale_idx, bn, tt):
    n, t, d = x3.shape
    mod, row0 = mod3
    assert row0 % bn == 0
    b0 = row0 // bn
    nt = t // tt
    vmem = 2 * (bn * tt * d * 4 + bn * tt * d * 2 + bn * SUBLANES * d * 4) + 3 * bn * tt * d * 4
    return pl.pallas_call(
        functools.partial(_norm_mod_body, shift_idx=shift_idx, scale_idx=scale_idx),
        grid=(n // bn, nt),
        in_specs=[pl.BlockSpec((bn, tt, d), lambda i, j: (i, j, 0)),
                  pl.BlockSpec((bn, N_MOD, d), lambda i, j: (b0 + i, 0, 0)),
                  pl.BlockSpec((1, d), lambda i, j: (0, 0))],
        out_specs=pl.BlockSpec((bn * tt, d), lambda i, j: (i * nt + j, 0)),
        out_shape=jax.ShapeDtypeStruct((n * t, d), BF16),
        compiler_params=_params(("parallel", "parallel"), vmem),
        name="norm_mod",
    )(x3, mod, g.reshape(1, d))


def _final_norm_body(x_ref, g_ref, o_ref):
    o_ref[...] = _rms(x_ref[...]) * g_ref[...]


def _final_norm(x2, g, tm=512):
    m, d = x2.shape
    tm = min(tm, m)
    return pl.pallas_call(
        _final_norm_body,
        grid=(m // tm,),
        in_specs=[pl.BlockSpec((tm, d), lambda i: (i, 0)),
                  pl.BlockSpec((1, d), lambda i: (0, 0))],
        out_specs=pl.BlockSpec((tm, d), lambda i: (i, 0)),
        out_shape=jax.ShapeDtypeStruct((m, d), F32),
        compiler_params=_params(("parallel",), 6 * tm * d * 4),
        name="final_norm",
    )(x2, g.reshape(1, d))


def _matmul_body(a_ref, w_ref, o_ref, wout_ref=None):
    o_ref[...] = _dot(a_ref[...], _weight_ref(w_ref, wout_ref)[...]).astype(o_ref.dtype)


def _matmul(a, w, n_cols, tm, tn, emit_w):
    m, k = a.shape
    n_i = m // tm
    assert n_i == 1 or not emit_w
    col = _snake(n_cols // tn)
    out_specs = [pl.BlockSpec((tm, tn), lambda i, j: (i, col(i, j)))]
    out_shape = [jax.ShapeDtypeStruct((m, n_cols), F32)]
    if emit_w:
        out_specs.append(pl.BlockSpec((k, tn), lambda i, j: (0, col(i, j))))
        out_shape.append(jax.ShapeDtypeStruct((k, n_cols), BF16))
    vmem = (min(n_i, 2) * tm * k * 2 + 2 * _nbytes((k, tn), w.dtype) + (2 * k * tn * 2 if emit_w else 0)
            + 3 * tm * tn * 4)
    return pl.pallas_call(
        _matmul_body,
        grid=(n_i, n_cols // tn),
        in_specs=[_row_spec((tm, k), lambda i, j: (i, 0), n_i),
                  pl.BlockSpec((k, tn), lambda i, j: (0, col(i, j)))],
        out_specs=out_specs, out_shape=out_shape,
        compiler_params=_params(("arbitrary", "arbitrary"), vmem + 4 * MIB),
        name="qkv",
    )(a, w)


def _silu(x):
    return x * jax.nn.sigmoid(x)


def _gated_body(*refs, n_w, emit_w, is_prompt, tiles_per_seq):
    h_ref = refs[0]
    w_refs = refs[1:1 + n_w]
    cw_ref = refs[1 + n_w]
    pos = 2 + n_w
    st_ref = None
    if not is_prompt:
        st_ref = refs[pos]
        pos += 1
    o_ref, aux_ref = refs[pos], refs[pos + 1]
    pos += 2
    wout_refs = refs[pos:pos + n_w] if emit_w else (None,) * n_w
    carry_ref = refs[-1] if is_prompt else None

    ws = [_weight_ref(r, o) for r, o in zip(w_refs, wout_refs)]
    cw = cw_ref[...]
    tm, tn = o_ref.shape
    rm = tm // 2
    nb = rm // SUBLANES
    t = lax.broadcasted_iota(jnp.int32, (nb, SUBLANES, tn), 1)
    if is_prompt:
        i, j = pl.program_id(0), pl.program_id(1)

        @pl.when((i % tiles_per_seq) == 0)
        def _():
            carry_ref[j] = jnp.zeros((SUBLANES, tn), F32)

        tail = carry_ref[j]
    for r0 in (0, rm):
        h = h_ref[r0:r0 + rm, :]
        if n_w == 3:
            v = _dot(h, ws[1][...]) * _dot(h, ws[2][...])
        else:
            v = _dot(h, ws[0][...])
        v3 = v.reshape(nb, SUBLANES, tn)
        if is_prompt:
            ext = jnp.concatenate([tail[None], v3], axis=0)
            e1 = pltpu.roll(ext, 1, axis=1)
            e2 = pltpu.roll(ext, 2, axis=1)
            v1 = jnp.where(t == 0, e1[:nb], e1[1:])
            v2 = jnp.where(t < 2, e2[:nb], e2[1:])
            tail = v3[nb - 1]
        else:
            g0 = r0 // SUBLANES
            p0 = st_ref[g0:g0 + nb, 0:1, :]
            p1 = st_ref[g0:g0 + nb, 1:2, :]
            v1 = jnp.where(t == 0, p1, pltpu.roll(v3, 1, axis=1))
            v2 = jnp.where(t == 0, p0, jnp.where(t == 1, p1, pltpu.roll(v3, 2, axis=1)))
            aux_ref[g0:g0 + nb] = v3[:, SUBLANES - (CONV_K - 1):, :]
        y = (cw[0:1, :] * v2 + cw[1:2, :] * v1 + cw[2:3, :] * v3).reshape(rm, tn)
        if n_w == 3:
            out = _dot(h, ws[0][...]) * y
        else:
            out = _silu(y) * _dot(h, ws[1][...])
        o_ref[r0:r0 + rm, :] = out.astype(o_ref.dtype)
    if is_prompt:
        carry_ref[j] = tail
        aux_ref[0] = tail


def _gated_proj(h, ws, col_offsets, n_cols, conv_w, state, seq_len, tm, tn, emit_w, name):
    m, k = h.shape
    n_w = len(ws)
    n_i, nj = m // tm, pl.cdiv(n_cols, tn)
    assert all(o % tn == 0 for o in col_offsets)
    offs = [o // tn for o in col_offsets]
    is_prompt = state is None
    spec = ((lambda shape, f: pl.BlockSpec(shape, f)) if is_prompt else
            (lambda shape, f: pl.BlockSpec(shape, lambda j, i: f(i, j))))
    w_specs = [spec((k, tn), functools.partial(lambda i, j, o: (0, o + j), o=o)) for o in offs]
    h_spec = _row_spec((tm, k), (lambda i, j: (i, 0)) if is_prompt else (lambda j, i: (i, 0)), n_i)
    in_specs = [h_spec] + w_specs + [spec((CONV_K, tn), lambda i, j: (0, j))]
    out_specs = [spec((tm, tn), lambda i, j: (i, j))]
    out_shape = [jax.ShapeDtypeStruct((m, n_cols), BF16)]
    args = (h,) + tuple(ws) + (conv_w,)
    if is_prompt:
        assert seq_len % tm == 0
        tiles_per_seq = seq_len // tm
        grid = (n_i, nj)
        out_specs.append(spec((1, SUBLANES, tn), lambda i, j: (i, 0, j)))
        out_shape.append(jax.ShapeDtypeStruct((n_i, SUBLANES, n_cols), F32))
        scratch = [pltpu.VMEM((nj, SUBLANES, tn), F32)]
    else:
        assert seq_len == SUBLANES and tm % SUBLANES == 0
        tiles_per_seq = None
        grid = (nj, n_i)
        nb = tm // SUBLANES
        in_specs.append(spec((nb, CONV_K - 1, tn), lambda i, j: (i, 0, j)))
        out_specs.append(spec((nb, CONV_K - 1, tn), lambda i, j: (i, 0, j)))
        out_shape.append(jax.ShapeDtypeStruct(state.shape, F32))
        scratch = []
        args += (state,)
    if emit_w:
        out_specs += [spec((k, tn), lambda i, j: (0, j))] * n_w
        out_shape += [jax.ShapeDtypeStruct((k, n_cols), BF16)] * n_w
    vmem = (min(n_i, 2) * tm * k * 2 + 2 * sum(_nbytes((k, tn), w.dtype) for w in ws)
            + (2 * n_w * k * tn * 2 if emit_w else 0) + 2 * tm * tn * 2 + 14 * (tm // 2) * tn * 4)
    return pl.pallas_call(
        functools.partial(_gated_body, n_w=n_w, emit_w=emit_w, is_prompt=is_prompt, tiles_per_seq=tiles_per_seq),
        grid=grid, in_specs=in_specs, out_specs=out_specs, out_shape=out_shape,
        scratch_shapes=scratch, compiler_params=_params(("arbitrary", "arbitrary"), vmem + 4 * MIB), name=name,
    )(*args)


def _sink_softmax(s, mask, sink, axis):
    s = jnp.where(mask, s, -jnp.inf)
    mx = jnp.maximum(jnp.max(s, axis=axis, keepdims=True), sink)
    p = jnp.exp(s - mx)
    denom = jnp.sum(p, axis=axis, keepdims=True) + jnp.exp(sink - mx)
    return p * (1.0 / denom)


def _dot_tn(a, b):
    return lax.dot_general(a, b, (((0,), (0,)), ((), ())), preferred_element_type=F32)


def _block_diag_pair(tile, e):
    lo = lax.broadcasted_iota(jnp.int32, tile.shape, 1) < HEAD_DIM
    rolled = pltpu.roll(tile, HEAD_DIM, axis=1)
    zero = jnp.zeros_like(tile)
    if e == 0:
        top, bot = jnp.where(lo, tile, zero), jnp.where(lo, zero, rolled)
    else:
        top, bot = jnp.where(lo, rolled, zero), jnp.where(lo, zero, tile)
    return jnp.concatenate([top, bot], axis=0).astype(BF16)


def _prompt_attn_body(sink_ref, q_ref, kp_ref, kc_ref, vp_ref, vc_ref, o_ref, *, n_kv, blocks):
    step = pl.program_id(1)
    kall = jnp.concatenate([kp_ref[...], kc_ref[...]], axis=0)
    vall = jnp.concatenate([vp_ref[...], vc_ref[...]], axis=0)
    c = lax.broadcasted_iota(jnp.int32, (2 * BLOCK, BLOCK), 0)
    r = lax.broadcasted_iota(jnp.int32, (2 * BLOCK, BLOCK), 1)
    band = (c >= r) & (c <= r + WINDOW)
    first = lax.broadcasted_iota(jnp.int32, (2, 1, BLOCK), 0) == 0
    pairs_per_kv = GROUP // 2
    n_pairs = n_kv * pairs_per_kv
    for u in range(blocks):
        kk = kall[u * BLOCK:(u + 2) * BLOCK, :]
        vv = vall[u * BLOCK:(u + 2) * BLOCK, :]
        mask = (band & ((c >= BLOCK) | (step * blocks + u > 0)))[None]
        kbd, vbd = [], []
        for kh in range(n_kv):
            sl = slice((kh // 2) * 2 * HEAD_DIM, (kh // 2 + 1) * 2 * HEAD_DIM)
            kbd.append(_block_diag_pair(kk[:, sl], kh % 2))
            vbd.append(_block_diag_pair(vv[:, sl], kh % 2))
        rows = slice(u * BLOCK, (u + 1) * BLOCK)
        qp = [(q_ref[rows, i * 2 * HEAD_DIM:(i + 1) * 2 * HEAD_DIM] * SCALE).astype(BF16) for i in range(n_pairs)]
        s = [_dot_nt(kbd[i // pairs_per_kv], qp[i]).reshape(2, 2 * BLOCK, BLOCK) for i in range(n_pairs)]
        p = []
        for i in range(n_pairs):
            sink = jnp.where(first, sink_ref[2 * i], sink_ref[2 * i + 1])
            p.append(_sink_softmax(s[i], mask, sink, axis=1).astype(BF16).reshape(4 * BLOCK, BLOCK))
        for i in range(n_pairs):
            o_ref[rows, i * 2 * HEAD_DIM:(i + 1) * 2 * HEAD_DIM] = _dot_tn(p[i], vbd[i // pairs_per_kv]).astype(o_ref.dtype)


def _prompt_attention(qkv, sinks, n, t, attn_w, kv_dim):
    nb = t // BLOCK
    blocks = next(b for b in (4, 2, 1) if nb % b == 0)
    ns = nb // blocks
    kcol = attn_w // kv_dim
    cur = lambda col: (lambda i, b: (i * ns + b, col))
    prev = lambda col: (lambda i, b: (i * nb + jnp.maximum(b * blocks - 1, 0), col))
    return pl.pallas_call(
        functools.partial(_prompt_attn_body, n_kv=kv_dim // HEAD_DIM, blocks=blocks),
        grid=(n, ns),
        in_specs=[pl.BlockSpec(memory_space=pltpu.MemorySpace.SMEM),
                  pl.BlockSpec((blocks * BLOCK, attn_w), cur(0)),
                  pl.BlockSpec((BLOCK, kv_dim), prev(kcol)),
                  pl.BlockSpec((blocks * BLOCK, kv_dim), cur(kcol)),
                  pl.BlockSpec((BLOCK, kv_dim), prev(kcol + 1)),
                  pl.BlockSpec((blocks * BLOCK, kv_dim), cur(kcol + 1))],
        out_specs=pl.BlockSpec((blocks * BLOCK, attn_w), cur(0)),
        out_shape=jax.ShapeDtypeStruct((n * t, attn_w), BF16),
        compiler_params=_params(("parallel", "parallel"), 32 * MIB),
        name="prompt_attention",
    )(sinks, qkv, qkv, qkv, qkv, qkv)


def _sample_attn_body(sink_ref, q_ref, kn_ref, vn_ref, ck_ref, cv_ref, o_ref, *, bn, n_kv, t):
    keys = 2 * WINDOW
    pair_w = 2 * GROUP * HEAD_DIM
    n_pairs = n_kv // 2
    c = lax.broadcasted_iota(jnp.int32, (keys, 2 * HEAD_DIM), 0)
    tok = lax.broadcasted_iota(jnp.int32, (keys, 2 * HEAD_DIM), 1) % t
    mask = ((c < WINDOW) & (c >= tok)) | ((c >= WINDOW) & (c - WINDOW <= tok))
    pad = jnp.zeros((keys - WINDOW - t, n_kv * HEAD_DIM), F32)
    lo = lax.broadcasted_iota(jnp.int32, (t, 2 * HEAD_DIM), 1) < HEAD_DIM
    zero = jnp.zeros((t, 2 * HEAD_DIM), F32)
    work = [(n, pr) for n in range(bn) for pr in range(n_pairs)]
    lanes = lambda a, pr: a[:, pr * 2 * HEAD_DIM:(pr + 1) * 2 * HEAD_DIM]
    kk = [jnp.concatenate([ck_ref[n], kn_ref[n * t:(n + 1) * t, :], pad], axis=0).astype(BF16) for n in range(bn)]
    vv = [jnp.concatenate([cv_ref[n], vn_ref[n * t:(n + 1) * t, :], pad], axis=0).astype(BF16) for n in range(bn)]

    def q_block_diag(n, pr):
        pieces = []
        for hh in range(2 * GROUP):
            e, v = hh // GROUP, hh // 2
            x = q_ref[n * t:(n + 1) * t, pr * pair_w + v * 2 * HEAD_DIM:pr * pair_w + (v + 1) * 2 * HEAD_DIM] * SCALE
            if hh % 2 != e:
                x = pltpu.roll(x, HEAD_DIM, axis=1)
            pieces.append(jnp.where(lo, x, zero) if e == 0 else jnp.where(lo, zero, x))
        return jnp.concatenate(pieces, axis=0).astype(BF16)

    s = [_dot_nt(lanes(kk[n], pr), q_block_diag(n, pr)) for n, pr in work]
    p = [_sink_softmax(s[i], mask, sink_ref[pr], axis=0).astype(BF16) for i, (n, pr) in enumerate(work)]
    for i, (n, pr) in enumerate(work):
        o = _dot_tn(p[i], lanes(vv[n], pr))
        slabs = []
        for v in range(GROUP):
            halves = []
            for hh in (2 * v, 2 * v + 1):
                x = o[hh * t:(hh + 1) * t, :]
                halves.append(pltpu.roll(x, HEAD_DIM, axis=1) if hh % 2 != hh // GROUP else x)
            slabs.append(jnp.where(lo, halves[0], halves[1]))
        o_ref[n * t:(n + 1) * t, pr * pair_w:(pr + 1) * pair_w] = jnp.concatenate(slabs, axis=1)


def _sample_attention(qkv, cache_k, cache_v, sink_lanes, n, t, attn_w, kv_dim, bn=8):
    n_kv = kv_dim // HEAD_DIM
    assert n_kv % 2 == 0 and t == SUBLANES
    kcol = attn_w // kv_dim
    return pl.pallas_call(
        functools.partial(_sample_attn_body, bn=bn, n_kv=n_kv, t=t),
        grid=(n // bn,),
        in_specs=[pl.BlockSpec((n_kv // 2, 1, 2 * HEAD_DIM), lambda i: (0, 0, 0)),
                  pl.BlockSpec((bn * t, attn_w), lambda i: (i, 0)),
                  pl.BlockSpec((bn * t, kv_dim), lambda i: (i, kcol)),
                  pl.BlockSpec((bn * t, kv_dim), lambda i: (i, kcol + 1)),
                  pl.BlockSpec((bn, WINDOW, kv_dim), lambda i: (i, 0, 0)),
                  pl.BlockSpec((bn, WINDOW, kv_dim), lambda i: (i, 0, 0))],
        out_specs=pl.BlockSpec((bn * t, attn_w), lambda i: (i, 0)),
        out_shape=jax.ShapeDtypeStruct((n * t, attn_w), F32),
        compiler_params=_params(("parallel",), 24 * MIB),
        name="sample_attention",
    )(sink_lanes, qkv, qkv, qkv, cache_k, cache_v)


def _seq_blocks(seq_len, tm, mod3):
    mod, row0 = mod3
    if seq_len >= tm:
        assert seq_len % tm == 0
        tiles = seq_len // tm
        return mod, 1, (lambda i: row0 + i // tiles)
    assert tm % seq_len == 0 and row0 % (tm // seq_len) == 0
    b0 = row0 // (tm // seq_len)
    return mod, tm // seq_len, (lambda i: b0 + i)


def _gated_residual(x, acc, mod_ref, gate_idx):
    tm, tn = acc.shape
    bn = mod_ref.shape[0]
    gate = mod_ref[:, gate_idx:gate_idx + 1, :]
    y = x.reshape(bn, tm // bn, tn) + gate * acc.reshape(bn, tm // bn, tn)
    return y.reshape(tm, tn)


def _out_proj_body(a_ref, c_ref, wa_ref, wc_ref, x_ref, mod_ref, o_ref, woa_ref=None, woc_ref=None, *, gate_idx):
    acc = (_dot(a_ref[...].astype(BF16), _weight_ref(wa_ref, woa_ref)[...])
           + _dot(c_ref[...], _weight_ref(wc_ref, woc_ref)[...]))
    o_ref[...] = _gated_residual(x_ref[...], acc, mod_ref, gate_idx)


def _out_proj(attn, conv, wa, wc, row_blocks, x, mod3, gate_idx, seq_len, tm, tn, emit_w):
    m, ka = attn.shape
    assert conv.shape[1] == ka
    d = x.shape[1]
    n_i = m // tm
    assert n_i == 1 or not emit_w
    mod, bn, seq_of = _seq_blocks(seq_len, tm, mod3)
    ra, rc = row_blocks
    col = _snake(d // tn)
    out_specs = [pl.BlockSpec((tm, tn), lambda i, j: (i, col(i, j)))]
    out_shape = [jax.ShapeDtypeStruct((m, d), F32)]
    if emit_w:
        out_specs += [pl.BlockSpec((ka, tn), lambda i, j: (0, col(i, j)))] * 2
        out_shape += [jax.ShapeDtypeStruct((ka, d), BF16)] * 2
    vmem = (min(n_i, 2) * (_nbytes((tm, ka), attn.dtype) + tm * ka * 2) + tm * ka * 2
            + 4 * _nbytes((ka, tn), wa.dtype) + (4 * ka * tn * 2 if emit_w else 0) + 7 * tm * tn * 4)
    return pl.pallas_call(
        functools.partial(_out_proj_body, gate_idx=gate_idx),
        grid=(n_i, d // tn),
        in_specs=[_row_spec((tm, ka), lambda i, j: (i, 0), n_i),
                  _row_spec((tm, ka), lambda i, j: (i, 0), n_i),
                  pl.BlockSpec((ka, tn), lambda i, j: (ra, col(i, j))),
                  pl.BlockSpec((ka, tn), lambda i, j: (rc, col(i, j))),
                  pl.BlockSpec((tm, tn), lambda i, j: (i, col(i, j))),
                  pl.BlockSpec((bn, N_MOD, tn), lambda i, j: (seq_of(i), 0, col(i, j)))],
        out_specs=out_specs, out_shape=out_shape,
        compiler_params=_params(("arbitrary", "arbitrary"), vmem + 4 * MIB),
        name="out_proj",
    )(attn, conv, wa, wc, x, mod)


def _ffn_down_body(a_ref, w_ref, x_ref, mod_ref, o_ref, wout_ref=None, *, gate_idx, nk):
    if nk == 1:
        acc = _dot(a_ref[...], _weight_ref(w_ref, wout_ref)[...])
        o_ref[...] = _gated_residual(x_ref[...], acc, mod_ref, gate_idx)
    else:
        kk = pl.program_id(2)

        @pl.when(kk == 0)
        def _():
            o_ref[...] = x_ref[...]

        tk = w_ref.shape[0]
        a = a_ref[:, pl.ds(pl.multiple_of(kk * tk, 128), tk)]
        acc = _dot(a, _weight_ref(w_ref, wout_ref)[...])
        o_ref[...] = _gated_residual(o_ref[...], acc, mod_ref, gate_idx)


def _ffn_down(act, w_down, x, mod3, gate_idx, seq_len, tm, tn, nk, emit_w):
    m, k = act.shape
    d = x.shape[1]
    tk = k // nk
    assert tk * nk == k and (m == tm or not emit_w)
    mod, bn, seq_of = _seq_blocks(seq_len, tm, mod3)
    col = _snake(d // tn)
    out_specs = [pl.BlockSpec((tm, tn), lambda i, j, kk: (i, col(i, j)))]
    out_shape = [jax.ShapeDtypeStruct((m, d), F32)]
    if emit_w:
        out_specs.append(pl.BlockSpec((tk, tn), lambda i, j, kk: (kk, col(i, j))))
        out_shape.append(jax.ShapeDtypeStruct((k, d), BF16))
    assert tk % 128 == 0
    n_i = m // tm
    vmem = (min(n_i, 2) * tm * k * 2 + 2 * _nbytes((tk, tn), w_down.dtype) + (2 * tk * tn * 2 if emit_w else 0)
            + 7 * tm * tn * 4)
    return pl.pallas_call(
        functools.partial(_ffn_down_body, gate_idx=gate_idx, nk=nk),
        grid=(n_i, d // tn, nk),
        in_specs=[_row_spec((tm, k), lambda i, j, kk: (i, 0), n_i),
                  pl.BlockSpec((tk, tn), lambda i, j, kk: (kk, col(i, j))),
                  pl.BlockSpec((tm, tn), lambda i, j, kk: (i, col(i, j))),
                  pl.BlockSpec((bn, N_MOD, tn), lambda i, j, kk: (seq_of(i), 0, col(i, j)))],
        out_specs=out_specs, out_shape=out_shape,
        compiler_params=_params(("arbitrary", "arbitrary", "arbitrary"), vmem + 4 * MIB),
        name="ffn_down",
    )(act, w_down, x, mod)


ROW_TILE = 1024
HALF_ROW_TILE = 512
COL_TILE_BF16 = 512
COL_TILE_F32 = 256
NORM_ROWS = 512
NORM_SEQS = 32


def _tiles(is_prompt, n, t):
    if is_prompt:
        tm = min(ROW_TILE, t)
        half = min(HALF_ROW_TILE, tm)
        return tm, half, half, COL_TILE_BF16, 1, 1, min(NORM_ROWS, t)
    assert t == SUBLANES
    tm = min(ROW_TILE, n * t)
    return tm, tm, tm, COL_TILE_F32, 2, min(NORM_SEQS, n), t


def _layer(x3, mod3, conv_state, ffn_state, cache_k, cache_v, w, is_prompt):
    n, t, d = x3.shape
    m = n * t
    attn_w = d // 2
    d_conv = d - attn_w
    n_q = attn_w // HEAD_DIM
    n_kv = max(1, n_q // GROUP)
    kv_dim = n_kv * HEAD_DIM
    off_b = attn_w + 2 * kv_dim
    x2 = x3.reshape(m, d)
    emit = not is_prompt

    tm, tm_conv, tm_down, tn, nk_down, bn, tt = _tiles(is_prompt, n, t)
    if is_prompt:
        w_conv, off_conv = (w["w_b"], w["w_c"], w["w_h"]), (0, 0, 0)
        w_act, off_act = (w["w_g"], w["w_u"]), (0, 0)
        w_oa, w_oc, out_rows = w["w_oa"], w["w_oc"], (0, 0)
        w_qkv, w_down = w["w_qkv"], w["w_d"]
        d_ff = w_down.shape[0]
    else:
        d_ff = w["w_down"].shape[0]
        w_conv, off_conv = (w["w_in"],) * 3, (off_b, off_b + d_conv, off_b + 2 * d_conv)
        w_act, off_act = (w["w_gate_up"],) * 2, (0, d_ff)
        w_oa = w_oc = w["w_out"]
        out_rows = (0, 1)
        w_qkv, w_down = w["w_in"], w["w_down"]

    h = _norm_mod(x3, mod3, w["g_mix"], 0, 1, bn, tt)
    qkv, *wb_qkv = _matmul(h, w_qkv, off_b, tm, tn, emit)
    conv, conv_aux, *wb_conv = _gated_proj(h, w_conv, off_conv, d_conv, w["conv_w"], conv_state, t, tm_conv,
                                           tn, emit, "conv_gate")
    if is_prompt:
        attn = _prompt_attention(qkv, w["sinks"], n, t, attn_w, kv_dim)
    else:
        sink_lanes = jnp.repeat(w["sinks"].reshape(n_kv // 2, 1, 2 * GROUP), t, axis=2)
        attn = _sample_attention(qkv, cache_k.reshape(n, WINDOW, kv_dim), cache_v.reshape(n, WINDOW, kv_dim),
                                 sink_lanes, n, t, attn_w, kv_dim, bn=min(16, n))
    x1, *wb_out = _out_proj(attn, conv, w_oa, w_oc, out_rows, x2, mod3, 2, t, tm, tn, emit)

    h2 = _norm_mod(x1.reshape(n, t, d), mod3, w["g_ffn"], 3, 4, bn, tt)
    act, ffn_aux, *wb_act = _gated_proj(h2, w_act, off_act, d_ff, w["ffn_conv_w"], ffn_state, t, tm, tn, emit,
                                        "ffn_act")
    xo, *wb_down = _ffn_down(act, w_down, x1, mod3, 5, t, tm_down, tn, nk_down, emit)

    keep = min(WINDOW, t)
    kv_new = qkv.reshape(n, t, off_b)[:, t - keep:, attn_w:]
    k_new = kv_new[:, :, :kv_dim].reshape(n, keep, n_kv, HEAD_DIM)
    v_new = kv_new[:, :, kv_dim:].reshape(n, keep, n_kv, HEAD_DIM)
    if is_prompt:
        k_state, v_state = k_new, v_new
        conv_state_out = conv_aux[t // tm_conv - 1::t // tm_conv, SUBLANES - (CONV_K - 1):, :]
        ffn_state_out = ffn_aux[t // tm - 1::t // tm, SUBLANES - (CONV_K - 1):, :]
        return xo, (k_state, v_state, conv_state_out, ffn_state_out)
    k_state = jnp.concatenate([cache_k, k_new], axis=1)[:, t:]
    v_state = jnp.concatenate([cache_v, v_new], axis=1)[:, t:]
    wb = dict(w, w_qkv=wb_qkv[0], w_b=wb_conv[0], w_c=wb_conv[1], w_h=wb_conv[2], w_oa=wb_out[0], w_oc=wb_out[1],
              w_g=wb_act[0], w_u=wb_act[1], w_d=wb_down[0])
    return xo, (k_state, v_state, conv_aux, ffn_aux), wb


def kernel(x_prompt, x_sample, cache_k_win, cache_v_win, state_conv, state_ffn_conv, c_prompt, c_sample, w_mod, b_mod, g_mix, g_ffn, w_in, conv_w, sinks, w_out, w_gate_up, ffn_conv_w, w_down, g_final):
    depth = w_mod.shape[0]
    np_, tp, d = x_prompt.shape
    ns, ts, _ = x_sample.shape
    assert tp % BLOCK == 0 and cache_k_win.shape[2] == WINDOW

    n_c = np_ + ns
    n_c_pad = -(-n_c // SUBLANES) * SUBLANES
    c_all = jnp.concatenate([c_sample, c_prompt, jnp.zeros((n_c_pad - n_c, d), F32)], axis=0)

    xp, xs = x_prompt, x_sample
    outs_p, outs_s = [], []
    for l in range(depth):
        w = dict(g_mix=g_mix[l], g_ffn=g_ffn[l], conv_w=conv_w[l], sinks=sinks[l], ffn_conv_w=ffn_conv_w[l],
                 w_in=w_in[l], w_out=w_out[l], w_gate_up=w_gate_up[l], w_down=w_down[l])
        mod = _mod(c_all, w_mod[l], b_mod[l])
        xs2, st_s, wb = _layer(xs, (mod, 0), state_conv[l], state_ffn_conv[l], cache_k_win[l], cache_v_win[l],
                               w, False)
        xp2, st_p = _layer(xp, (mod, ns), None, None, None, None, wb, True)
        xp, xs = xp2.reshape(xp.shape), xs2.reshape(xs.shape)
        outs_p.append(st_p)
        outs_s.append(st_s)

    y_prompt = _final_norm(xp.reshape(np_ * tp, d), g_final).reshape(np_, tp, d)
    y_sample = _final_norm(xs.reshape(ns * ts, d), g_final).reshape(ns, ts, d)
    stack = lambda outs, k: jnp.stack([o[k] for o in outs])
    return (y_prompt, y_sample,
            stack(outs_p, 0), stack(outs_p, 1), stack(outs_p, 2), stack(outs_p, 3),
            stack(outs_s, 0), stack(outs_s, 1), stack(outs_s, 2), stack(outs_s, 3))
```

```python
import functools

import jax
import jax.numpy as jnp
from jax import lax
from jax.experimental import pallas as pl
from jax.experimental.pallas import tpu as pltpu

F32 = jnp.float32
BF16 = jnp.bfloat16

HEAD_DIM = 64
SCALE = HEAD_DIM ** -0.5
GROUP = 8
WINDOW = 128
BLOCK = 128
CONV_K = 3
N_MOD = 6
EPS = 1e-5
SUBLANES = 8
MIB = 1024 * 1024
VMEM_CAP = 56 * MIB


def _params(semantics, vmem_bytes):
    return pltpu.CompilerParams(dimension_semantics=semantics,
                                vmem_limit_bytes=int(min(max(vmem_bytes, 16 * MIB), VMEM_CAP)))


def _dot(a, b):
    return jnp.dot(a, b, preferred_element_type=F32)


def _dot_nt(a, b):
    return lax.dot_general(a, b, (((1,), (1,)), ((), ())), preferred_element_type=F32)


def _weight_ref(w_ref, wout_ref):
    if wout_ref is None:
        return w_ref
    wout_ref[...] = w_ref[...].astype(BF16)
    return wout_ref


def _row_spec(shape, index_map, n_row_tiles):
    if n_row_tiles == 1:
        return pl.BlockSpec(shape, index_map, pipeline_mode=pl.Buffered(1))
    return pl.BlockSpec(shape, index_map)


def _snake(nj):
    return lambda i, j: jnp.where(i % 2 == 0, j, nj - 1 - j)


def _nbytes(shape, dtype):
    n = jnp.dtype(dtype).itemsize
    for s in shape:
        n *= s
    return n


def _mod_body(c_ref, *refs):
    w_refs, b_ref, o_ref = refs[:N_MOD], refs[N_MOD], refs[N_MOD + 1]
    c = c_ref[...]
    a = (c * jax.nn.sigmoid(c)).astype(BF16)
    for k in range(N_MOD):
        o_ref[:, k, :] = _dot(a, w_refs[k][...].astype(BF16)) + b_ref[k:k + 1, :]


def _mod(c_all, w_mod, b_mod, tn=128):
    nc, d = c_all.shape
    nj = d // tn
    w_specs = [pl.BlockSpec((d, tn), functools.partial(lambda j, k: (0, k * nj + j), k=k)) for k in range(N_MOD)]
    vmem = 2 * (nc * d * 4 + N_MOD * d * tn * 4 + nc * SUBLANES * tn * 4) + N_MOD * d * tn * 2 + nc * d * 4
    return pl.pallas_call(
        _mod_body,
        grid=(nj,),
        in_specs=[pl.BlockSpec((nc, d), lambda j: (0, 0))] + w_specs + [pl.BlockSpec((N_MOD, tn), lambda j: (0, j))],
        out_specs=pl.BlockSpec((nc, N_MOD, tn), lambda j: (0, 0, j)),
        out_shape=jax.ShapeDtypeStruct((nc, N_MOD, d), F32),
        compiler_params=_params(("parallel",), vmem + 4 * MIB),
        name="mod",
    )(c_all, *([w_mod] * N_MOD), b_mod.reshape(N_MOD, d))


def _rms(x):
    return x * lax.rsqrt(jnp.mean(x * x, axis=-1, keepdims=True) + EPS)


def _norm_mod_body(x_ref, mod_ref, g_ref, o_ref, *, shift_idx, scale_idx):
    x = x_ref[...]
    gain = g_ref[...] * (1.0 + mod_ref[:, scale_idx:scale_idx + 1, :])
    h = _rms(x) * gain + mod_ref[:, shift_idx:shift_idx + 1, :]
    o_ref[...] = h.reshape(o_ref.shape).astype(o_ref.dtype)


def _norm_mod(x3, mod3, g, shift_idx, scale_idx, bn, tt):
    n, t, d = x3.shape
    mod, row0 = mod3
    assert row0 % bn == 0
    b0 = row0 // bn
    nt = t // tt
    vmem = 2 * (bn * tt * d * 4 + bn * tt * d * 2 + bn * SUBLANES * d * 4) + 3 * bn * tt * d * 4
    return pl.pallas_call(
        functools.partial(_norm_mod_body, shift_idx=shift_idx, scale_idx=scale_idx),
        grid=(n // bn, nt),
        in_specs=[pl.BlockSpec((bn, tt, d), lambda i, j: (i, j, 0)),
                  pl.BlockSpec((bn, N_MOD, d), lambda i, j: (b0 + i, 0, 0)),
                  pl.BlockSpec((1, d), lambda i, j: (0, 0))],
        out_specs=pl.BlockSpec((bn * tt, d), lambda i, j: (i * nt + j, 0)),
        out_shape=jax.ShapeDtypeStruct((n * t, d), BF16),
        compiler_params=_params(("parallel", "parallel"), vmem),
        name="norm_mod",
    )(x3, mod, g.reshape(1, d))


def _final_norm_body(x_ref, g_ref, o_ref):
    o_ref[...] = _rms(x_ref[...]) * g_ref[...]


def _final_norm(x2, g, tm=512):
    m, d = x2.shape
    tm = min(tm, m)
    return pl.pallas_call(
        _final_norm_body,
        grid=(m // tm,),
        in_specs=[pl.BlockSpec((tm, d), lambda i: (i, 0)),
                  pl.BlockSpec((1, d), lambda i: (0, 0))],
        out_specs=pl.BlockSpec((tm, d), lambda i: (i, 0)),
        out_shape=jax.ShapeDtypeStruct((m, d), F32),
        compiler_params=_params(("parallel",), 6 * tm * d * 4),
        name="final_norm",
    )(x2, g.reshape(1, d))


def _matmul_body(a_ref, w_ref, o_ref, wout_ref=None):
    o_ref[...] = _dot(a_ref[...], _weight_ref(w_ref, wout_ref)[...]).astype(o_ref.dtype)


def _matmul(a, w, n_cols, tm, tn, emit_w):
    m, k = a.shape
    n_i = m // tm
    assert n_i == 1 or not emit_w
    col = _snake(n_cols // tn)
    out_specs = [pl.BlockSpec((tm, tn), lambda i, j: (i, col(i, j)))]
    out_shape = [jax.ShapeDtypeStruct((m, n_cols), F32)]
    if emit_w:
        out_specs.append(pl.BlockSpec((k, tn), lambda i, j: (0, col(i, j))))
        out_shape.append(jax.ShapeDtypeStruct((k, n_cols), BF16))
    vmem = (min(n_i, 2) * tm * k * 2 + 2 * _nbytes((k, tn), w.dtype) + (2 * k * tn * 2 if emit_w else 0)
            + 3 * tm * tn * 4)
    return pl.pallas_call(
        _matmul_body,
        grid=(n_i, n_cols // tn),
        in_specs=[_row_spec((tm, k), lambda i, j: (i, 0), n_i),
                  pl.BlockSpec((k, tn), lambda i, j: (0, col(i, j)))],
        out_specs=out_specs, out_shape=out_shape,
        compiler_params=_params(("arbitrary", "arbitrary"), vmem + 4 * MIB),
        name="qkv",
    )(a, w)


def _silu(x):
    return x * jax.nn.sigmoid(x)


def _gated_body(*refs, n_w, emit_w, is_prompt, tiles_per_seq, nj):
    h_ref = refs[0]
    w_refs = refs[1:1 + n_w]
    cw_ref = refs[1 + n_w]
    pos = 2 + n_w
    st_ref = None
    if not is_prompt:
        st_ref = refs[pos]
        pos += 1
    o_ref, aux_ref = refs[pos], refs[pos + 1]
    pos += 2
    wout_refs = refs[pos:pos + n_w] if emit_w else (None,) * n_w
    carry_ref = refs[-1] if is_prompt else None

    ws = [_weight_ref(r, o) for r, o in zip(w_refs, wout_refs)]
    cw = cw_ref[...]
    tm, tn = o_ref.shape
    rm = tm // 2
    nb = rm // SUBLANES
    t = lax.broadcasted_iota(jnp.int32, (nb, SUBLANES, tn), 1)
    if is_prompt:
        i = pl.program_id(0)
        j = _snake(nj)(i, pl.program_id(1))

        @pl.when((i % tiles_per_seq) == 0)
        def _():
            carry_ref[j] = jnp.zeros((SUBLANES, tn), F32)

        tail = carry_ref[j]
    for r0 in (0, rm):
        h = h_ref[r0:r0 + rm, :]
        if n_w == 3:
            v = _dot(h, ws[1][...]) * _dot(h, ws[2][...])
        else:
            v = _dot(h, ws[0][...])
        v3 = v.reshape(nb, SUBLANES, tn)
        if is_prompt:
            ext = jnp.concatenate([tail[None], v3], axis=0)
            e1 = pltpu.roll(ext, 1, axis=1)
            e2 = pltpu.roll(ext, 2, axis=1)
            v1 = jnp.where(t == 0, e1[:nb], e1[1:])
            v2 = jnp.where(t < 2, e2[:nb], e2[1:])
            tail = v3[nb - 1]
        else:
            g0 = r0 // SUBLANES
            p0 = st_ref[g0:g0 + nb, 0:1, :]
            p1 = st_ref[g0:g0 + nb, 1:2, :]
            v1 = jnp.where(t == 0, p1, pltpu.roll(v3, 1, axis=1))
            v2 = jnp.where(t == 0, p0, jnp.where(t == 1, p1, pltpu.roll(v3, 2, axis=1)))
            aux_ref[g0:g0 + nb] = v3[:, SUBLANES - (CONV_K - 1):, :]
        y = (cw[0:1, :] * v2 + cw[1:2, :] * v1 + cw[2:3, :] * v3).reshape(rm, tn)
        if n_w == 3:
            out = _dot(h, ws[0][...]) * y
        else:
            out = _silu(y) * _dot(h, ws[1][...])
        o_ref[r0:r0 + rm, :] = out.astype(o_ref.dtype)
    if is_prompt:
        carry_ref[j] = tail
        aux_ref[0] = tail


def _gated_proj(h, ws, col_offsets, n_cols, conv_w, state, seq_len, tm, tn, emit_w, name):
    m, k = h.shape
    n_w = len(ws)
    n_i, nj = m // tm, pl.cdiv(n_cols, tn)
    assert all(o % tn == 0 for o in col_offsets)
    offs = [o // tn for o in col_offsets]
    is_prompt = state is None
    col = _snake(nj)
    spec = ((lambda shape, f: pl.BlockSpec(shape, lambda i, j: f(i, col(i, j)))) if is_prompt else
            (lambda shape, f: pl.BlockSpec(shape, lambda j, i: f(i, j))))
    w_specs = [spec((k, tn), functools.partial(lambda i, j, o: (0, o + j), o=o)) for o in offs]
    h_spec = _row_spec((tm, k), (lambda i, j: (i, 0)) if is_prompt else (lambda j, i: (i, 0)), n_i)
    in_specs = [h_spec] + w_specs + [spec((CONV_K, tn), lambda i, j: (0, j))]
    out_specs = [spec((tm, tn), lambda i, j: (i, j))]
    out_shape = [jax.ShapeDtypeStruct((m, n_cols), BF16)]
    args = (h,) + tuple(ws) + (conv_w,)
    if is_prompt:
        assert seq_len % tm == 0
        tiles_per_seq = seq_len // tm
        grid = (n_i, nj)
        out_specs.append(spec((1, SUBLANES, tn), lambda i, j: (i, 0, j)))
        out_shape.append(jax.ShapeDtypeStruct((n_i, SUBLANES, n_cols), F32))
        scratch = [pltpu.VMEM((nj, SUBLANES, tn), F32)]
    else:
        assert seq_len == SUBLANES and tm % SUBLANES == 0
        tiles_per_seq = None
        grid = (nj, n_i)
        nb = tm // SUBLANES
        in_specs.append(spec((nb, CONV_K - 1, tn), lambda i, j: (i, 0, j)))
        out_specs.append(spec((nb, CONV_K - 1, tn), lambda i, j: (i, 0, j)))
        out_shape.append(jax.ShapeDtypeStruct(state.shape, F32))
        scratch = []
        args += (state,)
    if emit_w:
        out_specs += [spec((k, tn), lambda i, j: (0, j))] * n_w
        out_shape += [jax.ShapeDtypeStruct((k, n_cols), BF16)] * n_w
    vmem = (min(n_i, 2) * tm * k * 2 + 2 * sum(_nbytes((k, tn), w.dtype) for w in ws)
            + (2 * n_w * k * tn * 2 if emit_w else 0) + 2 * tm * tn * 2 + 14 * (tm // 2) * tn * 4)
    return pl.pallas_call(
        functools.partial(_gated_body, n_w=n_w, emit_w=emit_w, is_prompt=is_prompt, tiles_per_seq=tiles_per_seq,
                          nj=nj),
        grid=grid, in_specs=in_specs, out_specs=out_specs, out_shape=out_shape,
        scratch_shapes=scratch, compiler_params=_params(("arbitrary", "arbitrary"), vmem + 4 * MIB), name=name,
    )(*args)


def _sink_softmax(s, mask, sink, axis):
    s = jnp.where(mask, s, -jnp.inf)
    mx = jnp.maximum(jnp.max(s, axis=axis, keepdims=True), sink)
    p = jnp.exp(s - mx)
    denom = jnp.sum(p, axis=axis, keepdims=True) + jnp.exp(sink - mx)
    return p * (1.0 / denom)


def _dot_tn(a, b):
    return lax.dot_general(a, b, (((0,), (0,)), ((), ())), preferred_element_type=F32)


def _block_diag_pair(tile, e):
    lo = lax.broadcasted_iota(jnp.int32, tile.shape, 1) < HEAD_DIM
    rolled = pltpu.roll(tile, HEAD_DIM, axis=1)
    zero = jnp.zeros_like(tile)
    if e == 0:
        top, bot = jnp.where(lo, tile, zero), jnp.where(lo, zero, rolled)
    else:
        top, bot = jnp.where(lo, rolled, zero), jnp.where(lo, zero, tile)
    return jnp.concatenate([top, bot], axis=0).astype(BF16)


def _prompt_attn_body(sink_ref, q_ref, kp_ref, kc_ref, vp_ref, vc_ref, o_ref, *, n_kv, blocks):
    step = pl.program_id(1)
    kall = jnp.concatenate([kp_ref[...], kc_ref[...]], axis=0)
    vall = jnp.concatenate([vp_ref[...], vc_ref[...]], axis=0)
    c = lax.broadcasted_iota(jnp.int32, (2 * BLOCK, BLOCK), 0)
    r = lax.broadcasted_iota(jnp.int32, (2 * BLOCK, BLOCK), 1)
    band = (c >= r) & (c <= r + WINDOW)
    first = lax.broadcasted_iota(jnp.int32, (2, 1, BLOCK), 0) == 0
    pairs_per_kv = GROUP // 2
    n_pairs = n_kv * pairs_per_kv
    for u in range(blocks):
        kk = kall[u * BLOCK:(u + 2) * BLOCK, :]
        vv = vall[u * BLOCK:(u + 2) * BLOCK, :]
        mask = (band & ((c >= BLOCK) | (step * blocks + u > 0)))[None]
        kbd, vbd = [], []
        for kh in range(n_kv):
            sl = slice((kh // 2) * 2 * HEAD_DIM, (kh // 2 + 1) * 2 * HEAD_DIM)
            kbd.append(_block_diag_pair(kk[:, sl], kh % 2))
            vbd.append(_block_diag_pair(vv[:, sl], kh % 2))
        rows = slice(u * BLOCK, (u + 1) * BLOCK)
        qp = [(q_ref[rows, i * 2 * HEAD_DIM:(i + 1) * 2 * HEAD_DIM] * SCALE).astype(BF16) for i in range(n_pairs)]
        s = [_dot_nt(kbd[i // pairs_per_kv], qp[i]).reshape(2, 2 * BLOCK, BLOCK) for i in range(n_pairs)]
        p = []
        for i in range(n_pairs):
            sink = jnp.where(first, sink_ref[2 * i], sink_ref[2 * i + 1])
            p.append(_sink_softmax(s[i], mask, sink, axis=1).astype(BF16).reshape(4 * BLOCK, BLOCK))
        for i in range(n_pairs):
            o_ref[rows, i * 2 * HEAD_DIM:(i + 1) * 2 * HEAD_DIM] = _dot_tn(p[i], vbd[i // pairs_per_kv]).astype(o_ref.dtype)


def _prompt_attention(qkv, sinks, n, t, attn_w, kv_dim):
    nb = t // BLOCK
    blocks = next(b for b in (4, 2, 1) if nb % b == 0)
    ns = nb // blocks
    kcol = attn_w // kv_dim
    cur = lambda col: (lambda i, b: (i * ns + b, col))
    prev = lambda col: (lambda i, b: (i * nb + jnp.maximum(b * blocks - 1, 0), col))
    return pl.pallas_call(
        functools.partial(_prompt_attn_body, n_kv=kv_dim // HEAD_DIM, blocks=blocks),
        grid=(n, ns),
        in_specs=[pl.BlockSpec(memory_space=pltpu.MemorySpace.SMEM),
                  pl.BlockSpec((blocks * BLOCK, attn_w), cur(0)),
                  pl.BlockSpec((BLOCK, kv_dim), prev(kcol)),
                  pl.BlockSpec((blocks * BLOCK, kv_dim), cur(kcol)),
                  pl.BlockSpec((BLOCK, kv_dim), prev(kcol + 1)),
                  pl.BlockSpec((blocks * BLOCK, kv_dim), cur(kcol + 1))],
        out_specs=pl.BlockSpec((blocks * BLOCK, attn_w), cur(0)),
        out_shape=jax.ShapeDtypeStruct((n * t, attn_w), BF16),
        compiler_params=_params(("parallel", "parallel"), 32 * MIB),
        name="prompt_attention",
    )(sinks, qkv, qkv, qkv, qkv, qkv)


def _sample_attn_body(sink_ref, q_ref, kn_ref, vn_ref, ck_ref, cv_ref, o_ref, *, bn, n_kv, t):
    keys = 2 * WINDOW
    pair_w = 2 * GROUP * HEAD_DIM
    n_pairs = n_kv // 2
    c = lax.broadcasted_iota(jnp.int32, (keys, 2 * HEAD_DIM), 0)
    tok = lax.broadcasted_iota(jnp.int32, (keys, 2 * HEAD_DIM), 1) % t
    mask = ((c < WINDOW) & (c >= tok)) | ((c >= WINDOW) & (c - WINDOW <= tok))
    pad = jnp.zeros((keys - WINDOW - t, n_kv * HEAD_DIM), F32)
    lo = lax.broadcasted_iota(jnp.int32, (t, 2 * HEAD_DIM), 1) < HEAD_DIM
    zero = jnp.zeros((t, 2 * HEAD_DIM), F32)
    work = [(n, pr) for n in range(bn) for pr in range(n_pairs)]
    lanes = lambda a, pr: a[:, pr * 2 * HEAD_DIM:(pr + 1) * 2 * HEAD_DIM]
    kk = [jnp.concatenate([ck_ref[n], kn_ref[n * t:(n + 1) * t, :], pad], axis=0).astype(BF16) for n in range(bn)]
    vv = [jnp.concatenate([cv_ref[n], vn_ref[n * t:(n + 1) * t, :], pad], axis=0).astype(BF16) for n in range(bn)]

    def q_block_diag(n, pr):
        pieces = []
        for hh in range(2 * GROUP):
            e, v = hh // GROUP, hh // 2
            x = q_ref[n * t:(n + 1) * t, pr * pair_w + v * 2 * HEAD_DIM:pr * pair_w + (v + 1) * 2 * HEAD_DIM] * SCALE
            if hh % 2 != e:
                x = pltpu.roll(x, HEAD_DIM, axis=1)
            pieces.append(jnp.where(lo, x, zero) if e == 0 else jnp.where(lo, zero, x))
        return jnp.concatenate(pieces, axis=0).astype(BF16)

    s = [_dot_nt(lanes(kk[n], pr), q_block_diag(n, pr)) for n, pr in work]
    p = [_sink_softmax(s[i], mask, sink_ref[pr], axis=0).astype(BF16) for i, (n, pr) in enumerate(work)]
    for i, (n, pr) in enumerate(work):
        o = _dot_tn(p[i], lanes(vv[n], pr))
        slabs = []
        for v in range(GROUP):
            halves = []
            for hh in (2 * v, 2 * v + 1):
                x = o[hh * t:(hh + 1) * t, :]
                halves.append(pltpu.roll(x, HEAD_DIM, axis=1) if hh % 2 != hh // GROUP else x)
            slabs.append(jnp.where(lo, halves[0], halves[1]))
        o_ref[n * t:(n + 1) * t, pr * pair_w:(pr + 1) * pair_w] = jnp.concatenate(slabs, axis=1)


def _sample_attention(qkv, cache_k, cache_v, sink_lanes, n, t, attn_w, kv_dim, bn=8):
    n_kv = kv_dim // HEAD_DIM
    assert n_kv % 2 == 0 and t == SUBLANES
    kcol = attn_w // kv_dim
    return pl.pallas_call(
        functools.partial(_sample_attn_body, bn=bn, n_kv=n_kv, t=t),
        grid=(n // bn,),
        in_specs=[pl.BlockSpec((n_kv // 2, 1, 2 * HEAD_DIM), lambda i: (0, 0, 0)),
                  pl.BlockSpec((bn * t, attn_w), lambda i: (i, 0)),
                  pl.BlockSpec((bn * t, kv_dim), lambda i: (i, kcol)),
                  pl.BlockSpec((bn * t, kv_dim), lambda i: (i, kcol + 1)),
                  pl.BlockSpec((bn, WINDOW, kv_dim), lambda i: (i, 0, 0)),
                  pl.BlockSpec((bn, WINDOW, kv_dim), lambda i: (i, 0, 0))],
        out_specs=pl.BlockSpec((bn * t, attn_w), lambda i: (i, 0)),
        out_shape=jax.ShapeDtypeStruct((n * t, attn_w), F32),
        compiler_params=_params(("parallel",), 24 * MIB),
        name="sample_attention",
    )(sink_lanes, qkv, qkv, qkv, cache_k, cache_v)


def _seq_blocks(seq_len, tm, mod3):
    mod, row0 = mod3
    if seq_len >= tm:
        assert seq_len % tm == 0
        tiles = seq_len // tm
        return mod, 1, (lambda i: row0 + i // tiles)
    assert tm % seq_len == 0 and row0 % (tm // seq_len) == 0
    b0 = row0 // (tm // seq_len)
    return mod, tm // seq_len, (lambda i: b0 + i)


def _gated_residual(x, acc, mod_ref, gate_idx):
    tm, tn = acc.shape
    bn = mod_ref.shape[0]
    gate = mod_ref[:, gate_idx:gate_idx + 1, :]
    y = x.reshape(bn, tm // bn, tn) + gate * acc.reshape(bn, tm // bn, tn)
    return y.reshape(tm, tn)


def _out_proj_body(a_ref, c_ref, wa_ref, wc_ref, x_ref, mod_ref, o_ref, woa_ref=None, woc_ref=None, *, gate_idx):
    acc = (_dot(a_ref[...].astype(BF16), _weight_ref(wa_ref, woa_ref)[...])
           + _dot(c_ref[...], _weight_ref(wc_ref, woc_ref)[...]))
    o_ref[...] = _gated_residual(x_ref[...], acc, mod_ref, gate_idx)


def _out_proj(attn, conv, wa, wc, row_blocks, x, mod3, gate_idx, seq_len, tm, tn, emit_w):
    m, ka = attn.shape
    assert conv.shape[1] == ka
    d = x.shape[1]
    n_i = m // tm
    assert n_i == 1 or not emit_w
    mod, bn, seq_of = _seq_blocks(seq_len, tm, mod3)
    ra, rc = row_blocks
    col = _snake(d // tn)
    out_specs = [pl.BlockSpec((tm, tn), lambda i, j: (i, col(i, j)))]
    out_shape = [jax.ShapeDtypeStruct((m, d), F32)]
    if emit_w:
        out_specs += [pl.BlockSpec((ka, tn), lambda i, j: (0, col(i, j)))] * 2
        out_shape += [jax.ShapeDtypeStruct((ka, d), BF16)] * 2
    vmem = (min(n_i, 2) * (_nbytes((tm, ka), attn.dtype) + tm * ka * 2) + tm * ka * 2
            + 4 * _nbytes((ka, tn), wa.dtype) + (4 * ka * tn * 2 if emit_w else 0) + 7 * tm * tn * 4)
    return pl.pallas_call(
        functools.partial(_out_proj_body, gate_idx=gate_idx),
        grid=(n_i, d // tn),
        in_specs=[_row_spec((tm, ka), lambda i, j: (i, 0), n_i),
                  _row_spec((tm, ka), lambda i, j: (i, 0), n_i),
                  pl.BlockSpec((ka, tn), lambda i, j: (ra, col(i, j))),
                  pl.BlockSpec((ka, tn), lambda i, j: (rc, col(i, j))),
                  pl.BlockSpec((tm, tn), lambda i, j: (i, col(i, j))),
                  pl.BlockSpec((bn, N_MOD, tn), lambda i, j: (seq_of(i), 0, col(i, j)))],
        out_specs=out_specs, out_shape=out_shape,
        compiler_params=_params(("arbitrary", "arbitrary"), vmem + 4 * MIB),
        name="out_proj",
    )(attn, conv, wa, wc, x, mod)


def _ffn_down_body(a_ref, w_ref, x_ref, mod_ref, o_ref, wout_ref=None, *, gate_idx, nk):
    if nk == 1:
        acc = _dot(a_ref[...], _weight_ref(w_ref, wout_ref)[...])
        o_ref[...] = _gated_residual(x_ref[...], acc, mod_ref, gate_idx)
    else:
        kk = pl.program_id(2)

        @pl.when(kk == 0)
        def _():
            o_ref[...] = x_ref[...]

        tk = w_ref.shape[0]
        a = a_ref[:, pl.ds(pl.multiple_of(kk * tk, 128), tk)]
        acc = _dot(a, _weight_ref(w_ref, wout_ref)[...])
        o_ref[...] = _gated_residual(o_ref[...], acc, mod_ref, gate_idx)


def _ffn_down(act, w_down, x, mod3, gate_idx, seq_len, tm, tn, nk, emit_w):
    m, k = act.shape
    d = x.shape[1]
    tk = k // nk
    assert tk * nk == k and (m == tm or not emit_w)
    mod, bn, seq_of = _seq_blocks(seq_len, tm, mod3)
    col = _snake(d // tn)
    out_specs = [pl.BlockSpec((tm, tn), lambda i, j, kk: (i, col(i, j)))]
    out_shape = [jax.ShapeDtypeStruct((m, d), F32)]
    if emit_w:
        out_specs.append(pl.BlockSpec((tk, tn), lambda i, j, kk: (kk, col(i, j))))
        out_shape.append(jax.ShapeDtypeStruct((k, d), BF16))
    assert tk % 128 == 0
    n_i = m // tm
    vmem = (min(n_i, 2) * tm * k * 2 + 2 * _nbytes((tk, tn), w_down.dtype) + (2 * tk * tn * 2 if emit_w else 0)
            + 7 * tm * tn * 4)
    return pl.pallas_call(
        functools.partial(_ffn_down_body, gate_idx=gate_idx, nk=nk),
        grid=(n_i, d // tn, nk),
        in_specs=[_row_spec((tm, k), lambda i, j, kk: (i, 0), n_i),
                  pl.BlockSpec((tk, tn), lambda i, j, kk: (kk, col(i, j))),
                  pl.BlockSpec((tm, tn), lambda i, j, kk: (i, col(i, j))),
                  pl.BlockSpec((bn, N_MOD, tn), lambda i, j, kk: (seq_of(i), 0, col(i, j)))],
        out_specs=out_specs, out_shape=out_shape,
        compiler_params=_params(("arbitrary", "arbitrary", "arbitrary"), vmem + 4 * MIB),
        name="ffn_down",
    )(act, w_down, x, mod)


ROW_TILE = 1024
HALF_ROW_TILE = 512
COL_TILE_BF16 = 512
COL_TILE_F32 = 256
NORM_ROWS = 512
NORM_SEQS = 32


def _tiles(is_prompt, n, t):
    if is_prompt:
        tm = min(ROW_TILE, t)
        half = min(HALF_ROW_TILE, tm)
        return tm, half, half, COL_TILE_BF16, 1, 1, min(NORM_ROWS, t)
    assert t == SUBLANES
    tm = min(ROW_TILE, n * t)
    return tm, tm, tm, COL_TILE_F32, 2, min(NORM_SEQS, n), t


def _layer(x3, mod3, conv_state, ffn_state, cache_k, cache_v, w, is_prompt):
    n, t, d = x3.shape
    m = n * t
    attn_w = d // 2
    d_conv = d - attn_w
    n_q = attn_w // HEAD_DIM
    n_kv = max(1, n_q // GROUP)
    kv_dim = n_kv * HEAD_DIM
    off_b = attn_w + 2 * kv_dim
    x2 = x3.reshape(m, d)
    emit = not is_prompt

    tm, tm_conv, tm_down, tn, nk_down, bn, tt = _tiles(is_prompt, n, t)
    if is_prompt:
        w_conv, off_conv = (w["w_b"], w["w_c"], w["w_h"]), (0, 0, 0)
        w_act, off_act = (w["w_g"], w["w_u"]), (0, 0)
        w_oa, w_oc, out_rows = w["w_oa"], w["w_oc"], (0, 0)
        w_qkv, w_down = w["w_qkv"], w["w_d"]
        d_ff = w_down.shape[0]
    else:
        d_ff = w["w_down"].shape[0]
        w_conv, off_conv = (w["w_in"],) * 3, (off_b, off_b + d_conv, off_b + 2 * d_conv)
        w_act, off_act = (w["w_gate_up"],) * 2, (0, d_ff)
        w_oa = w_oc = w["w_out"]
        out_rows = (0, 1)
        w_qkv, w_down = w["w_in"], w["w_down"]

    h = _norm_mod(x3, mod3, w["g_mix"], 0, 1, bn, tt)
    qkv, *wb_qkv = _matmul(h, w_qkv, off_b, tm, tn, emit)
    conv, conv_aux, *wb_conv = _gated_proj(h, w_conv, off_conv, d_conv, w["conv_w"], conv_state, t, tm_conv,
                                           tn, emit, "conv_gate")
    if is_prompt:
        attn = _prompt_attention(qkv, w["sinks"], n, t, attn_w, kv_dim)
    else:
        sink_lanes = jnp.repeat(w["sinks"].reshape(n_kv // 2, 1, 2 * GROUP), t, axis=2)
        attn = _sample_attention(qkv, cache_k.reshape(n, WINDOW, kv_dim), cache_v.reshape(n, WINDOW, kv_dim),
                                 sink_lanes, n, t, attn_w, kv_dim, bn=min(16, n))
    x1, *wb_out = _out_proj(attn, conv, w_oa, w_oc, out_rows, x2, mod3, 2, t, tm, tn, emit)

    h2 = _norm_mod(x1.reshape(n, t, d), mod3, w["g_ffn"], 3, 4, bn, tt)
    act, ffn_aux, *wb_act = _gated_proj(h2, w_act, off_act, d_ff, w["ffn_conv_w"], ffn_state, t, tm, tn, emit,
                                        "ffn_act")
    xo, *wb_down = _ffn_down(act, w_down, x1, mod3, 5, t, tm_down, tn, nk_down, emit)

    keep = min(WINDOW, t)
    kv_new = qkv.reshape(n, t, off_b)[:, t - keep:, attn_w:]
    k_new = kv_new[:, :, :kv_dim].reshape(n, keep, n_kv, HEAD_DIM)
    v_new = kv_new[:, :, kv_dim:].reshape(n, keep, n_kv, HEAD_DIM)
    if is_prompt:
        k_state, v_state = k_new, v_new
        conv_state_out = conv_aux[t // tm_conv - 1::t // tm_conv, SUBLANES - (CONV_K - 1):, :]
        ffn_state_out = ffn_aux[t // tm - 1::t // tm, SUBLANES - (CONV_K - 1):, :]
        return xo, (k_state, v_state, conv_state_out, ffn_state_out)
    k_state = jnp.concatenate([cache_k, k_new], axis=1)[:, t:]
    v_state = jnp.concatenate([cache_v, v_new], axis=1)[:, t:]
    wb = dict(w, w_qkv=wb_qkv[0], w_b=wb_conv[0], w_c=wb_conv[1], w_h=wb_conv[2], w_oa=wb_out[0], w_oc=wb_out[1],
              w_g=wb_act[0], w_u=wb_act[1], w_d=wb_down[0])
    return xo, (k_state, v_state, conv_aux, ffn_aux), wb


def kernel(x_prompt, x_sample, cache_k_win, cache_v_win, state_conv, state_ffn_conv, c_prompt, c_sample, w_mod, b_mod, g_mix, g_ffn, w_in, conv_w, sinks, w_out, w_gate_up, ffn_conv_w, w_down, g_final):
    depth = w_mod.shape[0]
    np_, tp, d = x_prompt.shape
    ns, ts, _ = x_sample.shape
    assert tp % BLOCK == 0 and cache_k_win.shape[2] == WINDOW

    n_c = np_ + ns
    n_c_pad = -(-n_c // SUBLANES) * SUBLANES
    c_all = jnp.concatenate([c_sample, c_prompt, jnp.zeros((n_c_pad - n_c, d), F32)], axis=0)

    xp, xs = x_prompt, x_sample
    outs_p, outs_s = [], []
    for l in range(depth):
        w = dict(g_mix=g_mix[l], g_ffn=g_ffn[l], conv_w=conv_w[l], sinks=sinks[l], ffn_conv_w=ffn_conv_w[l],
                 w_in=w_in[l], w_out=w_out[l], w_gate_up=w_gate_up[l], w_down=w_down[l])
        mod = _mod(c_all, w_mod[l], b_mod[l])
        xs2, st_s, wb = _layer(xs, (mod, 0), state_conv[l], state_ffn_conv[l], cache_k_win[l], cache_v_win[l],
                               w, False)
        xp2, st_p = _layer(xp, (mod, ns), None, None, None, None, wb, True)
        xp, xs = xp2.reshape(xp.shape), xs2.reshape(xs.shape)
        outs_p.append(st_p)
        outs_s.append(st_s)

    y_prompt = _final_norm(xp.reshape(np_ * tp, d), g_final).reshape(np_, tp, d)
    y_sample = _final_norm(xs.reshape(ns * ts, d), g_final).reshape(ns, ts, d)
    stack = lambda outs, k: jnp.stack([o[k] for o in outs])
    return (y_prompt, y_sample,
            stack(outs_p, 0), stack(outs_p, 1), stack(outs_p, 2), stack(outs_p, 3),
            stack(outs_s, 0), stack(outs_s, 1), stack(outs_s, 2), stack(outs_s, 3))
```

```python
import functools

import jax
import jax.numpy as jnp
from jax import lax
from jax.experimental import pallas as pl
from jax.experimental.pallas import tpu as pltpu

F32 = jnp.float32
BF16 = jnp.bfloat16

HEAD_DIM = 64
SCALE = HEAD_DIM ** -0.5
GROUP = 8
WINDOW = 128
BLOCK = 128
CONV_K = 3
N_MOD = 6
EPS = 1e-5
SUBLANES = 8
MIB = 1024 * 1024
VMEM_CAP = 56 * MIB


def _params(semantics, vmem_bytes):
    return pltpu.CompilerParams(dimension_semantics=semantics,
                                vmem_limit_bytes=int(min(max(vmem_bytes, 16 * MIB), VMEM_CAP)))


def _dot(a, b):
    return jnp.dot(a, b, preferred_element_type=F32)


def _dot_nt(a, b):
    return lax.dot_general(a, b, (((1,), (1,)), ((), ())), preferred_element_type=F32)


def _weight_ref(w_ref, wout_ref):
    if wout_ref is None:
        return w_ref
    wout_ref[...] = w_ref[...].astype(BF16)
    return wout_ref


def _row_spec(shape, index_map, n_row_tiles):
    if n_row_tiles == 1:
        return pl.BlockSpec(shape, index_map, pipeline_mode=pl.Buffered(1))
    return pl.BlockSpec(shape, index_map)


def _snake(nj):
    return lambda i, j: jnp.where(i % 2 == 0, j, nj - 1 - j)


def _nbytes(shape, dtype):
    n = jnp.dtype(dtype).itemsize
    for s in shape:
        n *= s
    return n


def _mod_body(c_ref, *refs):
    w_refs, b_ref, o_ref = refs[:N_MOD], refs[N_MOD], refs[N_MOD + 1]
    c = c_ref[...]
    a = (c * jax.nn.sigmoid(c)).astype(BF16)
    for k in range(N_MOD):
        o_ref[:, k, :] = _dot(a, w_refs[k][...].astype(BF16)) + b_ref[k:k + 1, :]


def _mod(c_all, w_mod, b_mod, tn=128):
    nc, d = c_all.shape
    nj = d // tn
    w_specs = [pl.BlockSpec((d, tn), functools.partial(lambda j, k: (0, k * nj + j), k=k)) for k in range(N_MOD)]
    vmem = 2 * (nc * d * 4 + N_MOD * d * tn * 4 + nc * SUBLANES * tn * 4) + N_MOD * d * tn * 2 + nc * d * 4
    return pl.pallas_call(
        _mod_body,
        grid=(nj,),
        in_specs=[pl.BlockSpec((nc, d), lambda j: (0, 0))] + w_specs + [pl.BlockSpec((N_MOD, tn), lambda j: (0, j))],
        out_specs=pl.BlockSpec((nc, N_MOD, tn), lambda j: (0, 0, j)),
        out_shape=jax.ShapeDtypeStruct((nc, N_MOD, d), F32),
        compiler_params=_params(("parallel",), vmem + 4 * MIB),
        name="mod",
    )(c_all, *([w_mod] * N_MOD), b_mod.reshape(N_MOD, d))


def _rms(x):
    return x * lax.rsqrt(jnp.mean(x * x, axis=-1, keepdims=True) + EPS)


def _norm_mod_body(x_ref, mod_ref, g_ref, o_ref, *, shift_idx, scale_idx):
    x = x_ref[...]
    gain = g_ref[...] * (1.0 + mod_ref[:, scale_idx:scale_idx + 1, :])
    h = _rms(x) * gain + mod_ref[:, shift_idx:shift_idx + 1, :]
    o_ref[...] = h.reshape(o_ref.shape).astype(o_ref.dtype)


def _norm_mod(x3, mod3, g, shift_idx, scale_idx, bn, tt):
    n, t, d = x3.shape
    mod, row0 = mod3
    assert row0 % bn == 0
    b0 = row0 // bn
    nt = t // tt
    vmem = 2 * (bn * tt * d * 4 + bn * tt * d * 2 + bn * SUBLANES * d * 4) + 3 * bn * tt * d * 4
    return pl.pallas_call(
        functools.partial(_norm_mod_body, shift_idx=shift_idx, scale_idx=scale_idx),
        grid=(n // bn, nt),
        in_specs=[pl.BlockSpec((bn, tt, d), lambda i, j: (i, j, 0)),
                  pl.BlockSpec((bn, N_MOD, d), lambda i, j: (b0 + i, 0, 0)),
                  pl.BlockSpec((1, d), lambda i, j: (0, 0))],
        out_specs=pl.BlockSpec((bn * tt, d), lambda i, j: (i * nt + j, 0)),
        out_shape=jax.ShapeDtypeStruct((n * t, d), BF16),
        compiler_params=_params(("parallel", "parallel"), vmem),
        name="norm_mod",
    )(x3, mod, g.reshape(1, d))


def _final_norm_body(x_ref, g_ref, o_ref):
    o_ref[...] = _rms(x_ref[...]) * g_ref[...]


def _final_norm(x2, g, tm=512):
    m, d = x2.shape
    tm = min(tm, m)
    return pl.pallas_call(
        _final_norm_body,
        grid=(m // tm,),
        in_specs=[pl.BlockSpec((tm, d), lambda i: (i, 0)),
                  pl.BlockSpec((1, d), lambda i: (0, 0))],
        out_specs=pl.BlockSpec((tm, d), lambda i: (i, 0)),
        out_shape=jax.ShapeDtypeStruct((m, d), F32),
        compiler_params=_params(("parallel",), 6 * tm * d * 4),
        name="final_norm",
    )(x2, g.reshape(1, d))


def _matmul_body(a_ref, w_ref, o_ref, wout_ref=None):
    o_ref[...] = _dot(a_ref[...], _weight_ref(w_ref, wout_ref)[...]).astype(o_ref.dtype)


def _matmul(a, w, n_cols, tm, tn, emit_w):
    m, k = a.shape
    n_i = m // tm
    assert n_i == 1 or not emit_w
    col = _snake(n_cols // tn)
    out_specs = [pl.BlockSpec((tm, tn), lambda i, j: (i, col(i, j)))]
    out_shape = [jax.ShapeDtypeStruct((m, n_cols), F32)]
    if emit_w:
        out_specs.append(pl.BlockSpec((k, tn), lambda i, j: (0, col(i, j))))
        out_shape.append(jax.ShapeDtypeStruct((k, n_cols), BF16))
    vmem = (min(n_i, 2) * tm * k * 2 + 2 * _nbytes((k, tn), w.dtype) + (2 * k * tn * 2 if emit_w else 0)
            + 3 * tm * tn * 4)
    return pl.pallas_call(
        _matmul_body,
        grid=(n_i, n_cols // tn),
        in_specs=[_row_spec((tm, k), lambda i, j: (i, 0), n_i),
                  pl.BlockSpec((k, tn), lambda i, j: (0, col(i, j)))],
        out_specs=out_specs, out_shape=out_shape,
        compiler_params=_params(("arbitrary", "arbitrary"), vmem + 4 * MIB),
        name="qkv",
    )(a, w)


def _silu(x):
    return x * jax.nn.sigmoid(x)


def _gated_body(*refs, n_w, emit_w, is_prompt, tiles_per_seq):
    h_ref = refs[0]
    w_refs = refs[1:1 + n_w]
    cw_ref = refs[1 + n_w]
    pos = 2 + n_w
    st_ref = None
    if not is_prompt:
        st_ref = refs[pos]
        pos += 1
    o_ref, aux_ref = refs[pos], refs[pos + 1]
    pos += 2
    wout_refs = refs[pos:pos + n_w] if emit_w else (None,) * n_w
    carry_ref = refs[-1] if is_prompt else None

    ws = [_weight_ref(r, o) for r, o in zip(w_refs, wout_refs)]
    cw = cw_ref[...]
    tm, tn = o_ref.shape
    rm = tm // 2
    nb = rm // SUBLANES
    t = lax.broadcasted_iota(jnp.int32, (nb, SUBLANES, tn), 1)
    if is_prompt:
        i, j = pl.program_id(0), pl.program_id(1)

        @pl.when((i % tiles_per_seq) == 0)
        def _():
            carry_ref[j] = jnp.zeros((SUBLANES, tn), F32)

        tail = carry_ref[j]
    for r0 in (0, rm):
        h = h_ref[r0:r0 + rm, :]
        if n_w == 3:
            v = _dot(h, ws[1][...]) * _dot(h, ws[2][...])
        else:
            v = _dot(h, ws[0][...])
        v3 = v.reshape(nb, SUBLANES, tn)
        if is_prompt:
            ext = jnp.concatenate([tail[None], v3], axis=0)
            e1 = pltpu.roll(ext, 1, axis=1)
            e2 = pltpu.roll(ext, 2, axis=1)
            v1 = jnp.where(t == 0, e1[:nb], e1[1:])
            v2 = jnp.where(t < 2, e2[:nb], e2[1:])
            tail = v3[nb - 1]
        else:
            g0 = r0 // SUBLANES
            p0 = st_ref[g0:g0 + nb, 0:1, :]
            p1 = st_ref[g0:g0 + nb, 1:2, :]
            v1 = jnp.where(t == 0, p1, pltpu.roll(v3, 1, axis=1))
            v2 = jnp.where(t == 0, p0, jnp.where(t == 1, p1, pltpu.roll(v3, 2, axis=1)))
            aux_ref[g0:g0 + nb] = v3[:, SUBLANES - (CONV_K - 1):, :]
        y = (cw[0:1, :] * v2 + cw[1:2, :] * v1 + cw[2:3, :] * v3).reshape(rm, tn)
        if n_w == 3:
            out = _dot(h, ws[0][...]) * y
        else:
            out = _silu(y) * _dot(h, ws[1][...])
        o_ref[r0:r0 + rm, :] = out.astype(o_ref.dtype)
    if is_prompt:
        carry_ref[j] = tail
        aux_ref[0] = tail


def _gated_proj(h, ws, col_offsets, n_cols, conv_w, state, seq_len, tm, tn, emit_w, name):
    m, k = h.shape
    n_w = len(ws)
    n_i, nj = m // tm, pl.cdiv(n_cols, tn)
    assert all(o % tn == 0 for o in col_offsets)
    offs = [o // tn for o in col_offsets]
    is_prompt = state is None
    spec = ((lambda shape, f: pl.BlockSpec(shape, f)) if is_prompt else
            (lambda shape, f: pl.BlockSpec(shape, lambda j, i: f(i, j))))
    w_specs = [spec((k, tn), functools.partial(lambda i, j, o: (0, o + j), o=o)) for o in offs]
    h_spec = _row_spec((tm, k), (lambda i, j: (i, 0)) if is_prompt else (lambda j, i: (i, 0)), n_i)
    in_specs = [h_spec] + w_specs + [spec((CONV_K, tn), lambda i, j: (0, j))]
    out_specs = [spec((tm, tn), lambda i, j: (i, j))]
    out_shape = [jax.ShapeDtypeStruct((m, n_cols), BF16)]
    args = (h,) + tuple(ws) + (conv_w,)
    if is_prompt:
        assert seq_len % tm == 0
        tiles_per_seq = seq_len // tm
        grid = (n_i, nj)
        out_specs.append(spec((1, SUBLANES, tn), lambda i, j: (i, 0, j)))
        out_shape.append(jax.ShapeDtypeStruct((n_i, SUBLANES, n_cols), F32))
        scratch = [pltpu.VMEM((nj, SUBLANES, tn), F32)]
    else:
        assert seq_len == SUBLANES and tm % SUBLANES == 0
        tiles_per_seq = None
        grid = (nj, n_i)
        nb = tm // SUBLANES
        in_specs.append(spec((nb, CONV_K - 1, tn), lambda i, j: (i, 0, j)))
        out_specs.append(spec((nb, CONV_K - 1, tn), lambda i, j: (i, 0, j)))
        out_shape.append(jax.ShapeDtypeStruct(state.shape, F32))
        scratch = []
        args += (state,)
    if emit_w:
        out_specs += [spec((k, tn), lambda i, j: (0, j))] * n_w
        out_shape += [jax.ShapeDtypeStruct((k, n_cols), BF16)] * n_w
    vmem = (min(n_i, 2) * tm * k * 2 + 2 * sum(_nbytes((k, tn), w.dtype) for w in ws)
            + (2 * n_w * k * tn * 2 if emit_w else 0) + 2 * tm * tn * 2 + 14 * (tm // 2) * tn * 4)
    return pl.pallas_call(
        functools.partial(_gated_body, n_w=n_w, emit_w=emit_w, is_prompt=is_prompt, tiles_per_seq=tiles_per_seq),
        grid=grid, in_specs=in_specs, out_specs=out_specs, out_shape=out_shape,
        scratch_shapes=scratch, compiler_params=_params(("arbitrary", "arbitrary"), vmem + 4 * MIB), name=name,
    )(*args)


def _sink_softmax(s, mask, sink, axis):
    s = jnp.where(mask, s, -jnp.inf)
    mx = jnp.maximum(jnp.max(s, axis=axis, keepdims=True), sink)
    p = jnp.exp(s - mx)
    denom = jnp.sum(p, axis=axis, keepdims=True) + jnp.exp(sink - mx)
    return p * (1.0 / denom)


def _dot_tn(a, b):
    return lax.dot_general(a, b, (((0,), (0,)), ((), ())), preferred_element_type=F32)


def _block_diag_pair(tile, e):
    lo = lax.broadcasted_iota(jnp.int32, tile.shape, 1) < HEAD_DIM
    rolled = pltpu.roll(tile, HEAD_DIM, axis=1)
    zero = jnp.zeros_like(tile)
    if e == 0:
        top, bot = jnp.where(lo, tile, zero), jnp.where(lo, zero, rolled)
    else:
        top, bot = jnp.where(lo, rolled, zero), jnp.where(lo, zero, tile)
    return jnp.concatenate([top, bot], axis=0).astype(BF16)


def _prompt_attn_body(sink_ref, q_ref, kp_ref, kc_ref, vp_ref, vc_ref, o_ref, *, n_kv, blocks):
    step = pl.program_id(1)
    kall = jnp.concatenate([kp_ref[...], kc_ref[...]], axis=0)
    vall = jnp.concatenate([vp_ref[...], vc_ref[...]], axis=0)
    c = lax.broadcasted_iota(jnp.int32, (2 * BLOCK, BLOCK), 0)
    r = lax.broadcasted_iota(jnp.int32, (2 * BLOCK, BLOCK), 1)
    band = (c >= r) & (c <= r + WINDOW)
    first = lax.broadcasted_iota(jnp.int32, (2, 1, BLOCK), 0) == 0
    pairs_per_kv = GROUP // 2
    n_pairs = n_kv * pairs_per_kv
    for u in range(blocks):
        kk = kall[u * BLOCK:(u + 2) * BLOCK, :]
        vv = vall[u * BLOCK:(u + 2) * BLOCK, :]
        mask = (band & ((c >= BLOCK) | (step * blocks + u > 0)))[None]
        kbd, vbd = [], []
        for kh in range(n_kv):
            sl = slice((kh // 2) * 2 * HEAD_DIM, (kh // 2 + 1) * 2 * HEAD_DIM)
            kbd.append(_block_diag_pair(kk[:, sl], kh % 2))
            vbd.append(_block_diag_pair(vv[:, sl], kh % 2))
        rows = slice(u * BLOCK, (u + 1) * BLOCK)
        qp = [(q_ref[rows, i * 2 * HEAD_DIM:(i + 1) * 2 * HEAD_DIM] * SCALE).astype(BF16) for i in range(n_pairs)]
        s = [_dot_nt(kbd[i // pairs_per_kv], qp[i]).reshape(2, 2 * BLOCK, BLOCK) for i in range(n_pairs)]
        p = []
        for i in range(n_pairs):
            sink = jnp.where(first, sink_ref[2 * i], sink_ref[2 * i + 1])
            p.append(_sink_softmax(s[i], mask, sink, axis=1).astype(BF16).reshape(4 * BLOCK, BLOCK))
        for i in range(n_pairs):
            o_ref[rows, i * 2 * HEAD_DIM:(i + 1) * 2 * HEAD_DIM] = _dot_tn(p[i], vbd[i // pairs_per_kv]).astype(o_ref.dtype)


def _prompt_attention(qkv, sinks, n, t, attn_w, kv_dim):
    nb = t // BLOCK
    blocks = next(b for b in (4, 2, 1) if nb % b == 0)
    ns = nb // blocks
    kcol = attn_w // kv_dim
    cur = lambda col: (lambda i, b: (i * ns + b, col))
    prev = lambda col: (lambda i, b: (i * nb + jnp.maximum(b * blocks - 1, 0), col))
    return pl.pallas_call(
        functools.partial(_prompt_attn_body, n_kv=kv_dim // HEAD_DIM, blocks=blocks),
        grid=(n, ns),
        in_specs=[pl.BlockSpec(memory_space=pltpu.MemorySpace.SMEM),
                  pl.BlockSpec((blocks * BLOCK, attn_w), cur(0)),
                  pl.BlockSpec((BLOCK, kv_dim), prev(kcol)),
                  pl.BlockSpec((blocks * BLOCK, kv_dim), cur(kcol)),
                  pl.BlockSpec((BLOCK, kv_dim), prev(kcol + 1)),
                  pl.BlockSpec((blocks * BLOCK, kv_dim), cur(kcol + 1))],
        out_specs=pl.BlockSpec((blocks * BLOCK, attn_w), cur(0)),
        out_shape=jax.ShapeDtypeStruct((n * t, attn_w), BF16),
        compiler_params=_params(("parallel", "parallel"), 32 * MIB),
        name="prompt_attention",
    )(sinks, qkv, qkv, qkv, qkv, qkv)


def _sample_attn_body(sink_ref, q_ref, kn_ref, vn_ref, ck_ref, cv_ref, o_ref, *, bn, n_kv, t):
    keys = 2 * WINDOW
    pair_w = 2 * GROUP * HEAD_DIM
    n_pairs = n_kv // 2
    c = lax.broadcasted_iota(jnp.int32, (keys, 2 * HEAD_DIM), 0)
    tok = lax.broadcasted_iota(jnp.int32, (keys, 2 * HEAD_DIM), 1) % t
    mask = ((c < WINDOW) & (c >= tok)) | ((c >= WINDOW) & (c - WINDOW <= tok))
    pad = jnp.zeros((keys - WINDOW - t, n_kv * HEAD_DIM), F32)
    lo = lax.broadcasted_iota(jnp.int32, (t, 2 * HEAD_DIM), 1) < HEAD_DIM
    zero = jnp.zeros((t, 2 * HEAD_DIM), F32)
    work = [(n, pr) for n in range(bn) for pr in range(n_pairs)]
    lanes = lambda a, pr: a[:, pr * 2 * HEAD_DIM:(pr + 1) * 2 * HEAD_DIM]
    kk = [jnp.concatenate([ck_ref[n], kn_ref[n * t:(n + 1) * t, :], pad], axis=0).astype(BF16) for n in range(bn)]
    vv = [jnp.concatenate([cv_ref[n], vn_ref[n * t:(n + 1) * t, :], pad], axis=0).astype(BF16) for n in range(bn)]

    def q_block_diag(n, pr):
        pieces = []
        for hh in range(2 * GROUP):
            e, v = hh // GROUP, hh // 2
            x = q_ref[n * t:(n + 1) * t, pr * pair_w + v * 2 * HEAD_DIM:pr * pair_w + (v + 1) * 2 * HEAD_DIM] * SCALE
            if hh % 2 != e:
                x = pltpu.roll(x, HEAD_DIM, axis=1)
            pieces.append(jnp.where(lo, x, zero) if e == 0 else jnp.where(lo, zero, x))
        return jnp.concatenate(pieces, axis=0).astype(BF16)

    s = [_dot_nt(lanes(kk[n], pr), q_block_diag(n, pr)) for n, pr in work]
    p = [_sink_softmax(s[i], mask, sink_ref[pr], axis=0).astype(BF16) for i, (n, pr) in enumerate(work)]
    for i, (n, pr) in enumerate(work):
        o = _dot_tn(p[i], lanes(vv[n], pr))
        slabs = []
        for v in range(GROUP):
            halves = []
            for hh in (2 * v, 2 * v + 1):
                x = o[hh * t:(hh + 1) * t, :]
                halves.append(pltpu.roll(x, HEAD_DIM, axis=1) if hh % 2 != hh // GROUP else x)
            slabs.append(jnp.where(lo, halves[0], halves[1]))
        o_ref[n * t:(n + 1) * t, pr * pair_w:(pr + 1) * pair_w] = jnp.concatenate(slabs, axis=1)


def _sample_attention(qkv, cache_k, cache_v, sink_lanes, n, t, attn_w, kv_dim, bn=8):
    n_kv = kv_dim // HEAD_DIM
    assert n_kv % 2 == 0 and t == SUBLANES
    kcol = attn_w // kv_dim
    return pl.pallas_call(
        functools.partial(_sample_attn_body, bn=bn, n_kv=n_kv, t=t),
        grid=(n // bn,),
        in_specs=[pl.BlockSpec((n_kv // 2, 1, 2 * HEAD_DIM), lambda i: (0, 0, 0)),
                  pl.BlockSpec((bn * t, attn_w), lambda i: (i, 0)),
                  pl.BlockSpec((bn * t, kv_dim), lambda i: (i, kcol)),
                  pl.BlockSpec((bn * t, kv_dim), lambda i: (i, kcol + 1)),
                  pl.BlockSpec((bn, WINDOW, kv_dim), lambda i: (i, 0, 0)),
                  pl.BlockSpec((bn, WINDOW, kv_dim), lambda i: (i, 0, 0))],
        out_specs=pl.BlockSpec((bn * t, attn_w), lambda i: (i, 0)),
        out_shape=jax.ShapeDtypeStruct((n * t, attn_w), F32),
        compiler_params=_params(("parallel",), 24 * MIB),
        name="sample_attention",
    )(sink_lanes, qkv, qkv, qkv, cache_k, cache_v)


def _seq_blocks(seq_len, tm, mod3):
    mod, row0 = mod3
    if seq_len >= tm:
        assert seq_len % tm == 0
        tiles = seq_len // tm
        return mod, 1, (lambda i: row0 + i // tiles)
    assert tm % seq_len == 0 and row0 % (tm // seq_len) == 0
    b0 = row0 // (tm // seq_len)
    return mod, tm // seq_len, (lambda i: b0 + i)


def _gated_residual(x, acc, mod_ref, gate_idx):
    tm, tn = acc.shape
    bn = mod_ref.shape[0]
    gate = mod_ref[:, gate_idx:gate_idx + 1, :]
    y = x.reshape(bn, tm // bn, tn) + gate * acc.reshape(bn, tm // bn, tn)
    return y.reshape(tm, tn)


def _out_proj_body(a_ref, c_ref, wa_ref, wc_ref, x_ref, mod_ref, o_ref, woa_ref=None, woc_ref=None, *, gate_idx):
    acc = (_dot(a_ref[...].astype(BF16), _weight_ref(wa_ref, woa_ref)[...])
           + _dot(c_ref[...], _weight_ref(wc_ref, woc_ref)[...]))
    o_ref[...] = _gated_residual(x_ref[...], acc, mod_ref, gate_idx)


def _out_proj(attn, conv, wa, wc, row_blocks, x, mod3, gate_idx, seq_len, tm, tn, emit_w):
    m, ka = attn.shape
    assert conv.shape[1] == ka
    d = x.shape[1]
    n_i = m // tm
    assert n_i == 1 or not emit_w
    mod, bn, seq_of = _seq_blocks(seq_len, tm, mod3)
    ra, rc = row_blocks
    col = _snake(d // tn)
    out_specs = [pl.BlockSpec((tm, tn), lambda i, j: (i, col(i, j)))]
    out_shape = [jax.ShapeDtypeStruct((m, d), F32)]
    if emit_w:
        out_specs += [pl.BlockSpec((ka, tn), lambda i, j: (0, col(i, j)))] * 2
        out_shape += [jax.ShapeDtypeStruct((ka, d), BF16)] * 2
    vmem = (min(n_i, 2) * (_nbytes((tm, ka), attn.dtype) + tm * ka * 2) + tm * ka * 2
            + 4 * _nbytes((ka, tn), wa.dtype) + (4 * ka * tn * 2 if emit_w else 0) + 7 * tm * tn * 4)
    return pl.pallas_call(
        functools.partial(_out_proj_body, gate_idx=gate_idx),
        grid=(n_i, d // tn),
        in_specs=[_row_spec((tm, ka), lambda i, j: (i, 0), n_i),
                  _row_spec((tm, ka), lambda i, j: (i, 0), n_i),
                  pl.BlockSpec((ka, tn), lambda i, j: (ra, col(i, j))),
                  pl.BlockSpec((ka, tn), lambda i, j: (rc, col(i, j))),
                  pl.BlockSpec((tm, tn), lambda i, j: (i, col(i, j))),
                  pl.BlockSpec((bn, N_MOD, tn), lambda i, j: (seq_of(i), 0, col(i, j)))],
        out_specs=out_specs, out_shape=out_shape,
        compiler_params=_params(("arbitrary", "arbitrary"), vmem + 4 * MIB),
        name="out_proj",
    )(attn, conv, wa, wc, x, mod)


def _out_proj_norm_body(a_ref, c_ref, wa_ref, wc_ref, x_ref, mod_ref, g_ref, o_ref, h_ref, *, nj, rows):
    j = pl.program_id(1)
    tn = x_ref.shape[1]
    cols = pl.ds(pl.multiple_of(j * tn, 128), tn)
    acc = _dot(a_ref[...], wa_ref[...]) + _dot(c_ref[...], wc_ref[...])
    o_ref[:, cols] = x_ref[...] + mod_ref[0, 2:3, cols] * acc

    @pl.when(j == nj - 1)
    def _():
        gain = g_ref[...] * (1.0 + mod_ref[0, 4:5, :])
        shift = mod_ref[0, 3:4, :]
        for r in range(0, o_ref.shape[0], rows):
            h_ref[r:r + rows, :] = (_rms(o_ref[r:r + rows, :]) * gain + shift).astype(h_ref.dtype)


def _out_proj_norm(attn, conv, wa, wc, x, mod3, g, seq_len, tm, tn):
    m, ka = attn.shape
    d = x.shape[1]
    mod, row0 = mod3
    assert seq_len % tm == 0
    tiles = seq_len // tm
    nj = d // tn
    vmem = 2 * (2 * tm * ka * 2 + 2 * ka * tn * 2 + tm * tn * 4 + tm * d * 4 + tm * d * 2) + 8 * tm * tn * 4
    return pl.pallas_call(
        functools.partial(_out_proj_norm_body, nj=nj, rows=128),
        grid=(m // tm, nj),
        in_specs=[pl.BlockSpec((tm, ka), lambda i, j: (i, 0)),
                  pl.BlockSpec((tm, ka), lambda i, j: (i, 0)),
                  pl.BlockSpec((ka, tn), lambda i, j: (0, j)),
                  pl.BlockSpec((ka, tn), lambda i, j: (0, j)),
                  pl.BlockSpec((tm, tn), lambda i, j: (i, j)),
                  pl.BlockSpec((1, N_MOD, d), lambda i, j: (row0 + i // tiles, 0, 0)),
                  pl.BlockSpec((1, d), lambda i, j: (0, 0))],
        out_specs=[pl.BlockSpec((tm, d), lambda i, j: (i, 0)),
                   pl.BlockSpec((tm, d), lambda i, j: (i, 0))],
        out_shape=[jax.ShapeDtypeStruct((m, d), F32), jax.ShapeDtypeStruct((m, d), BF16)],
        compiler_params=_params(("arbitrary", "arbitrary"), vmem + 4 * MIB),
        name="out_proj_norm",
    )(attn, conv, wa, wc, x, mod, g.reshape(1, d))


def _ffn_down_body(a_ref, w_ref, x_ref, mod_ref, o_ref, wout_ref=None, *, gate_idx, nk):
    if nk == 1:
        acc = _dot(a_ref[...], _weight_ref(w_ref, wout_ref)[...])
        o_ref[...] = _gated_residual(x_ref[...], acc, mod_ref, gate_idx)
    else:
        kk = pl.program_id(2)

        @pl.when(kk == 0)
        def _():
            o_ref[...] = x_ref[...]

        tk = w_ref.shape[0]
        a = a_ref[:, pl.ds(pl.multiple_of(kk * tk, 128), tk)]
        acc = _dot(a, _weight_ref(w_ref, wout_ref)[...])
        o_ref[...] = _gated_residual(o_ref[...], acc, mod_ref, gate_idx)


def _ffn_down(act, w_down, x, mod3, gate_idx, seq_len, tm, tn, nk, emit_w):
    m, k = act.shape
    d = x.shape[1]
    tk = k // nk
    assert tk * nk == k and (m == tm or not emit_w)
    mod, bn, seq_of = _seq_blocks(seq_len, tm, mod3)
    col = _snake(d // tn)
    out_specs = [pl.BlockSpec((tm, tn), lambda i, j, kk: (i, col(i, j)))]
    out_shape = [jax.ShapeDtypeStruct((m, d), F32)]
    if emit_w:
        out_specs.append(pl.BlockSpec((tk, tn), lambda i, j, kk: (kk, col(i, j))))
        out_shape.append(jax.ShapeDtypeStruct((k, d), BF16))
    assert tk % 128 == 0
    n_i = m // tm
    vmem = (min(n_i, 2) * tm * k * 2 + 2 * _nbytes((tk, tn), w_down.dtype) + (2 * tk * tn * 2 if emit_w else 0)
            + 7 * tm * tn * 4)
    return pl.pallas_call(
        functools.partial(_ffn_down_body, gate_idx=gate_idx, nk=nk),
        grid=(n_i, d // tn, nk),
        in_specs=[_row_spec((tm, k), lambda i, j, kk: (i, 0), n_i),
                  pl.BlockSpec((tk, tn), lambda i, j, kk: (kk, col(i, j))),
                  pl.BlockSpec((tm, tn), lambda i, j, kk: (i, col(i, j))),
                  pl.BlockSpec((bn, N_MOD, tn), lambda i, j, kk: (seq_of(i), 0, col(i, j)))],
        out_specs=out_specs, out_shape=out_shape,
        compiler_params=_params(("arbitrary", "arbitrary", "arbitrary"), vmem + 4 * MIB),
        name="ffn_down",
    )(act, w_down, x, mod)


ROW_TILE = 1024
HALF_ROW_TILE = 512
COL_TILE_BF16 = 512
COL_TILE_F32 = 256
NORM_ROWS = 512
NORM_SEQS = 32


def _tiles(is_prompt, n, t):
    if is_prompt:
        tm = min(ROW_TILE, t)
        half = min(HALF_ROW_TILE, tm)
        return tm, half, half, COL_TILE_BF16, 1, 1, min(NORM_ROWS, t)
    assert t == SUBLANES
    tm = min(ROW_TILE, n * t)
    return tm, tm, tm, COL_TILE_F32, 2, min(NORM_SEQS, n), t


def _layer(x3, mod3, conv_state, ffn_state, cache_k, cache_v, w, is_prompt):
    n, t, d = x3.shape
    m = n * t
    attn_w = d // 2
    d_conv = d - attn_w
    n_q = attn_w // HEAD_DIM
    n_kv = max(1, n_q // GROUP)
    kv_dim = n_kv * HEAD_DIM
    off_b = attn_w + 2 * kv_dim
    x2 = x3.reshape(m, d)
    emit = not is_prompt

    tm, tm_conv, tm_down, tn, nk_down, bn, tt = _tiles(is_prompt, n, t)
    if is_prompt:
        w_conv, off_conv = (w["w_b"], w["w_c"], w["w_h"]), (0, 0, 0)
        w_act, off_act = (w["w_g"], w["w_u"]), (0, 0)
        w_oa, w_oc, out_rows = w["w_oa"], w["w_oc"], (0, 0)
        w_qkv, w_down = w["w_qkv"], w["w_d"]
        d_ff = w_down.shape[0]
    else:
        d_ff = w["w_down"].shape[0]
        w_conv, off_conv = (w["w_in"],) * 3, (off_b, off_b + d_conv, off_b + 2 * d_conv)
        w_act, off_act = (w["w_gate_up"],) * 2, (0, d_ff)
        w_oa = w_oc = w["w_out"]
        out_rows = (0, 1)
        w_qkv, w_down = w["w_in"], w["w_down"]

    h = _norm_mod(x3, mod3, w["g_mix"], 0, 1, bn, tt)
    qkv, *wb_qkv = _matmul(h, w_qkv, off_b, tm, tn, emit)
    conv, conv_aux, *wb_conv = _gated_proj(h, w_conv, off_conv, d_conv, w["conv_w"], conv_state, t, tm_conv,
                                           tn, emit, "conv_gate")
    if is_prompt:
        attn = _prompt_attention(qkv, w["sinks"], n, t, attn_w, kv_dim)
    else:
        sink_lanes = jnp.repeat(w["sinks"].reshape(n_kv // 2, 1, 2 * GROUP), t, axis=2)
        attn = _sample_attention(qkv, cache_k.reshape(n, WINDOW, kv_dim), cache_v.reshape(n, WINDOW, kv_dim),
                                 sink_lanes, n, t, attn_w, kv_dim, bn=min(16, n))
    if is_prompt:
        x1, h2 = _out_proj_norm(attn, conv, w_oa, w_oc, x2, mod3, w["g_ffn"], t, tm_down, tn)
        wb_out = []
    else:
        x1, *wb_out = _out_proj(attn, conv, w_oa, w_oc, out_rows, x2, mod3, 2, t, tm, tn, emit)
        h2 = _norm_mod(x1.reshape(n, t, d), mod3, w["g_ffn"], 3, 4, bn, tt)
    act, ffn_aux, *wb_act = _gated_proj(h2, w_act, off_act, d_ff, w["ffn_conv_w"], ffn_state, t, tm, tn, emit,
                                        "ffn_act")
    xo, *wb_down = _ffn_down(act, w_down, x1, mod3, 5, t, tm_down, tn, nk_down, emit)

    keep = min(WINDOW, t)
    kv_new = qkv.reshape(n, t, off_b)[:, t - keep:, attn_w:]
    k_new = kv_new[:, :, :kv_dim].reshape(n, keep, n_kv, HEAD_DIM)
    v_new = kv_new[:, :, kv_dim:].reshape(n, keep, n_kv, HEAD_DIM)
    if is_prompt:
        k_state, v_state = k_new, v_new
        conv_state_out = conv_aux[t // tm_conv - 1::t // tm_conv, SUBLANES - (CONV_K - 1):, :]
        ffn_state_out = ffn_aux[t // tm - 1::t // tm, SUBLANES - (CONV_K - 1):, :]
        return xo, (k_state, v_state, conv_state_out, ffn_state_out)
    k_state = jnp.concatenate([cache_k, k_new], axis=1)[:, t:]
    v_state = jnp.concatenate([cache_v, v_new], axis=1)[:, t:]
    wb = dict(w, w_qkv=wb_qkv[0], w_b=wb_conv[0], w_c=wb_conv[1], w_h=wb_conv[2], w_oa=wb_out[0], w_oc=wb_out[1],
              w_g=wb_act[0], w_u=wb_act[1], w_d=wb_down[0])
    return xo, (k_state, v_state, conv_aux, ffn_aux), wb


def kernel(x_prompt, x_sample, cache_k_win, cache_v_win, state_conv, state_ffn_conv, c_prompt, c_sample, w_mod, b_mod, g_mix, g_ffn, w_in, conv_w, sinks, w_out, w_gate_up, ffn_conv_w, w_down, g_final):
    depth = w_mod.shape[0]
    np_, tp, d = x_prompt.shape
    ns, ts, _ = x_sample.shape
    assert tp % BLOCK == 0 and cache_k_win.shape[2] == WINDOW

    n_c = np_ + ns
    n_c_pad = -(-n_c // SUBLANES) * SUBLANES
    c_all = jnp.concatenate([c_sample, c_prompt, jnp.zeros((n_c_pad - n_c, d), F32)], axis=0)

    xp, xs = x_prompt, x_sample
    outs_p, outs_s = [], []
    for l in range(depth):
        w = dict(g_mix=g_mix[l], g_ffn=g_ffn[l], conv_w=conv_w[l], sinks=sinks[l], ffn_conv_w=ffn_conv_w[l],
                 w_in=w_in[l], w_out=w_out[l], w_gate_up=w_gate_up[l], w_down=w_down[l])
        mod = _mod(c_all, w_mod[l], b_mod[l])
        xs2, st_s, wb = _layer(xs, (mod, 0), state_conv[l], state_ffn_conv[l], cache_k_win[l], cache_v_win[l],
                               w, False)
        xp2, st_p = _layer(xp, (mod, ns), None, None, None, None, wb, True)
        xp, xs = xp2.reshape(xp.shape), xs2.reshape(xs.shape)
        outs_p.append(st_p)
        outs_s.append(st_s)

    y_prompt = _final_norm(xp.reshape(np_ * tp, d), g_final).reshape(np_, tp, d)
    y_sample = _final_norm(xs.reshape(ns * ts, d), g_final).reshape(ns, ts, d)
    stack = lambda outs, k: jnp.stack([o[k] for o in outs])
    return (y_prompt, y_sample,
            stack(outs_p, 0), stack(outs_p, 1), stack(outs_p, 2), stack(outs_p, 3),
            stack(outs_s, 0), stack(outs_s, 1), stack(outs_s, 2), stack(outs_s, 3))
```

```python
import functools

import jax
import jax.numpy as jnp
from jax import lax
from jax.experimental import pallas as pl
from jax.experimental.pallas import tpu as pltpu

F32 = jnp.float32
BF16 = jnp.bfloat16

HEAD_DIM = 64
SCALE = HEAD_DIM ** -0.5
GROUP = 8
WINDOW = 128
BLOCK = 128
CONV_K = 3
N_MOD = 6
EPS = 1e-5
SUBLANES = 8
MIB = 1024 * 1024
VMEM_CAP = 56 * MIB


def _params(semantics, vmem_bytes):
    return pltpu.CompilerParams(dimension_semantics=semantics,
                                vmem_limit_bytes=int(min(max(vmem_bytes, 16 * MIB), VMEM_CAP)))


def _dot(a, b):
    return jnp.dot(a, b, preferred_element_type=F32)


def _dot_nt(a, b):
    return lax.dot_general(a, b, (((1,), (1,)), ((), ())), preferred_element_type=F32)


def _weight_ref(w_ref, wout_ref):
    if wout_ref is None:
        return w_ref
    wout_ref[...] = w_ref[...].astype(BF16)
    return wout_ref


def _row_spec(shape, index_map, n_row_tiles):
    if n_row_tiles == 1:
        return pl.BlockSpec(shape, index_map, pipeline_mode=pl.Buffered(1))
    return pl.BlockSpec(shape, index_map)


def _snake(nj):
    return lambda i, j: jnp.where(i % 2 == 0, j, nj - 1 - j)


def _nbytes(shape, dtype):
    n = jnp.dtype(dtype).itemsize
    for s in shape:
        n *= s
    return n


def _mod_body(c_ref, *refs):
    w_refs, b_ref, o_ref = refs[:N_MOD], refs[N_MOD], refs[N_MOD + 1]
    c = c_ref[...]
    a = (c * jax.nn.sigmoid(c)).astype(BF16)
    for k in range(N_MOD):
        o_ref[:, k, :] = _dot(a, w_refs[k][...].astype(BF16)) + b_ref[k:k + 1, :]


def _mod(c_all, w_mod, b_mod, tn=128):
    nc, d = c_all.shape
    nj = d // tn
    w_specs = [pl.BlockSpec((d, tn), functools.partial(lambda j, k: (0, k * nj + j), k=k)) for k in range(N_MOD)]
    vmem = 2 * (nc * d * 4 + N_MOD * d * tn * 4 + nc * SUBLANES * tn * 4) + N_MOD * d * tn * 2 + nc * d * 4
    return pl.pallas_call(
        _mod_body,
        grid=(nj,),
        in_specs=[pl.BlockSpec((nc, d), lambda j: (0, 0))] + w_specs + [pl.BlockSpec((N_MOD, tn), lambda j: (0, j))],
        out_specs=pl.BlockSpec((nc, N_MOD, tn), lambda j: (0, 0, j)),
        out_shape=jax.ShapeDtypeStruct((nc, N_MOD, d), F32),
        compiler_params=_params(("parallel",), vmem + 4 * MIB),
        name="mod",
    )(c_all, *([w_mod] * N_MOD), b_mod.reshape(N_MOD, d))


def _rms(x):
    return x * lax.rsqrt(jnp.mean(x * x, axis=-1, keepdims=True) + EPS)


def _norm_mod_body(x_ref, mod_ref, g_ref, o_ref, *, shift_idx, scale_idx):
    x = x_ref[...]
    gain = g_ref[...] * (1.0 + mod_ref[:, scale_idx:scale_idx + 1, :])
    h = _rms(x) * gain + mod_ref[:, shift_idx:shift_idx + 1, :]
    o_ref[...] = h.reshape(o_ref.shape).astype(o_ref.dtype)


def _norm_mod(x3, mod3, g, shift_idx, scale_idx, bn, tt):
    n, t, d = x3.shape
    mod, row0 = mod3
    assert row0 % bn == 0
    b0 = row0 // bn
    nt = t // tt
    vmem = 2 * (bn * tt * d * 4 + bn * tt * d * 2 + bn * SUBLANES * d * 4) + 3 * bn * tt * d * 4
    return pl.pallas_call(
        functools.partial(_norm_mod_body, shift_idx=shift_idx, scale_idx=scale_idx),
        grid=(n // bn, nt),
        in_specs=[pl.BlockSpec((bn, tt, d), lambda i, j: (i, j, 0)),
                  pl.BlockSpec((bn, N_MOD, d), lambda i, j: (b0 + i, 0, 0)),
                  pl.BlockSpec((1, d), lambda i, j: (0, 0))],
        out_specs=pl.BlockSpec((bn * tt, d), lambda i, j: (i * nt + j, 0)),
        out_shape=jax.ShapeDtypeStruct((n * t, d), BF16),
        compiler_params=_params(("parallel", "parallel"), vmem),
        name="norm_mod",
    )(x3, mod, g.reshape(1, d))


def _final_norm_body(x_ref, g_ref, o_ref):
    o_ref[...] = _rms(x_ref[...]) * g_ref[...]


def _final_norm(x2, g, tm=512):
    m, d = x2.shape
    tm = min(tm, m)
    return pl.pallas_call(
        _final_norm_body,
        grid=(m // tm,),
        in_specs=[pl.BlockSpec((tm, d), lambda i: (i, 0)),
                  pl.BlockSpec((1, d), lambda i: (0, 0))],
        out_specs=pl.BlockSpec((tm, d), lambda i: (i, 0)),
        out_shape=jax.ShapeDtypeStruct((m, d), F32),
        compiler_params=_params(("parallel",), 6 * tm * d * 4),
        name="final_norm",
    )(x2, g.reshape(1, d))


def _matmul_body(a_ref, w_ref, o_ref, wout_ref=None):
    o_ref[...] = _dot(a_ref[...], _weight_ref(w_ref, wout_ref)[...]).astype(o_ref.dtype)


def _matmul(a, w, n_cols, tm, tn, emit_w):
    m, k = a.shape
    n_i = m // tm
    assert n_i == 1 or not emit_w
    col = _snake(n_cols // tn)
    out_specs = [pl.BlockSpec((tm, tn), lambda i, j: (i, col(i, j)))]
    out_shape = [jax.ShapeDtypeStruct((m, n_cols), F32)]
    if emit_w:
        out_specs.append(pl.BlockSpec((k, tn), lambda i, j: (0, col(i, j))))
        out_shape.append(jax.ShapeDtypeStruct((k, n_cols), BF16))
    vmem = (min(n_i, 2) * tm * k * 2 + 2 * _nbytes((k, tn), w.dtype) + (2 * k * tn * 2 if emit_w else 0)
            + 3 * tm * tn * 4)
    return pl.pallas_call(
        _matmul_body,
        grid=(n_i, n_cols // tn),
        in_specs=[_row_spec((tm, k), lambda i, j: (i, 0), n_i),
                  _row_spec((k, tn), lambda i, j: (0, col(i, j)), n_cols // tn)],
        out_specs=out_specs, out_shape=out_shape,
        compiler_params=_params(("arbitrary", "arbitrary"), vmem + 4 * MIB),
        name="qkv",
    )(a, w)


def _silu(x):
    return x * jax.nn.sigmoid(x)


def _gated_body(*refs, n_w, emit_w, is_prompt, tiles_per_seq):
    h_ref = refs[0]
    w_refs = refs[1:1 + n_w]
    cw_ref = refs[1 + n_w]
    pos = 2 + n_w
    st_ref = None
    if not is_prompt:
        st_ref = refs[pos]
        pos += 1
    o_ref, aux_ref = refs[pos], refs[pos + 1]
    pos += 2
    wout_refs = refs[pos:pos + n_w] if emit_w else (None,) * n_w
    carry_ref = refs[-1] if is_prompt else None

    ws = [_weight_ref(r, o) for r, o in zip(w_refs, wout_refs)]
    cw = cw_ref[...]
    tm, tn = o_ref.shape
    rm = tm // 2
    nb = rm // SUBLANES
    t = lax.broadcasted_iota(jnp.int32, (nb, SUBLANES, tn), 1)
    if is_prompt:
        i, j = pl.program_id(0), pl.program_id(1)

        @pl.when((i % tiles_per_seq) == 0)
        def _():
            carry_ref[j] = jnp.zeros((SUBLANES, tn), F32)

        tail = carry_ref[j]
    for r0 in (0, rm):
        h = h_ref[r0:r0 + rm, :]
        if n_w == 3:
            v = _dot(h, ws[1][...]) * _dot(h, ws[2][...])
        else:
            v = _dot(h, ws[0][...])
        v3 = v.reshape(nb, SUBLANES, tn)
        if is_prompt:
            ext = jnp.concatenate([tail[None], v3], axis=0)
            e1 = pltpu.roll(ext, 1, axis=1)
            e2 = pltpu.roll(ext, 2, axis=1)
            v1 = jnp.where(t == 0, e1[:nb], e1[1:])
            v2 = jnp.where(t < 2, e2[:nb], e2[1:])
            tail = v3[nb - 1]
        else:
            g0 = r0 // SUBLANES
            p0 = st_ref[g0:g0 + nb, 0:1, :]
            p1 = st_ref[g0:g0 + nb, 1:2, :]
            v1 = jnp.where(t == 0, p1, pltpu.roll(v3, 1, axis=1))
            v2 = jnp.where(t == 0, p0, jnp.where(t == 1, p1, pltpu.roll(v3, 2, axis=1)))
            aux_ref[g0:g0 + nb] = v3[:, SUBLANES - (CONV_K - 1):, :]
        y = (cw[0:1, :] * v2 + cw[1:2, :] * v1 + cw[2:3, :] * v3).reshape(rm, tn)
        if n_w == 3:
            out = _dot(h, ws[0][...]) * y
        else:
            out = _silu(y) * _dot(h, ws[1][...])
        o_ref[r0:r0 + rm, :] = out.astype(o_ref.dtype)
    if is_prompt:
        carry_ref[j] = tail
        aux_ref[0] = tail


def _gated_proj(h, ws, col_offsets, n_cols, conv_w, state, seq_len, tm, tn, emit_w, name):
    m, k = h.shape
    n_w = len(ws)
    n_i, nj = m // tm, pl.cdiv(n_cols, tn)
    assert all(o % tn == 0 for o in col_offsets)
    offs = [o // tn for o in col_offsets]
    is_prompt = state is None
    spec = ((lambda shape, f: pl.BlockSpec(shape, f)) if is_prompt else
            (lambda shape, f: pl.BlockSpec(shape, lambda j, i: f(i, j))))
    w_specs = [spec((k, tn), functools.partial(lambda i, j, o: (0, o + j), o=o)) for o in offs]
    h_spec = _row_spec((tm, k), (lambda i, j: (i, 0)) if is_prompt else (lambda j, i: (i, 0)), n_i)
    in_specs = [h_spec] + w_specs + [spec((CONV_K, tn), lambda i, j: (0, j))]
    out_specs = [spec((tm, tn), lambda i, j: (i, j))]
    out_shape = [jax.ShapeDtypeStruct((m, n_cols), BF16)]
    args = (h,) + tuple(ws) + (conv_w,)
    if is_prompt:
        assert seq_len % tm == 0
        tiles_per_seq = seq_len // tm
        grid = (n_i, nj)
        out_specs.append(spec((1, SUBLANES, tn), lambda i, j: (i, 0, j)))
        out_shape.append(jax.ShapeDtypeStruct((n_i, SUBLANES, n_cols), F32))
        scratch = [pltpu.VMEM((nj, SUBLANES, tn), F32)]
    else:
        assert seq_len == SUBLANES and tm % SUBLANES == 0
        tiles_per_seq = None
        grid = (nj, n_i)
        nb = tm // SUBLANES
        in_specs.append(spec((nb, CONV_K - 1, tn), lambda i, j: (i, 0, j)))
        out_specs.append(spec((nb, CONV_K - 1, tn), lambda i, j: (i, 0, j)))
        out_shape.append(jax.ShapeDtypeStruct(state.shape, F32))
        scratch = []
        args += (state,)
    if emit_w:
        out_specs += [spec((k, tn), lambda i, j: (0, j))] * n_w
        out_shape += [jax.ShapeDtypeStruct((k, n_cols), BF16)] * n_w
    vmem = (min(n_i, 2) * tm * k * 2 + 2 * sum(_nbytes((k, tn), w.dtype) for w in ws)
            + (2 * n_w * k * tn * 2 if emit_w else 0) + 2 * tm * tn * 2 + 14 * (tm // 2) * tn * 4)
    return pl.pallas_call(
        functools.partial(_gated_body, n_w=n_w, emit_w=emit_w, is_prompt=is_prompt, tiles_per_seq=tiles_per_seq),
        grid=grid, in_specs=in_specs, out_specs=out_specs, out_shape=out_shape,
        scratch_shapes=scratch, compiler_params=_params(("arbitrary", "arbitrary"), vmem + 4 * MIB), name=name,
    )(*args)


def _sink_softmax(s, mask, sink, axis):
    s = jnp.where(mask, s, -jnp.inf)
    mx = jnp.maximum(jnp.max(s, axis=axis, keepdims=True), sink)
    p = jnp.exp(s - mx)
    denom = jnp.sum(p, axis=axis, keepdims=True) + jnp.exp(sink - mx)
    return p * (1.0 / denom)


def _dot_tn(a, b):
    return lax.dot_general(a, b, (((0,), (0,)), ((), ())), preferred_element_type=F32)


def _block_diag_pair(tile, e):
    lo = lax.broadcasted_iota(jnp.int32, tile.shape, 1) < HEAD_DIM
    rolled = pltpu.roll(tile, HEAD_DIM, axis=1)
    zero = jnp.zeros_like(tile)
    if e == 0:
        top, bot = jnp.where(lo, tile, zero), jnp.where(lo, zero, rolled)
    else:
        top, bot = jnp.where(lo, rolled, zero), jnp.where(lo, zero, tile)
    return jnp.concatenate([top, bot], axis=0).astype(BF16)


def _prompt_attn_body(sink_ref, q_ref, kp_ref, kc_ref, vp_ref, vc_ref, o_ref, *, n_kv, blocks):
    step = pl.program_id(1)
    kall = jnp.concatenate([kp_ref[...], kc_ref[...]], axis=0)
    vall = jnp.concatenate([vp_ref[...], vc_ref[...]], axis=0)
    c = lax.broadcasted_iota(jnp.int32, (2 * BLOCK, BLOCK), 0)
    r = lax.broadcasted_iota(jnp.int32, (2 * BLOCK, BLOCK), 1)
    band = (c >= r) & (c <= r + WINDOW)
    first = lax.broadcasted_iota(jnp.int32, (2, 1, BLOCK), 0) == 0
    pairs_per_kv = GROUP // 2
    n_pairs = n_kv * pairs_per_kv
    for u in range(blocks):
        kk = kall[u * BLOCK:(u + 2) * BLOCK, :]
        vv = vall[u * BLOCK:(u + 2) * BLOCK, :]
        mask = (band & ((c >= BLOCK) | (step * blocks + u > 0)))[None]
        kbd, vbd = [], []
        for kh in range(n_kv):
            sl = slice((kh // 2) * 2 * HEAD_DIM, (kh // 2 + 1) * 2 * HEAD_DIM)
            kbd.append(_block_diag_pair(kk[:, sl], kh % 2))
            vbd.append(_block_diag_pair(vv[:, sl], kh % 2))
        rows = slice(u * BLOCK, (u + 1) * BLOCK)
        qp = [(q_ref[rows, i * 2 * HEAD_DIM:(i + 1) * 2 * HEAD_DIM] * SCALE).astype(BF16) for i in range(n_pairs)]
        s = [_dot_nt(kbd[i // pairs_per_kv], qp[i]).reshape(2, 2 * BLOCK, BLOCK) for i in range(n_pairs)]
        p = []
        for i in range(n_pairs):
            sink = jnp.where(first, sink_ref[2 * i], sink_ref[2 * i + 1])
            p.append(_sink_softmax(s[i], mask, sink, axis=1).astype(BF16).reshape(4 * BLOCK, BLOCK))
        for i in range(n_pairs):
            o_ref[rows, i * 2 * HEAD_DIM:(i + 1) * 2 * HEAD_DIM] = _dot_tn(p[i], vbd[i // pairs_per_kv]).astype(o_ref.dtype)


def _prompt_attention(qkv, sinks, n, t, attn_w, kv_dim):
    nb = t // BLOCK
    blocks = next(b for b in (4, 2, 1) if nb % b == 0)
    ns = nb // blocks
    kcol = attn_w // kv_dim
    cur = lambda col: (lambda i, b: (i * ns + b, col))
    prev = lambda col: (lambda i, b: (i * nb + jnp.maximum(b * blocks - 1, 0), col))
    return pl.pallas_call(
        functools.partial(_prompt_attn_body, n_kv=kv_dim // HEAD_DIM, blocks=blocks),
        grid=(n, ns),
        in_specs=[pl.BlockSpec(memory_space=pltpu.MemorySpace.SMEM),
                  pl.BlockSpec((blocks * BLOCK, attn_w), cur(0)),
                  pl.BlockSpec((BLOCK, kv_dim), prev(kcol)),
                  pl.BlockSpec((blocks * BLOCK, kv_dim), cur(kcol)),
                  pl.BlockSpec((BLOCK, kv_dim), prev(kcol + 1)),
                  pl.BlockSpec((blocks * BLOCK, kv_dim), cur(kcol + 1))],
        out_specs=pl.BlockSpec((blocks * BLOCK, attn_w), cur(0)),
        out_shape=jax.ShapeDtypeStruct((n * t, attn_w), BF16),
        compiler_params=_params(("parallel", "parallel"), 32 * MIB),
        name="prompt_attention",
    )(sinks, qkv, qkv, qkv, qkv, qkv)


def _sample_attn_body(sink_ref, q_ref, kn_ref, vn_ref, ck_ref, cv_ref, o_ref, *, bn, n_kv, t):
    keys = 2 * WINDOW
    pair_w = 2 * GROUP * HEAD_DIM
    n_pairs = n_kv // 2
    c = lax.broadcasted_iota(jnp.int32, (keys, 2 * HEAD_DIM), 0)
    tok = lax.broadcasted_iota(jnp.int32, (keys, 2 * HEAD_DIM), 1) % t
    mask = ((c < WINDOW) & (c >= tok)) | ((c >= WINDOW) & (c - WINDOW <= tok))
    pad = jnp.zeros((keys - WINDOW - t, n_kv * HEAD_DIM), F32)
    lo = lax.broadcasted_iota(jnp.int32, (t, 2 * HEAD_DIM), 1) < HEAD_DIM
    zero = jnp.zeros((t, 2 * HEAD_DIM), F32)
    work = [(n, pr) for n in range(bn) for pr in range(n_pairs)]
    lanes = lambda a, pr: a[:, pr * 2 * HEAD_DIM:(pr + 1) * 2 * HEAD_DIM]
    kk = [jnp.concatenate([ck_ref[n], kn_ref[n * t:(n + 1) * t, :], pad], axis=0).astype(BF16) for n in range(bn)]
    vv = [jnp.concatenate([cv_ref[n], vn_ref[n * t:(n + 1) * t, :], pad], axis=0).astype(BF16) for n in range(bn)]

    def q_block_diag(n, pr):
        pieces = []
        for hh in range(2 * GROUP):
            e, v = hh // GROUP, hh // 2
            x = q_ref[n * t:(n + 1) * t, pr * pair_w + v * 2 * HEAD_DIM:pr * pair_w + (v + 1) * 2 * HEAD_DIM] * SCALE
            if hh % 2 != e:
                x = pltpu.roll(x, HEAD_DIM, axis=1)
            pieces.append(jnp.where(lo, x, zero) if e == 0 else jnp.where(lo, zero, x))
        return jnp.concatenate(pieces, axis=0).astype(BF16)

    s = [_dot_nt(lanes(kk[n], pr), q_block_diag(n, pr)) for n, pr in work]
    p = [_sink_softmax(s[i], mask, sink_ref[pr], axis=0).astype(BF16) for i, (n, pr) in enumerate(work)]
    for i, (n, pr) in enumerate(work):
        o = _dot_tn(p[i], lanes(vv[n], pr))
        slabs = []
        for v in range(GROUP):
            halves = []
            for hh in (2 * v, 2 * v + 1):
                x = o[hh * t:(hh + 1) * t, :]
                halves.append(pltpu.roll(x, HEAD_DIM, axis=1) if hh % 2 != hh // GROUP else x)
            slabs.append(jnp.where(lo, halves[0], halves[1]))
        o_ref[n * t:(n + 1) * t, pr * pair_w:(pr + 1) * pair_w] = jnp.concatenate(slabs, axis=1)


def _sample_attention(qkv, cache_k, cache_v, sink_lanes, n, t, attn_w, kv_dim, bn=8):
    n_kv = kv_dim // HEAD_DIM
    assert n_kv % 2 == 0 and t == SUBLANES
    kcol = attn_w // kv_dim
    return pl.pallas_call(
        functools.partial(_sample_attn_body, bn=bn, n_kv=n_kv, t=t),
        grid=(n // bn,),
        in_specs=[pl.BlockSpec((n_kv // 2, 1, 2 * HEAD_DIM), lambda i: (0, 0, 0)),
                  pl.BlockSpec((bn * t, attn_w), lambda i: (i, 0)),
                  pl.BlockSpec((bn * t, kv_dim), lambda i: (i, kcol)),
                  pl.BlockSpec((bn * t, kv_dim), lambda i: (i, kcol + 1)),
                  pl.BlockSpec((bn, WINDOW, kv_dim), lambda i: (i, 0, 0)),
                  pl.BlockSpec((bn, WINDOW, kv_dim), lambda i: (i, 0, 0))],
        out_specs=pl.BlockSpec((bn * t, attn_w), lambda i: (i, 0)),
        out_shape=jax.ShapeDtypeStruct((n * t, attn_w), F32),
        compiler_params=_params(("parallel",), 24 * MIB),
        name="sample_attention",
    )(sink_lanes, qkv, qkv, qkv, cache_k, cache_v)


def _seq_blocks(seq_len, tm, mod3):
    mod, row0 = mod3
    if seq_len >= tm:
        assert seq_len % tm == 0
        tiles = seq_len // tm
        return mod, 1, (lambda i: row0 + i // tiles)
    assert tm % seq_len == 0 and row0 % (tm // seq_len) == 0
    b0 = row0 // (tm // seq_len)
    return mod, tm // seq_len, (lambda i: b0 + i)


def _gated_residual(x, acc, mod_ref, gate_idx):
    tm, tn = acc.shape
    bn = mod_ref.shape[0]
    gate = mod_ref[:, gate_idx:gate_idx + 1, :]
    y = x.reshape(bn, tm // bn, tn) + gate * acc.reshape(bn, tm // bn, tn)
    return y.reshape(tm, tn)


def _out_proj_body(a_ref, c_ref, wa_ref, wc_ref, x_ref, mod_ref, o_ref, woa_ref=None, woc_ref=None, *, gate_idx):
    acc = (_dot(a_ref[...].astype(BF16), _weight_ref(wa_ref, woa_ref)[...])
           + _dot(c_ref[...], _weight_ref(wc_ref, woc_ref)[...]))
    o_ref[...] = _gated_residual(x_ref[...], acc, mod_ref, gate_idx)


def _out_proj(attn, conv, wa, wc, row_blocks, x, mod3, gate_idx, seq_len, tm, tn, emit_w):
    m, ka = attn.shape
    assert conv.shape[1] == ka
    d = x.shape[1]
    n_i = m // tm
    assert n_i == 1 or not emit_w
    mod, bn, seq_of = _seq_blocks(seq_len, tm, mod3)
    ra, rc = row_blocks
    col = _snake(d // tn)
    out_specs = [pl.BlockSpec((tm, tn), lambda i, j: (i, col(i, j)))]
    out_shape = [jax.ShapeDtypeStruct((m, d), F32)]
    if emit_w:
        out_specs += [pl.BlockSpec((ka, tn), lambda i, j: (0, col(i, j)))] * 2
        out_shape += [jax.ShapeDtypeStruct((ka, d), BF16)] * 2
    vmem = (min(n_i, 2) * (_nbytes((tm, ka), attn.dtype) + tm * ka * 2) + tm * ka * 2
            + 4 * _nbytes((ka, tn), wa.dtype) + (4 * ka * tn * 2 if emit_w else 0) + 7 * tm * tn * 4)
    return pl.pallas_call(
        functools.partial(_out_proj_body, gate_idx=gate_idx),
        grid=(n_i, d // tn),
        in_specs=[_row_spec((tm, ka), lambda i, j: (i, 0), n_i),
                  _row_spec((tm, ka), lambda i, j: (i, 0), n_i),
                  pl.BlockSpec((ka, tn), lambda i, j: (ra, col(i, j))),
                  pl.BlockSpec((ka, tn), lambda i, j: (rc, col(i, j))),
                  pl.BlockSpec((tm, tn), lambda i, j: (i, col(i, j))),
                  pl.BlockSpec((bn, N_MOD, tn), lambda i, j: (seq_of(i), 0, col(i, j)))],
        out_specs=out_specs, out_shape=out_shape,
        compiler_params=_params(("arbitrary", "arbitrary"), vmem + 4 * MIB),
        name="out_proj",
    )(attn, conv, wa, wc, x, mod)


def _ffn_down_body(a_ref, w_ref, x_ref, mod_ref, o_ref, wout_ref=None, *, gate_idx, nk):
    if nk == 1:
        acc = _dot(a_ref[...], _weight_ref(w_ref, wout_ref)[...])
        o_ref[...] = _gated_residual(x_ref[...], acc, mod_ref, gate_idx)
    else:
        kk = pl.program_id(2)

        @pl.when(kk == 0)
        def _():
            o_ref[...] = x_ref[...]

        tk = w_ref.shape[0]
        a = a_ref[:, pl.ds(pl.multiple_of(kk * tk, 128), tk)]
        acc = _dot(a, _weight_ref(w_ref, wout_ref)[...])
        o_ref[...] = _gated_residual(o_ref[...], acc, mod_ref, gate_idx)


def _ffn_down(act, w_down, x, mod3, gate_idx, seq_len, tm, tn, nk, emit_w):
    m, k = act.shape
    d = x.shape[1]
    tk = k // nk
    assert tk * nk == k and (m == tm or not emit_w)
    mod, bn, seq_of = _seq_blocks(seq_len, tm, mod3)
    col = _snake(d // tn)
    out_specs = [pl.BlockSpec((tm, tn), lambda i, j, kk: (i, col(i, j)))]
    out_shape = [jax.ShapeDtypeStruct((m, d), F32)]
    if emit_w:
        out_specs.append(pl.BlockSpec((tk, tn), lambda i, j, kk: (kk, col(i, j))))
        out_shape.append(jax.ShapeDtypeStruct((k, d), BF16))
    assert tk % 128 == 0
    n_i = m // tm
    vmem = (min(n_i, 2) * tm * k * 2 + 2 * _nbytes((tk, tn), w_down.dtype) + (2 * tk * tn * 2 if emit_w else 0)
            + 7 * tm * tn * 4)
    return pl.pallas_call(
        functools.partial(_ffn_down_body, gate_idx=gate_idx, nk=nk),
        grid=(n_i, d // tn, nk),
        in_specs=[_row_spec((tm, k), lambda i, j, kk: (i, 0), n_i),
                  pl.BlockSpec((tk, tn), lambda i, j, kk: (kk, col(i, j))),
                  pl.BlockSpec((tm, tn), lambda i, j, kk: (i, col(i, j))),
                  pl.BlockSpec((bn, N_MOD, tn), lambda i, j, kk: (seq_of(i), 0, col(i, j)))],
        out_specs=out_specs, out_shape=out_shape,
        compiler_params=_params(("arbitrary", "arbitrary", "arbitrary"), vmem + 4 * MIB),
        name="ffn_down",
    )(act, w_down, x, mod)


ROW_TILE = 1024
HALF_ROW_TILE = 512
COL_TILE_BF16 = 512
COL_TILE_F32 = 256
NORM_ROWS = 512
NORM_SEQS = 32


def _tiles(is_prompt, n, t):
    if is_prompt:
        tm = min(ROW_TILE, t)
        half = min(HALF_ROW_TILE, tm)
        return tm, half, half, COL_TILE_BF16, 1, 1, min(NORM_ROWS, t)
    assert t == SUBLANES
    tm = min(ROW_TILE, n * t)
    return tm, tm, tm, COL_TILE_F32, 2, min(NORM_SEQS, n), t


def _layer(x3, mod3, conv_state, ffn_state, cache_k, cache_v, w, is_prompt):
    n, t, d = x3.shape
    m = n * t
    attn_w = d // 2
    d_conv = d - attn_w
    n_q = attn_w // HEAD_DIM
    n_kv = max(1, n_q // GROUP)
    kv_dim = n_kv * HEAD_DIM
    off_b = attn_w + 2 * kv_dim
    x2 = x3.reshape(m, d)
    emit = not is_prompt

    tm, tm_conv, tm_down, tn, nk_down, bn, tt = _tiles(is_prompt, n, t)
    if is_prompt:
        w_conv, off_conv = (w["w_b"], w["w_c"], w["w_h"]), (0, 0, 0)
        w_act, off_act = (w["w_g"], w["w_u"]), (0, 0)
        w_oa, w_oc, out_rows = w["w_oa"], w["w_oc"], (0, 0)
        w_qkv, w_down = w["w_qkv"], w["w_d"]
        d_ff = w_down.shape[0]
    else:
        d_ff = w["w_down"].shape[0]
        w_conv, off_conv = (w["w_in"],) * 3, (off_b, off_b + d_conv, off_b + 2 * d_conv)
        w_act, off_act = (w["w_gate_up"],) * 2, (0, d_ff)
        w_oa = w_oc = w["w_out"]
        out_rows = (0, 1)
        w_qkv, w_down = w["w_in"], w["w_down"]

    h = _norm_mod(x3, mod3, w["g_mix"], 0, 1, bn, tt)
    qkv, *wb_qkv = _matmul(h, w_qkv, off_b, *((tm_down, off_b) if is_prompt else (tm, tn)), emit)
    conv, conv_aux, *wb_conv = _gated_proj(h, w_conv, off_conv, d_conv, w["conv_w"], conv_state, t, tm_conv,
                                           tn, emit, "conv_gate")
    if is_prompt:
        attn = _prompt_attention(qkv, w["sinks"], n, t, attn_w, kv_dim)
    else:
        sink_lanes = jnp.repeat(w["sinks"].reshape(n_kv // 2, 1, 2 * GROUP), t, axis=2)
        attn = _sample_attention(qkv, cache_k.reshape(n, WINDOW, kv_dim), cache_v.reshape(n, WINDOW, kv_dim),
                                 sink_lanes, n, t, attn_w, kv_dim, bn=min(16, n))
    x1, *wb_out = _out_proj(attn, conv, w_oa, w_oc, out_rows, x2, mod3, 2, t, tm, tn, emit)

    h2 = _norm_mod(x1.reshape(n, t, d), mod3, w["g_ffn"], 3, 4, bn, tt)
    act, ffn_aux, *wb_act = _gated_proj(h2, w_act, off_act, d_ff, w["ffn_conv_w"], ffn_state, t, tm, tn, emit,
                                        "ffn_act")
    xo, *wb_down = _ffn_down(act, w_down, x1, mod3, 5, t, tm_down, tn, nk_down, emit)

    keep = min(WINDOW, t)
    kv_new = qkv.reshape(n, t, off_b)[:, t - keep:, attn_w:]
    k_new = kv_new[:, :, :kv_dim].reshape(n, keep, n_kv, HEAD_DIM)
    v_new = kv_new[:, :, kv_dim:].reshape(n, keep, n_kv, HEAD_DIM)
    if is_prompt:
        k_state, v_state = k_new, v_new
        conv_state_out = conv_aux[t // tm_conv - 1::t // tm_conv, SUBLANES - (CONV_K - 1):, :]
        ffn_state_out = ffn_aux[t // tm - 1::t // tm, SUBLANES - (CONV_K - 1):, :]
        return xo, (k_state, v_state, conv_state_out, ffn_state_out)
    k_state = jnp.concatenate([cache_k, k_new], axis=1)[:, t:]
    v_state = jnp.concatenate([cache_v, v_new], axis=1)[:, t:]
    wb = dict(w, w_qkv=wb_qkv[0], w_b=wb_conv[0], w_c=wb_conv[1], w_h=wb_conv[2], w_oa=wb_out[0], w_oc=wb_out[1],
              w_g=wb_act[0], w_u=wb_act[1], w_d=wb_down[0])
    return xo, (k_state, v_state, conv_aux, ffn_aux), wb


def kernel(x_prompt, x_sample, cache_k_win, cache_v_win, state_conv, state_ffn_conv, c_prompt, c_sample, w_mod, b_mod, g_mix, g_ffn, w_in, conv_w, sinks, w_out, w_gate_up, ffn_conv_w, w_down, g_final):
    depth = w_mod.shape[0]
    np_, tp, d = x_prompt.shape
    ns, ts, _ = x_sample.shape
    assert tp % BLOCK == 0 and cache_k_win.shape[2] == WINDOW

    n_c = np_ + ns
    n_c_pad = -(-n_c // SUBLANES) * SUBLANES
    c_all = jnp.concatenate([c_sample, c_prompt, jnp.zeros((n_c_pad - n_c, d), F32)], axis=0)

    xp, xs = x_prompt, x_sample
    outs_p, outs_s = [], []
    for l in range(depth):
        w = dict(g_mix=g_mix[l], g_ffn=g_ffn[l], conv_w=conv_w[l], sinks=sinks[l], ffn_conv_w=ffn_conv_w[l],
                 w_in=w_in[l], w_out=w_out[l], w_gate_up=w_gate_up[l], w_down=w_down[l])
        mod = _mod(c_all, w_mod[l], b_mod[l])
        xs2, st_s, wb = _layer(xs, (mod, 0), state_conv[l], state_ffn_conv[l], cache_k_win[l], cache_v_win[l],
                               w, False)
        xp2, st_p = _layer(xp, (mod, ns), None, None, None, None, wb, True)
        xp, xs = xp2.reshape(xp.shape), xs2.reshape(xs.shape)
        outs_p.append(st_p)
        outs_s.append(st_s)

    y_prompt = _final_norm(xp.reshape(np_ * tp, d), g_final).reshape(np_, tp, d)
    y_sample = _final_norm(xs.reshape(ns * ts, d), g_final).reshape(ns, ts, d)
    stack = lambda outs, k: jnp.stack([o[k] for o in outs])
    return (y_prompt, y_sample,
            stack(outs_p, 0), stack(outs_p, 1), stack(outs_p, 2), stack(outs_p, 3),
            stack(outs_s, 0), stack(outs_s, 1), stack(outs_s, 2), stack(outs_s, 3))
```
